```python
import jax, jax.numpy as jnp
from jax import lax
import numpy as np

D_MODEL = 1024
BATCH = 8
SEQ = 8192
DEPTH = 2

N_MIXERS = 2
EPS = 1e-6
GLA_HEADS = 4
GLA_DK = D_MODEL // 2 // GLA_HEADS
GLA_DV = D_MODEL // GLA_HEADS
GLA_GATE_RANK = 16
GLA_GATE_TAU = 16.0
GLA_CHUNK = 64
GLA_IN = 2 * GLA_HEADS * GLA_DK + GLA_HEADS * GLA_DV + GLA_GATE_RANK + GLA_HEADS * GLA_DV
MLA_HEADS = 8
MLA_NOPE = 128
MLA_ROPE = 64
MLA_V = D_MODEL // MLA_HEADS
MLA_Q_LORA = D_MODEL // 4
MLA_KV_LORA = D_MODEL // 8
MLA_IN = MLA_Q_LORA + MLA_KV_LORA + MLA_ROPE
MLA_SCALE = (MLA_NOPE + MLA_ROPE) ** -0.5
ROPE_THETA = 10000.0
Q_BLOCK = 128
MOE_GROUPS = 8
MOE_PER_GROUP = 8
MOE_EXPERTS = MOE_GROUPS * MOE_PER_GROUP
MOE_TOPK = 2
MOE_FF = D_MODEL // 4
MOE_BLOCK = 128

kernel_name = 'hybrid_gla_mla_hier_moe'


def rmsnorm(x, g):
    x32 = x.astype(jnp.float32)
    y = x32 * lax.rsqrt(jnp.mean(x32 * x32, axis=-1, keepdims=True) + EPS)
    return (y * g.astype(jnp.float32)).astype(x.dtype)


def rope_tables(positions):
    inv_freq = 1.0 / (ROPE_THETA ** (jnp.arange(0, MLA_ROPE, 2, dtype=jnp.float32) / MLA_ROPE))
    ang = positions.astype(jnp.float32)[..., None] * inv_freq
    return jnp.cos(ang), jnp.sin(ang)


def apply_rope(t, cos, sin):
    t32 = t.astype(jnp.float32)
    t1, t2 = jnp.split(t32, 2, axis=-1)
    return jnp.concatenate([t1 * cos - t2 * sin, t2 * cos + t1 * sin], axis=-1).astype(t.dtype)


def gla_mixer(h, w_in, w_gate, b_gate, out_norm, w_o):
    bsz, seq, _ = h.shape
    nc = seq // GLA_CHUNK
    qk = GLA_HEADS * GLA_DK
    vd = GLA_HEADS * GLA_DV
    proj = h @ w_in
    q, k, v, a_lr, r = jnp.split(proj, [qk, 2 * qk, 2 * qk + vd, 2 * qk + vd + GLA_GATE_RANK], axis=-1)
    log_a = jax.nn.log_sigmoid((a_lr @ w_gate + b_gate).astype(jnp.float32)) / GLA_GATE_TAU

    def chunks(t, d):
        return t.reshape(bsz, nc, GLA_CHUNK, GLA_HEADS, d).transpose(0, 3, 1, 2, 4).astype(jnp.float32)

    qc = chunks(q, GLA_DK) * (GLA_DK ** -0.5)
    kc = chunks(k, GLA_DK)
    vc = chunks(v, GLA_DV)
    g = jnp.cumsum(chunks(log_a, GLA_DK), axis=3)
    g_mid = g[:, :, :, GLA_CHUNK // 2:GLA_CHUNK // 2 + 1]
    g_last = g[:, :, :, -1:]
    causal = jnp.tril(jnp.ones((GLA_CHUNK, GLA_CHUNK), dtype=bool))
    a_intra = jnp.einsum('bhnid,bhnjd->bhnij', qc * jnp.exp(g - g_mid), kc * jnp.exp(g_mid - g))
    o_intra = jnp.einsum('bhnij,bhnjv->bhniv', jnp.where(causal, a_intra, 0.0), vc)
    q_inter = jnp.moveaxis(qc * jnp.exp(g), 2, 0)
    k_state = jnp.moveaxis(kc * jnp.exp(g_last - g), 2, 0)
    decay = jnp.moveaxis(jnp.exp(g_last[:, :, :, 0]), 2, 0)
    v_s = jnp.moveaxis(vc, 2, 0)

    def step(state, inp):
        q_i, k_i, v_i, d_i = inp
        o_i = jnp.einsum('bhid,bhdv->bhiv', q_i, state)
        state = state * d_i[..., None] + jnp.einsum('bhjd,bhjv->bhdv', k_i, v_i)
        return state, o_i

    s0 = jnp.zeros((bsz, GLA_HEADS, GLA_DK, GLA_DV), jnp.float32)
    _, o_inter = lax.scan(step, s0, (q_inter, k_state, v_s, decay))
    o = o_intra + jnp.moveaxis(o_inter, 0, 2)
    o = o.transpose(0, 2, 3, 1, 4).reshape(bsz, seq, GLA_HEADS, GLA_DV)
    o = o * lax.rsqrt(jnp.mean(o * o, axis=-1, keepdims=True) + EPS) * out_norm.astype(jnp.float32)
    o = o.reshape(bsz, seq, vd) * jax.nn.silu(r.astype(jnp.float32))
    return o.astype(h.dtype) @ w_o


def mla_mixer(h, cos, sin, w_in, q_norm, w_uq, kv_norm, w_uk, w_uv, w_o):
    bsz, seq, _ = h.shape
    nqb = seq // Q_BLOCK
    proj = h @ w_in
    c_q, c_kv, k_rope = jnp.split(proj, [MLA_Q_LORA, MLA_Q_LORA + MLA_KV_LORA], axis=-1)
    c_q = rmsnorm(c_q, q_norm)
    c_kv = rmsnorm(c_kv, kv_norm)
    q = jnp.einsum('bsc,chd->bshd', c_q, w_uq)
    q_nope, q_rope = q[..., :MLA_NOPE], q[..., MLA_NOPE:]
    q_rope = apply_rope(q_rope, cos[:, :, None, :], sin[:, :, None, :])
    k_rope = apply_rope(k_rope, cos, sin)
    q_lat = jnp.einsum('bshn,chn->bshc', q_nope, w_uk)
    q_cat = jnp.concatenate([q_lat, q_rope], axis=-1) * MLA_SCALE
    k_cat = jnp.concatenate([c_kv, k_rope], axis=-1)
    q_blocks = q_cat.reshape(bsz, nqb, Q_BLOCK, MLA_HEADS, MLA_KV_LORA + MLA_ROPE).transpose(1, 0, 2, 3, 4)
    k_pos = jnp.arange(seq)

    def attend(args):
        q_b, b_idx = args
        s = jnp.einsum('bqhc,bkc->bhqk', q_b, k_cat).astype(jnp.float32)
        q_pos = b_idx * Q_BLOCK + jnp.arange(Q_BLOCK)
        s = jnp.where(k_pos[None, :] <= q_pos[:, None], s, -jnp.inf)
        p = jax.nn.softmax(s, axis=-1).astype(c_kv.dtype)
        return jnp.einsum('bhqk,bkc->bqhc', p, c_kv)

    o_lat = lax.map(attend, (q_blocks, jnp.arange(nqb)))
    o_lat = o_lat.transpose(1, 0, 2, 3, 4).reshape(bsz, seq, MLA_HEADS, MLA_KV_LORA)
    o = jnp.einsum('bshc,chv->bshv', o_lat, w_uv).reshape(bsz, seq, MLA_HEADS * MLA_V)
    return o @ w_o


def hier_moe(h, w_group, w_expert, w1, w3, w2):
    bsz, seq, dm = h.shape
    n_tok = bsz * seq
    xf = h.reshape(n_tok, dm)
    tok = jnp.arange(n_tok)
    g_logits = (xf @ w_group).astype(jnp.float32)
    g_prob = jax.nn.softmax(g_logits, axis=-1)
    g_sel = jnp.argmax(g_logits, axis=-1)
    g_w = g_prob[tok, g_sel]
    e_logits = (xf @ w_expert).astype(jnp.float32).reshape(n_tok, MOE_GROUPS, MOE_PER_GROUP)
    e_prob = jax.nn.softmax(e_logits[tok, g_sel], axis=-1)
    top_p, top_i = lax.top_k(e_prob, MOE_TOPK)
    gates = g_w[:, None] * top_p / jnp.sum(top_p, axis=-1, keepdims=True)
    expert_id = g_sel[:, None] * MOE_PER_GROUP + top_i
    n_asg = n_tok * MOE_TOPK
    flat_e = expert_id.reshape(n_asg)
    flat_tok = jnp.arange(n_asg) // MOE_TOPK
    flat_gate = gates.reshape(n_asg)
    order = jnp.argsort(flat_e)
    se, stok, sgate = flat_e[order], flat_tok[order], flat_gate[order]
    counts = jax.ops.segment_sum(jnp.ones_like(flat_e), flat_e, num_segments=MOE_EXPERTS)
    starts = jnp.cumsum(counts) - counts
    padded = (counts + MOE_BLOCK - 1) // MOE_BLOCK * MOE_BLOCK
    pend = jnp.cumsum(padded)
    pstarts = pend - padded
    dest = pstarts[se] + (jnp.arange(n_asg) - starts[se])
    n_slots = n_asg + MOE_EXPERTS * MOE_BLOCK
    n_blocks = n_slots // MOE_BLOCK
    slot_tok = jnp.full((n_slots,), n_tok, dtype=jnp.int32).at[dest].set(stok.astype(jnp.int32))
    slot_gate = jnp.zeros((n_slots,), jnp.float32).at[dest].set(sgate)
    block_e = jnp.minimum(jnp.searchsorted(pend, jnp.arange(n_blocks) * MOE_BLOCK, side='right'), MOE_EXPERTS - 1)
    x_pad = jnp.concatenate([xf, jnp.zeros((1, dm), xf.dtype)], axis=0)
    xs = x_pad[slot_tok].reshape(n_blocks, MOE_BLOCK, dm)

    def expert_block(args):
        xb, e = args
        hid = jax.nn.silu(xb @ w1[e]) * (xb @ w3[e])
        return hid @ w2[e]

    ys = lax.map(expert_block, (xs, block_e)).reshape(n_slots, dm)
    ys = ys * slot_gate[:, None].astype(ys.dtype)
    out = jnp.zeros((n_tok + 1, dm), ys.dtype).at[slot_tok].add(ys)[:n_tok]
    return out.reshape(bsz, seq, dm)


def setup_inputs(seed: int = 0) -> dict:
    key = jax.random.key(seed)
    ks = jax.random.split(key, 24)
    f32 = jnp.float32
    n_gla = (DEPTH + N_MIXERS - 1) // N_MIXERS
    n_mla = DEPTH // N_MIXERS

    def nrm(k, shape, fan_in):
        return jax.random.normal(k, shape, f32) * (fan_in ** -0.5)

    def gain(k, shape):
        return 1.0 + 0.1 * jax.random.normal(k, shape, f32)

    x = jax.random.normal(ks[0], (BATCH, SEQ, D_MODEL), f32)
    offsets = jax.random.randint(ks[1], (BATCH, 1), 0, 4096, dtype=jnp.int32)
    positions = offsets + jnp.arange(SEQ, dtype=jnp.int32)[None, :]
    return {
        'x': x,
        'positions': positions,
        'attn_norm': gain(ks[2], (DEPTH, D_MODEL)),
        'ffn_norm': gain(ks[3], (DEPTH, D_MODEL)),
        'final_norm': gain(ks[4], (D_MODEL,)),
        'gla_w_in': nrm(ks[5], (n_gla, D_MODEL, GLA_IN), D_MODEL),
        'gla_w_gate': nrm(ks[6], (n_gla, GLA_GATE_RANK, GLA_HEADS * GLA_DK), GLA_GATE_RANK),
        'gla_b_gate': 0.1 * jax.random.normal(ks[7], (n_gla, GLA_HEADS * GLA_DK), f32),
        'gla_out_norm': gain(ks[8], (n_gla, GLA_DV)),
        'gla_w_o': nrm(ks[9], (n_gla, GLA_HEADS * GLA_DV, D_MODEL), GLA_HEADS * GLA_DV),
        'mla_w_in': nrm(ks[10], (n_mla, D_MODEL, MLA_IN), D_MODEL),
        'mla_q_norm': gain(ks[11], (n_mla, MLA_Q_LORA)),
        'mla_w_uq': nrm(ks[12], (n_mla, MLA_Q_LORA, MLA_HEADS, MLA_NOPE + MLA_ROPE), MLA_Q_LORA),
        'mla_kv_norm': gain(ks[13], (n_mla, MLA_KV_LORA)),
        'mla_w_uk': nrm(ks[14], (n_mla, MLA_KV_LORA, MLA_HEADS, MLA_NOPE), MLA_KV_LORA),
        'mla_w_uv': nrm(ks[15], (n_mla, MLA_KV_LORA, MLA_HEADS, MLA_V), MLA_KV_LORA),
        'mla_w_o': nrm(ks[16], (n_mla, MLA_HEADS * MLA_V, D_MODEL), MLA_HEADS * MLA_V),
        'moe_w_group': nrm(ks[17], (DEPTH, D_MODEL, MOE_GROUPS), D_MODEL),
        'moe_w_expert': nrm(ks[18], (DEPTH, D_MODEL, MOE_EXPERTS), D_MODEL),
        'moe_w1': nrm(ks[19], (DEPTH, MOE_EXPERTS, D_MODEL, MOE_FF), D_MODEL),
        'moe_w3': nrm(ks[20], (DEPTH, MOE_EXPERTS, D_MODEL, MOE_FF), D_MODEL),
        'moe_w2': nrm(ks[21], (DEPTH, MOE_EXPERTS, MOE_FF, D_MODEL), MOE_FF),
    }


def reference(x, positions, attn_norm, ffn_norm, final_norm,
              gla_w_in, gla_w_gate, gla_b_gate, gla_out_norm, gla_w_o,
              mla_w_in, mla_q_norm, mla_w_uq, mla_kv_norm, mla_w_uk, mla_w_uv, mla_w_o,
              moe_w_group, moe_w_expert, moe_w1, moe_w3, moe_w2):
    cos, sin = rope_tables(positions)
    for i in range(DEPTH):
        j = i // N_MIXERS
        h = rmsnorm(x, attn_norm[i])
        if i % N_MIXERS == 0:
            m = gla_mixer(h, gla_w_in[j], gla_w_gate[j], gla_b_gate[j], gla_out_norm[j], gla_w_o[j])
        else:
            m = mla_mixer(h, cos, sin, mla_w_in[j], mla_q_norm[j], mla_w_uq[j], mla_kv_norm[j],
                          mla_w_uk[j], mla_w_uv[j], mla_w_o[j])
        x = x + m.astype(x.dtype)
        h = rmsnorm(x, ffn_norm[i])
        x = x + hier_moe(h, moe_w_group[i], moe_w_expert[i], moe_w1[i], moe_w3[i], moe_w2[i]).astype(x.dtype)
    return rmsnorm(x, final_norm)
```

```python
import functools

import jax
import jax.numpy as jnp
from jax import lax
from jax.experimental import pallas as pl
from jax.experimental.pallas import tpu as pltpu

F32 = jnp.float32
BF16 = jnp.bfloat16
I32 = jnp.int32
HIGHEST = lax.Precision.HIGHEST

EPS = 1e-6
GLA_HEADS = 4
GLA_DK = 128
GLA_DV = 256
GLA_GATE_RANK = 16
GLA_GATE_TAU = 16.0
GLA_CHUNK = 64
MLA_HEADS = 8
MLA_NOPE = 128
MLA_ROPE = 64
MLA_V = 128
MLA_Q_LORA = 256
MLA_KV_LORA = 128
MLA_SCALE = (MLA_NOPE + MLA_ROPE) ** -0.5
ROPE_THETA = 10000.0
MOE_GROUPS = 8
MOE_PER_GROUP = 8
MOE_EXPERTS = MOE_GROUPS * MOE_PER_GROUP
MOE_FF = 256

D_MODEL = 1024
LANES = 128
ROW_SUB = D_MODEL // LANES
VMEM_LIMIT = 48 * 1024 * 1024

TOK_TILE = 512
GLA_BLOCK = 256
ATT_BLOCK = 256
ROW_TILE = 256
EXP_BLOCK = 256


def _cparams(*sem):
    return pltpu.CompilerParams(dimension_semantics=sem, vmem_limit_bytes=VMEM_LIMIT)


def _rms(x, w):
    return x * lax.rsqrt(jnp.mean(x * x, axis=-1, keepdims=True) + EPS) * w


def _dot(a, b):
    return jnp.dot(a, b, preferred_element_type=F32)


def _dot_nt(a, b):
    return lax.dot_general(a, b, (((1,), (1,)), ((), ())), preferred_element_type=F32)


def _dot_tn(a, b):
    return lax.dot_general(a, b, (((0,), (0,)), ((), ())), preferred_element_type=F32)


def _gla_in_kernel(x_ref, nw_ref, wq_ref, wk_ref, wv_ref, wr_ref, wa_ref, wg_ref, bg_ref,
                   q_ref, k_ref, v_ref, r_ref, la_ref):
    hb = _rms(x_ref[...], nw_ref[...]).astype(BF16)
    q_ref[...] = _dot(hb, wq_ref[...]) * (GLA_DK ** -0.5)
    k_ref[...] = _dot(hb, wk_ref[...])
    v_ref[...] = _dot(hb, wv_ref[...]).astype(BF16)
    r_ref[...] = _dot(hb, wr_ref[...])
    a_lr = _dot(hb, wa_ref[...])
    z = jnp.dot(a_lr, wg_ref[...], precision=HIGHEST, preferred_element_type=F32) + bg_ref[...]
    log_sig = jnp.minimum(z, 0.0) - jnp.log1p(jnp.exp(-jnp.abs(z)))
    la_ref[...] = log_sig * (1.0 / GLA_GATE_TAU)


def _gla_in(x, nw, w_in, w_gate, b_gate):
    t, d = x.shape
    qk = GLA_HEADS * GLA_DK
    vd = GLA_HEADS * GLA_DV
    wq = w_in[:, :qk].astype(BF16)
    wk = w_in[:, qk:2 * qk].astype(BF16)
    wv = w_in[:, 2 * qk:2 * qk + vd].astype(BF16)
    wa = jnp.pad(w_in[:, 2 * qk + vd:2 * qk + vd + GLA_GATE_RANK], ((0, 0), (0, LANES - GLA_GATE_RANK))).astype(BF16)
    wr = w_in[:, 2 * qk + vd + GLA_GATE_RANK:].astype(BF16)
    wg = jnp.pad(w_gate, ((0, LANES - GLA_GATE_RANK), (0, 0)))
    tm = TOK_TILE
    row = lambda n: pl.BlockSpec((tm, n), lambda i: (i, 0))
    full = lambda a: pl.BlockSpec(a.shape, lambda i: (0, 0))
    nw2, bg2 = nw.reshape(1, d), b_gate.reshape(1, qk)
    return pl.pallas_call(
        _gla_in_kernel,
        grid=(t // tm,),
        in_specs=[row(d), full(nw2), full(wq), full(wk), full(wv), full(wr), full(wa), full(wg), full(bg2)],
        out_specs=[row(qk), row(qk), row(vd), row(vd), row(qk)],
        out_shape=[jax.ShapeDtypeStruct((t, qk), F32), jax.ShapeDtypeStruct((t, qk), F32),
                   jax.ShapeDtypeStruct((t, vd), BF16), jax.ShapeDtypeStruct((t, vd), F32),
                   jax.ShapeDtypeStruct((t, qk), F32)],
        compiler_params=_cparams("arbitrary"),
        name="gla_in",
    )(x, nw2, wq, wk, wv, wr, wa, wg, bg2)


def _gla_rec_kernel(q_ref, k_ref, v_ref, r_ref, la_ref, onw_ref, o_ref, st_ref):
    c = GLA_CHUNK

    @pl.when(pl.program_id(1) == 0)
    def _():
        st_ref[...] = jnp.zeros_like(st_ref)

    rows = lax.broadcasted_iota(I32, (c, c), 0)
    cols = lax.broadcasted_iota(I32, (c, c), 1)
    causal = cols <= rows
    tril = causal.astype(F32)
    onw = onw_ref[...]
    for ci in range(GLA_BLOCK // c):
        sl = pl.ds(ci * c, c)
        g_all = jnp.dot(tril, la_ref[sl, :], precision=HIGHEST, preferred_element_type=F32)
        for h in range(GLA_HEADS):
            ks = pl.ds(h * GLA_DK, GLA_DK)
            vs = pl.ds(h * GLA_DV, GLA_DV)
            g = g_all[:, h * GLA_DK:(h + 1) * GLA_DK]
            g_mid = g[c // 2:c // 2 + 1]
            g_last = g[c - 1:c]
            q = q_ref[sl, ks]
            k = k_ref[sl, ks]
            v = v_ref[sl, vs]
            a = _dot_nt((q * jnp.exp(g - g_mid)).astype(BF16), (k * jnp.exp(g_mid - g)).astype(BF16))
            a = jnp.where(causal, a, 0.0).astype(BF16)
            st = st_ref[h]
            o = _dot(a, v) + _dot_nt((q * jnp.exp(g)).astype(BF16), st.astype(BF16))
            k_state = (k * jnp.exp(g_last - g)).astype(BF16)
            st_ref[h] = st * jnp.exp(g_last) + _dot_tn(v, k_state)
            o = _rms(o, onw)
            r = r_ref[sl, vs]
            o_ref[sl, vs] = (o * (r * jax.nn.sigmoid(r))).astype(BF16)


def _gla_rec(q, k, v, r, la, onw, bsz, seq):
    t = bsz * seq
    qk = GLA_HEADS * GLA_DK
    vd = GLA_HEADS * GLA_DV
    nb = seq // GLA_BLOCK
    row = lambda n: pl.BlockSpec((GLA_BLOCK, n), lambda b, j: (b * nb + j, 0))
    onw2 = onw.reshape(1, GLA_DV)
    return pl.pallas_call(
        _gla_rec_kernel,
        grid=(bsz, nb),
        in_specs=[row(qk), row(qk), row(vd), row(vd), row(qk), pl.BlockSpec((1, GLA_DV), lambda b, j: (0, 0))],
        out_specs=row(vd),
        out_shape=jax.ShapeDtypeStruct((t, vd), BF16),
        scratch_shapes=[pltpu.VMEM((GLA_HEADS, GLA_DV, GLA_DK), F32)],
        compiler_params=_cparams("arbitrary", "arbitrary"),
        name="gla_rec",
    )(q, k, v, r, la, onw2)


def _store_rows(ref, val):
    n, d = val.shape
    nc = d // LANES
    for c in range(nc):
        ref[pl.ds(c, n, stride=nc), :] = val[:, c * LANES:(c + 1) * LANES]


def _load_rows(ref, start, n, nc, dtype):
    return jnp.concatenate([ref[pl.ds(start * nc + c, n, stride=nc), :].astype(dtype) for c in range(nc)], axis=1)


def _post_kernel(a_ref, wo_ref, x_ref, nw_ref, wr_ref,
                 x1_ref, h_ref, ids_ref, gates_ref, cnt_ref, carry_ref):
    tm = a_ref.shape[0]
    ne = MOE_EXPERTS

    @pl.when(pl.program_id(0) == 0)
    def _():
        carry_ref[...] = jnp.zeros_like(carry_ref)

    x1 = x_ref[...] + _dot(a_ref[...], wo_ref[...])
    x1_ref[...] = x1
    h = _rms(x1, nw_ref[...])
    _store_rows(h_ref, h)
    lt = lax.dot_general(wr_ref[...], h, (((1,), (1,)), ((), ())), precision=HIGHEST,
                         preferred_element_type=F32)
    gl = lt[0:MOE_GROUPS]
    gmax = jnp.max(gl, axis=0, keepdims=True)
    gi = lax.broadcasted_iota(I32, gl.shape, 0)
    g_sel = jnp.min(jnp.where(gl == gmax, gi, MOE_GROUPS), axis=0, keepdims=True)
    g_w = 1.0 / jnp.sum(jnp.exp(gl - gmax), axis=0, keepdims=True)
    el = lt[MOE_GROUPS:MOE_GROUPS + ne]
    ei = lax.broadcasted_iota(I32, el.shape, 0)
    in_group = (ei // MOE_PER_GROUP) == g_sel
    neg = jnp.float32(-jnp.inf)
    el1 = jnp.where(in_group, el, neg)
    l1 = jnp.max(el1, axis=0, keepdims=True)
    i1 = jnp.min(jnp.where(el1 == l1, ei, ne), axis=0, keepdims=True)
    el2 = jnp.where(ei == i1, neg, el1)
    l2 = jnp.max(el2, axis=0, keepdims=True)
    i2 = jnp.min(jnp.where(el2 == l2, ei, ne), axis=0, keepdims=True)
    e2 = jnp.exp(l2 - l1)
    gate1 = g_w / (1.0 + e2)
    gate2 = g_w * e2 / (1.0 + e2)
    oh1 = ei == i1
    oh2 = ei == i2
    both = jnp.where(oh1 | oh2, 1.0, 0.0)
    su = lax.broadcasted_iota(I32, (tm, tm), 0)
    tu = lax.broadcasted_iota(I32, (tm, tm), 1)
    upper = jnp.where(su < tu, 1.0, 0.0).astype(BF16)
    prefix = _dot(both.astype(BF16), upper) + carry_ref[:, 0:1]
    rank1 = jnp.sum(jnp.where(oh1, prefix, 0.0), axis=0, keepdims=True)
    rank2 = jnp.sum(jnp.where(oh2, prefix, 0.0), axis=0, keepdims=True)
    carry = carry_ref[...] + jnp.sum(both, axis=1, keepdims=True)
    carry_ref[...] = carry
    cnt_ref[...] = carry.astype(I32)
    ids_ref[...] = jnp.concatenate([i1, i2, rank1.astype(I32), rank2.astype(I32)], axis=0)
    gates_ref[...] = jnp.concatenate([gate1, gate2], axis=0)


def _post(a, wo, x, nw, w_group, w_expert):
    t, d = x.shape
    tm = TOK_TILE
    wr = jnp.concatenate([w_group.T, w_expert.T], axis=0)
    wr = jnp.pad(wr, ((0, LANES - wr.shape[0]), (0, 0)))
    wo = wo.astype(BF16)
    nw2 = nw.reshape(1, d)
    row = lambda n: pl.BlockSpec((tm, n), lambda i: (i, 0))
    full = lambda arr: pl.BlockSpec(arr.shape, lambda i: (0, 0))
    return pl.pallas_call(
        _post_kernel,
        grid=(t // tm,),
        in_specs=[row(a.shape[1]), full(wo), row(d), full(nw2), full(wr)],
        out_specs=[row(d), pl.BlockSpec((tm * d // LANES, LANES), lambda i: (i, 0)),
                   pl.BlockSpec((4, tm), lambda i: (0, i)), pl.BlockSpec((2, tm), lambda i: (0, i)),
                   pl.BlockSpec((MOE_EXPERTS, LANES), lambda i: (0, 0))],
        out_shape=[jax.ShapeDtypeStruct((t, d), F32), jax.ShapeDtypeStruct((t * d // LANES, LANES), F32),
                   jax.ShapeDtypeStruct((4, t), I32), jax.ShapeDtypeStruct((2, t), F32),
                   jax.ShapeDtypeStruct((MOE_EXPERTS, LANES), I32)],
        scratch_shapes=[pltpu.VMEM((MOE_EXPERTS, LANES), F32)],
        compiler_params=_cparams("arbitrary"),
        name="post_router",
    )(a, wo, x, nw2, wr)


def _row(ref, i):
    return ref.at[pl.ds(pl.multiple_of(i * ROW_SUB, ROW_SUB), ROW_SUB)]


def _zero_fill(zinfo_ref, zero_ref, xs_hbm, sem, wait):
    ne = MOE_EXPERTS
    nblk = xs_hbm.shape[0] // (EXP_BLOCK * ROW_SUB)

    def copy(off_rows, n_rows):
        off = pl.multiple_of(off_rows * ROW_SUB, ROW_SUB)
        cp = pltpu.make_async_copy(zero_ref.at[pl.ds(0, n_rows * ROW_SUB)],
                                   xs_hbm.at[pl.ds(off, n_rows * ROW_SUB)], sem)
        cp.wait() if wait else cp.start()

    def expert(e, carry):
        pad = zinfo_ref[ne + e]
        for b in range(EXP_BLOCK.bit_length() - 1):
            @pl.when(((pad >> b) & 1) == 1)
            def _():
                copy(zinfo_ref[e] + (pad & ((1 << b) - 1)), 1 << b)
        return carry

    def tail(kk, carry):
        blk = zinfo_ref[2 * ne] + kk

        @pl.when(blk < nblk)
        def _():
            copy(blk * EXP_BLOCK, EXP_BLOCK)
        return carry

    lax.fori_loop(0, ne, expert, 0)
    lax.fori_loop(0, ne + 1, tail, 0)


def _dispatch_kernel(zinfo_ref, dest_hbm, h_ref, xs_hbm, idx_ref, zero_ref, sem_idx, sem_rows):
    i = pl.program_id(0)
    tr = h_ref.shape[0] // ROW_SUB

    @pl.when(i == 0)
    def _():
        zero_ref[...] = jnp.zeros_like(zero_ref)
        _zero_fill(zinfo_ref, zero_ref, xs_hbm, sem_rows, wait=False)
        _zero_fill(zinfo_ref, zero_ref, xs_hbm, sem_rows, wait=True)

    idx_cp = pltpu.make_async_copy(dest_hbm.at[i], idx_ref, sem_idx)
    idx_cp.start()
    idx_cp.wait()

    def start(t, carry):
        pltpu.make_async_copy(_row(h_ref, t), _row(xs_hbm, idx_ref[t]), sem_rows).start()
        pltpu.make_async_copy(_row(h_ref, t), _row(xs_hbm, idx_ref[tr + t]), sem_rows).start()
        return carry

    def wait(t, carry):
        pltpu.make_async_copy(_row(h_ref, 0), _row(xs_hbm, 0), sem_rows).wait()
        return carry

    lax.fori_loop(0, tr, start, 0)
    lax.fori_loop(0, 2 * tr, wait, 0)


def _dispatch(h, dest, zoff, n_slots):
    t = h.shape[0] // ROW_SUB
    tr = ROW_TILE
    nt = t // tr
    dest_t = dest.reshape(2, nt, tr).transpose(1, 0, 2).reshape(nt, 2 * tr)
    grid_spec = pltpu.PrefetchScalarGridSpec(
        num_scalar_prefetch=1,
        grid=(nt,),
        in_specs=[pl.BlockSpec(memory_space=pl.ANY), pl.BlockSpec((tr * ROW_SUB, LANES), lambda i, z: (i, 0))],
        out_specs=pl.BlockSpec(memory_space=pl.ANY),
        scratch_shapes=[pltpu.SMEM((2 * tr,), I32), pltpu.VMEM((EXP_BLOCK * ROW_SUB, LANES), F32),
                        pltpu.SemaphoreType.DMA, pltpu.SemaphoreType.DMA],
    )
    return pl.pallas_call(
        _dispatch_kernel,
        grid_spec=grid_spec,
        out_shape=jax.ShapeDtypeStruct((n_slots * ROW_SUB, LANES), F32),
        compiler_params=_cparams("arbitrary"),
        name="moe_dispatch",
    )(zoff, dest_t, h)


def _expert_kernel(be_ref, nused_ref, xs_ref, w1_ref, w3_ref, w2_ref, ys_ref):
    @pl.when(pl.program_id(0) < nused_ref[0])
    def _():
        xb = _load_rows(xs_ref, 0, EXP_BLOCK, ROW_SUB, BF16)
        a = _dot(xb, w1_ref[0])
        b = _dot(xb, w3_ref[0])
        hid = (a * jax.nn.sigmoid(a) * b).astype(BF16)
        _store_rows(ys_ref, _dot(hid, w2_ref[0]))

    @pl.when(pl.program_id(0) >= nused_ref[0])
    def _():
        ys_ref[...] = jnp.zeros_like(ys_ref)


def _experts(xs, block_e, n_used, w1, w3, w2):
    bm = EXP_BLOCK
    n_slots = xs.shape[0] // ROW_SUB
    nblk = n_slots // bm
    d, ff = w1.shape[-2:]

    def xmap(i, be, nu):
        return (jnp.minimum(i, nu[0] - 1), 0)

    def wmap(i, be, nu):
        return (be[jnp.minimum(i, nu[0] - 1)], 0, 0)

    grid_spec = pltpu.PrefetchScalarGridSpec(
        num_scalar_prefetch=2,
        grid=(nblk,),
        in_specs=[pl.BlockSpec((bm * ROW_SUB, LANES), xmap), pl.BlockSpec((1, d, ff), wmap),
                  pl.BlockSpec((1, d, ff), wmap), pl.BlockSpec((1, ff, d), wmap)],
        out_specs=pl.BlockSpec((bm * ROW_SUB, LANES), lambda i, be, nu: (i, 0)),
    )
    return pl.pallas_call(
        _expert_kernel,
        grid_spec=grid_spec,
        out_shape=jax.ShapeDtypeStruct((n_slots * ROW_SUB, LANES), F32),
        compiler_params=_cparams("arbitrary"),
        name="moe_experts",
    )(block_e, n_used, xs, w1.astype(BF16), w3.astype(BF16), w2.astype(BF16))


def _combine_kernel(dest_hbm, ys_hbm, x_ref, g_ref, nw_ref, o_ref, idx_ref, buf_ref, sem_idx, sem_rows,
                    *, final_norm):
    i = pl.program_id(0)
    tr = x_ref.shape[0]
    idx_cp = pltpu.make_async_copy(dest_hbm.at[i], idx_ref, sem_idx)
    idx_cp.start()
    idx_cp.wait()

    def start(t, carry):
        pltpu.make_async_copy(_row(ys_hbm, idx_ref[t]), _row(buf_ref, t), sem_rows).start()
        return carry

    def wait(t, carry):
        pltpu.make_async_copy(_row(ys_hbm, 0), _row(buf_ref, 0), sem_rows).wait()
        return carry

    lax.fori_loop(0, 2 * tr, start, 0)
    lax.fori_loop(0, 2 * tr, wait, 0)
    g = g_ref[...]
    y1 = _load_rows(buf_ref, 0, tr, ROW_SUB, F32)
    y2 = _load_rows(buf_ref, tr, tr, ROW_SUB, F32)
    out = x_ref[...] + g[:, 0:1] * y1 + g[:, 1:2] * y2
    if final_norm:
        out = _rms(out, nw_ref[...])
    o_ref[...] = out


def _combine(ys, dest, x, gates, nw, final_norm):
    t, d = x.shape
    tr = ROW_TILE
    nt = t // tr
    dest_t = dest.reshape(2, nt, tr).transpose(1, 0, 2).reshape(nt, 2 * tr)
    g_col = gates.T
    nw2 = nw.reshape(1, d)
    return pl.pallas_call(
        functools.partial(_combine_kernel, final_norm=final_norm),
        grid=(nt,),
        in_specs=[pl.BlockSpec(memory_space=pl.ANY), pl.BlockSpec(memory_space=pl.ANY),
                  pl.BlockSpec((tr, d), lambda i: (i, 0)), pl.BlockSpec((tr, 2), lambda i: (i, 0)),
                  pl.BlockSpec((1, d), lambda i: (0, 0))],
        out_specs=pl.BlockSpec((tr, d), lambda i: (i, 0)),
        out_shape=jax.ShapeDtypeStruct((t, d), F32),
        scratch_shapes=[pltpu.SMEM((2 * tr,), I32), pltpu.VMEM((2 * tr * ROW_SUB, LANES), F32),
                        pltpu.SemaphoreType.DMA, pltpu.SemaphoreType.DMA],
        compiler_params=_cparams("arbitrary"),
        name="moe_combine",
    )(dest_t, ys, x, g_col, nw2)


def _moe(h, x1, ids, gates, counts, w1, w3, w2, nw_final, final_norm):
    t = x1.shape[0]
    bm = EXP_BLOCK
    n_asg = 2 * t
    n_slots = n_asg + MOE_EXPERTS * bm
    cnt = counts[:, 0]
    padded = (cnt + bm - 1) // bm * bm
    pend = jnp.cumsum(padded)
    pstarts = pend - padded
    dest = pstarts[ids[0:2]] + ids[2:4]
    nblk = n_slots // bm
    block_e = jnp.minimum(jnp.searchsorted(pend, jnp.arange(nblk, dtype=I32) * bm, side="right"),
                          MOE_EXPERTS - 1).astype(I32)
    n_used = (pend[-1:] // bm).astype(I32)
    zinfo = jnp.concatenate([pstarts + cnt, padded - cnt, n_used]).astype(I32)
    xs = _dispatch(h, dest.astype(I32), zinfo, n_slots)
    ys = _experts(xs, block_e, n_used, w1, w3, w2)
    return _combine(ys, dest.astype(I32), x1, gates, nw_final, final_norm)


def _mla_in_kernel(x_ref, pos_ref, nw_ref, win_ref, wks_ref, qn_ref, kvn_ref, wqn_ref, wqr_ref, wqs_ref, wuk_ref,
                   freq_ref, q_ref, k_ref):
    hb = _rms(x_ref[...], nw_ref[...]).astype(BF16)
    proj = _dot(hb, win_ref[...])
    k_sw = _dot(hb, wks_ref[...])
    c_q = _rms(proj[:, :MLA_Q_LORA], qn_ref[...]).astype(BF16)
    c_kv = _rms(proj[:, MLA_Q_LORA:MLA_Q_LORA + MLA_KV_LORA], kvn_ref[...])
    ang = pos_ref[...].astype(F32) * freq_ref[...]
    cos = jnp.cos(ang)
    sin = jnp.sin(ang) * jnp.where(lax.broadcasted_iota(I32, ang.shape, 1) < MLA_ROPE // 2, -1.0, 1.0)
    k_rope = proj[:, MLA_Q_LORA + MLA_KV_LORA:] * cos + k_sw * sin
    k_ref[...] = jnp.concatenate([c_kv, k_rope], axis=1).astype(BF16)
    q_nope = _dot(c_q, wqn_ref[...]).astype(BF16)
    q_r = _dot(c_q, wqr_ref[...])
    q_s = _dot(c_q, wqs_ref[...])
    for h in range(MLA_HEADS):
        sl = slice(h * LANES, (h + 1) * LANES)
        q_lat = _dot_nt(q_nope[:, sl], wuk_ref[h])
        q_rope = q_r[:, sl] * cos + q_s[:, sl] * sin
        q_ref[:, 2 * h * LANES:(2 * h + 1) * LANES] = (q_lat * MLA_SCALE).astype(BF16)
        q_ref[:, (2 * h + 1) * LANES:(2 * h + 2) * LANES] = (q_rope * MLA_SCALE).astype(BF16)


def _swap_halves(w):
    half = w.shape[-1] // 2
    return jnp.concatenate([w[..., half:], w[..., :half]], axis=-1)


def _mla_in(x, pos, nw, w_in, q_norm, w_uq, kv_norm, w_uk):
    t, d = x.shape
    tm = TOK_TILE
    nh = MLA_HEADS
    pad_r = LANES - MLA_ROPE
    w_kr = w_in[:, MLA_Q_LORA + MLA_KV_LORA:]
    win = jnp.concatenate([w_in, jnp.zeros((d, pad_r), F32)], axis=1).astype(BF16)
    wks = jnp.pad(_swap_halves(w_kr), ((0, 0), (0, pad_r))).astype(BF16)
    wqn = w_uq[:, :, :MLA_NOPE].reshape(MLA_Q_LORA, nh * MLA_NOPE).astype(BF16)
    w_r = w_uq[:, :, MLA_NOPE:]
    wqr = jnp.pad(w_r, ((0, 0), (0, 0), (0, pad_r))).reshape(MLA_Q_LORA, nh * LANES).astype(BF16)
    wqs = jnp.pad(_swap_halves(w_r), ((0, 0), (0, 0), (0, pad_r))).reshape(MLA_Q_LORA, nh * LANES).astype(BF16)
    wuk = w_uk.transpose(1, 0, 2).astype(BF16)
    inv_freq = 1.0 / (ROPE_THETA ** (jnp.arange(0, MLA_ROPE, 2, dtype=F32) / MLA_ROPE))
    freq = jnp.pad(jnp.concatenate([inv_freq, inv_freq]), (0, pad_r)).reshape(1, LANES)
    row = lambda n: pl.BlockSpec((tm, n), lambda i: (i, 0))
    full = lambda a: pl.BlockSpec(a.shape, lambda i: (0,) * a.ndim)
    args = (x, pos.reshape(t, 1), nw.reshape(1, d), win, wks, q_norm.reshape(1, -1), kv_norm.reshape(1, -1),
            wqn, wqr, wqs, wuk, freq)
    return pl.pallas_call(
        _mla_in_kernel,
        grid=(t // tm,),
        in_specs=[row(d), row(1)] + [full(a) for a in args[2:]],
        out_specs=[row(2 * nh * LANES), row(2 * LANES)],
        out_shape=[jax.ShapeDtypeStruct((t, 2 * nh * LANES), BF16), jax.ShapeDtypeStruct((t, 2 * LANES), BF16)],
        compiler_params=_cparams("arbitrary"),
        name="mla_in",
    )(*args)


def _mla_attn_kernel(q_ref, k_ref, wuv_ref, o_ref, qs_ref, m_ref, l_ref, acc_ref):
    i = pl.program_id(1)
    tq = ATT_BLOCK
    nh = MLA_HEADS
    for h in range(nh):
        qs_ref[h * tq:(h + 1) * tq, :] = q_ref[:, 2 * h * LANES:(2 * h + 2) * LANES]
    m_ref[...] = jnp.full_like(m_ref, -jnp.inf)
    l_ref[...] = jnp.zeros_like(l_ref)
    acc_ref[...] = jnp.zeros_like(acc_ref)

    def block(j, masked):
        kb = k_ref[pl.ds(pl.multiple_of(j * tq, tq), tq), :]
        s = _dot_nt(qs_ref[...], kb)
        if masked:
            qpos = lax.broadcasted_iota(I32, (tq, tq), 0)
            kpos = lax.broadcasted_iota(I32, (tq, tq), 1)
            keep = jnp.concatenate([kpos <= qpos] * nh, axis=0)
            s = jnp.where(keep, s, -jnp.inf)
        m_old = m_ref[...]
        m_new = jnp.maximum(m_old, jnp.max(s, axis=-1, keepdims=True))
        alpha = jnp.exp(m_old - m_new)
        p = jnp.exp(s - m_new)
        l_ref[...] = alpha * l_ref[...] + jnp.sum(p, axis=-1, keepdims=True)
        acc_ref[...] = alpha * acc_ref[...] + _dot(p.astype(BF16), kb[:, :MLA_KV_LORA])
        m_ref[...] = m_new

    def full_block(j, carry):
        block(j, False)
        return carry

    lax.fori_loop(0, i, full_block, 0)
    block(i, True)
    for h in range(nh):
        sl = slice(h * tq, (h + 1) * tq)
        o_lat = (acc_ref[sl, :] / l_ref[sl, :]).astype(BF16)
        o_ref[:, h * MLA_V:(h + 1) * MLA_V] = _dot(o_lat, wuv_ref[h]).astype(BF16)


def _mla_attn(q, k, w_uv, bsz, seq):
    t = bsz * seq
    tq = ATT_BLOCK
    nq = seq // tq
    nh = MLA_HEADS
    wuv = w_uv.transpose(1, 0, 2).astype(BF16)
    return pl.pallas_call(
        _mla_attn_kernel,
        grid=(bsz, nq),
        in_specs=[pl.BlockSpec((tq, 2 * nh * LANES), lambda b, i: (b * nq + i, 0)),
                  pl.BlockSpec((seq, 2 * LANES), lambda b, i: (b, 0)),
                  pl.BlockSpec(wuv.shape, lambda b, i: (0, 0, 0))],
        out_specs=pl.BlockSpec((tq, nh * MLA_V), lambda b, i: (b * nq + i, 0)),
        out_shape=jax.ShapeDtypeStruct((t, nh * MLA_V), BF16),
        scratch_shapes=[pltpu.VMEM((nh * tq, 2 * LANES), BF16), pltpu.VMEM((nh * tq, 1), F32),
                        pltpu.VMEM((nh * tq, 1), F32), pltpu.VMEM((nh * tq, MLA_KV_LORA), F32)],
        compiler_params=_cparams("arbitrary", "arbitrary"),
        name="mla_attn",
    )(q, k, wuv)


def kernel(x, positions, attn_norm, ffn_norm, final_norm, gla_w_in, gla_w_gate, gla_b_gate, gla_out_norm, gla_w_o,
           mla_w_in, mla_q_norm, mla_w_uq, mla_kv_norm, mla_w_uk, mla_w_uv, mla_w_o,
           moe_w_group, moe_w_expert, moe_w1, moe_w3, moe_w2):
    bsz, seq, d = x.shape
    t = bsz * seq
    depth = attn_norm.shape[0]
    xf = x.reshape(t, d)
    pos = positions.reshape(t)
    for i in range(depth):
        j = i // 2
        if i % 2 == 0:
            q, k, v, r, la = _gla_in(xf, attn_norm[i], gla_w_in[j], gla_w_gate[j], gla_b_gate[j])
            mixed = _gla_rec(q, k, v, r, la, gla_out_norm[j], bsz, seq)
            w_o = gla_w_o[j]
        else:
            qc, kc = _mla_in(xf, pos, attn_norm[i], mla_w_in[j], mla_q_norm[j], mla_w_uq[j], mla_kv_norm[j],
                             mla_w_uk[j])
            mixed = _mla_attn(qc, kc, mla_w_uv[j], bsz, seq)
            w_o = mla_w_o[j]
        x1, h, ids, gates, counts = _post(mixed, w_o, xf, ffn_norm[i], moe_w_group[i], moe_w_expert[i])
        last = i == depth - 1
        xf = _moe(h, x1, ids, gates, counts, moe_w1[i], moe_w3[i], moe_w2[i], final_norm, last)
    return xf.reshape(bsz, seq, d)
```

```python
import functools

import jax
import jax.numpy as jnp
from jax import lax
from jax.experimental import pallas as pl
from jax.experimental.pallas import tpu as pltpu

F32 = jnp.float32
BF16 = jnp.bfloat16
I32 = jnp.int32
HIGHEST = lax.Precision.HIGHEST

EPS = 1e-6
GLA_HEADS = 4
GLA_DK = 128
GLA_DV = 256
GLA_GATE_RANK = 16
GLA_GATE_TAU = 16.0
GLA_CHUNK = 64
MLA_HEADS = 8
MLA_NOPE = 128
MLA_ROPE = 64
MLA_V = 128
MLA_Q_LORA = 256
MLA_KV_LORA = 128
MLA_SCALE = (MLA_NOPE + MLA_ROPE) ** -0.5
ROPE_THETA = 10000.0
MOE_GROUPS = 8
MOE_PER_GROUP = 8
MOE_EXPERTS = MOE_GROUPS * MOE_PER_GROUP
MOE_FF = 256

D_MODEL = 1024
LANES = 128
ROW_SUB = D_MODEL // LANES
VMEM_LIMIT = 48 * 1024 * 1024

TOK_TILE = 512
GLA_BLOCK = 256
ATT_BLOCK = 256
ROW_TILE = 256
EXP_BLOCK = 256


def _cparams(*sem):
    return pltpu.CompilerParams(dimension_semantics=sem, vmem_limit_bytes=VMEM_LIMIT)


def _rms(x, w):
    return x * lax.rsqrt(jnp.mean(x * x, axis=-1, keepdims=True) + EPS) * w


def _dot(a, b):
    return jnp.dot(a, b, preferred_element_type=F32)


def _dot_nt(a, b):
    return lax.dot_general(a, b, (((1,), (1,)), ((), ())), preferred_element_type=F32)


def _dot_tn(a, b):
    return lax.dot_general(a, b, (((0,), (0,)), ((), ())), preferred_element_type=F32)


def _gla_in_kernel(x_ref, nw_ref, wq_ref, wk_ref, wv_ref, wr_ref, wa_ref, wg_ref, bg_ref,
                   q_ref, k_ref, v_ref, r_ref, la_ref):
    hb = _rms(x_ref[...], nw_ref[...]).astype(BF16)
    q_ref[...] = _dot(hb, wq_ref[...]) * (GLA_DK ** -0.5)
    k_ref[...] = _dot(hb, wk_ref[...])
    v_ref[...] = _dot(hb, wv_ref[...]).astype(BF16)
    r_ref[...] = _dot(hb, wr_ref[...])
    a_lr = _dot(hb, wa_ref[...])
    z = jnp.dot(a_lr, wg_ref[...], precision=HIGHEST, preferred_element_type=F32) + bg_ref[...]
    log_sig = jnp.minimum(z, 0.0) - jnp.log1p(jnp.exp(-jnp.abs(z)))
    la_ref[...] = log_sig * (1.0 / GLA_GATE_TAU)


def _gla_in(x, nw, w_in, w_gate, b_gate):
    t, d = x.shape
    qk = GLA_HEADS * GLA_DK
    vd = GLA_HEADS * GLA_DV
    wq = w_in[:, :qk].astype(BF16)
    wk = w_in[:, qk:2 * qk].astype(BF16)
    wv = w_in[:, 2 * qk:2 * qk + vd].astype(BF16)
    wa = jnp.pad(w_in[:, 2 * qk + vd:2 * qk + vd + GLA_GATE_RANK], ((0, 0), (0, LANES - GLA_GATE_RANK))).astype(BF16)
    wr = w_in[:, 2 * qk + vd + GLA_GATE_RANK:].astype(BF16)
    wg = jnp.pad(w_gate, ((0, LANES - GLA_GATE_RANK), (0, 0)))
    tm = TOK_TILE
    row = lambda n: pl.BlockSpec((tm, n), lambda i: (i, 0))
    full = lambda a: pl.BlockSpec(a.shape, lambda i: (0, 0))
    nw2, bg2 = nw.reshape(1, d), b_gate.reshape(1, qk)
    return pl.pallas_call(
        _gla_in_kernel,
        grid=(t // tm,),
        in_specs=[row(d), full(nw2), full(wq), full(wk), full(wv), full(wr), full(wa), full(wg), full(bg2)],
        out_specs=[row(qk), row(qk), row(vd), row(vd), row(qk)],
        out_shape=[jax.ShapeDtypeStruct((t, qk), F32), jax.ShapeDtypeStruct((t, qk), F32),
                   jax.ShapeDtypeStruct((t, vd), BF16), jax.ShapeDtypeStruct((t, vd), F32),
                   jax.ShapeDtypeStruct((t, qk), F32)],
        compiler_params=_cparams("arbitrary"),
        name="gla_in",
    )(x, nw2, wq, wk, wv, wr, wa, wg, bg2)


def _gla_rec_kernel(q_ref, k_ref, v_ref, r_ref, la_ref, onw_ref, o_ref, st_ref):
    c = GLA_CHUNK

    @pl.when(pl.program_id(1) == 0)
    def _():
        st_ref[...] = jnp.zeros_like(st_ref)

    rows = lax.broadcasted_iota(I32, (c, c), 0)
    cols = lax.broadcasted_iota(I32, (c, c), 1)
    causal = cols <= rows
    tril = causal.astype(F32)
    onw = onw_ref[...]
    for ci in range(GLA_BLOCK // c):
        sl = pl.ds(ci * c, c)
        g_all = jnp.dot(tril, la_ref[sl, :], precision=HIGHEST, preferred_element_type=F32)
        for h in range(GLA_HEADS):
            ks = pl.ds(h * GLA_DK, GLA_DK)
            vs = pl.ds(h * GLA_DV, GLA_DV)
            g = g_all[:, h * GLA_DK:(h + 1) * GLA_DK]
            g_mid = g[c // 2:c // 2 + 1]
            g_last = g[c - 1:c]
            q = q_ref[sl, ks]
            k = k_ref[sl, ks]
            v = v_ref[sl, vs]
            a = _dot_nt((q * jnp.exp(g - g_mid)).astype(BF16), (k * jnp.exp(g_mid - g)).astype(BF16))
            a = jnp.where(causal, a, 0.0).astype(BF16)
            st = st_ref[h]
            o = _dot(a, v) + _dot_nt((q * jnp.exp(g)).astype(BF16), st.astype(BF16))
            k_state = (k * jnp.exp(g_last - g)).astype(BF16)
            st_ref[h] = st * jnp.exp(g_last) + _dot_tn(v, k_state)
            o = _rms(o, onw)
            r = r_ref[sl, vs]
            o_ref[sl, vs] = (o * (r * jax.nn.sigmoid(r))).astype(BF16)


def _gla_rec(q, k, v, r, la, onw, bsz, seq):
    t = bsz * seq
    qk = GLA_HEADS * GLA_DK
    vd = GLA_HEADS * GLA_DV
    nb = seq // GLA_BLOCK
    row = lambda n: pl.BlockSpec((GLA_BLOCK, n), lambda b, j: (b * nb + j, 0))
    onw2 = onw.reshape(1, GLA_DV)
    return pl.pallas_call(
        _gla_rec_kernel,
        grid=(bsz, nb),
        in_specs=[row(qk), row(qk), row(vd), row(vd), row(qk), pl.BlockSpec((1, GLA_DV), lambda b, j: (0, 0))],
        out_specs=row(vd),
        out_shape=jax.ShapeDtypeStruct((t, vd), BF16),
        scratch_shapes=[pltpu.VMEM((GLA_HEADS, GLA_DV, GLA_DK), F32)],
        compiler_params=_cparams("arbitrary", "arbitrary"),
        name="gla_rec",
    )(q, k, v, r, la, onw2)


def _store_rows(ref, val):
    n, d = val.shape
    nc = d // LANES
    for c in range(nc):
        ref[pl.ds(c, n, stride=nc), :] = val[:, c * LANES:(c + 1) * LANES]


def _load_rows(ref, start, n, nc, dtype):
    return jnp.concatenate([ref[pl.ds(start * nc + c, n, stride=nc), :].astype(dtype) for c in range(nc)], axis=1)


def _post_kernel(a_ref, wo_ref, x_ref, nw_ref, wr_ref,
                 x1_ref, h_ref, ids_ref, gates_ref, cnt_ref, carry_ref):
    tm = a_ref.shape[0]
    ne = MOE_EXPERTS

    @pl.when(pl.program_id(0) == 0)
    def _():
        carry_ref[...] = jnp.zeros_like(carry_ref)

    x1 = x_ref[...] + _dot(a_ref[...], wo_ref[...])
    x1_ref[...] = x1
    h = _rms(x1, nw_ref[...])
    _store_rows(h_ref, h)
    lt = lax.dot_general(wr_ref[...], h, (((1,), (1,)), ((), ())), precision=HIGHEST,
                         preferred_element_type=F32)
    gl = lt[0:MOE_GROUPS]
    gmax = jnp.max(gl, axis=0, keepdims=True)
    gi = lax.broadcasted_iota(I32, gl.shape, 0)
    g_sel = jnp.min(jnp.where(gl == gmax, gi, MOE_GROUPS), axis=0, keepdims=True)
    g_w = 1.0 / jnp.sum(jnp.exp(gl - gmax), axis=0, keepdims=True)
    el = lt[MOE_GROUPS:MOE_GROUPS + ne]
    ei = lax.broadcasted_iota(I32, el.shape, 0)
    in_group = (ei // MOE_PER_GROUP) == g_sel
    neg = jnp.float32(-jnp.inf)
    el1 = jnp.where(in_group, el, neg)
    l1 = jnp.max(el1, axis=0, keepdims=True)
    i1 = jnp.min(jnp.where(el1 == l1, ei, ne), axis=0, keepdims=True)
    el2 = jnp.where(ei == i1, neg, el1)
    l2 = jnp.max(el2, axis=0, keepdims=True)
    i2 = jnp.min(jnp.where(el2 == l2, ei, ne), axis=0, keepdims=True)
    e2 = jnp.exp(l2 - l1)
    gate1 = g_w / (1.0 + e2)
    gate2 = g_w * e2 / (1.0 + e2)
    oh1 = ei == i1
    oh2 = ei == i2
    both = jnp.where(oh1 | oh2, 1.0, 0.0)
    su = lax.broadcasted_iota(I32, (tm, tm), 0)
    tu = lax.broadcasted_iota(I32, (tm, tm), 1)
    upper = jnp.where(su < tu, 1.0, 0.0).astype(BF16)
    prefix = _dot(both.astype(BF16), upper) + carry_ref[:, 0:1]
    rank1 = jnp.sum(jnp.where(oh1, prefix, 0.0), axis=0, keepdims=True)
    rank2 = jnp.sum(jnp.where(oh2, prefix, 0.0), axis=0, keepdims=True)
    carry = carry_ref[...] + jnp.sum(both, axis=1, keepdims=True)
    carry_ref[...] = carry
    cnt_ref[...] = carry.astype(I32)
    ids_ref[...] = jnp.concatenate([i1, i2, rank1.astype(I32), rank2.astype(I32)], axis=0)
    gates_ref[...] = jnp.concatenate([gate1, gate2], axis=0)


def _post(a, wo, x, nw, w_group, w_expert):
    t, d = x.shape
    tm = TOK_TILE
    wr = jnp.concatenate([w_group.T, w_expert.T], axis=0)
    wr = jnp.pad(wr, ((0, LANES - wr.shape[0]), (0, 0)))
    wo = wo.astype(BF16)
    nw2 = nw.reshape(1, d)
    row = lambda n: pl.BlockSpec((tm, n), lambda i: (i, 0))
    full = lambda arr: pl.BlockSpec(arr.shape, lambda i: (0, 0))
    return pl.pallas_call(
        _post_kernel,
        grid=(t // tm,),
        in_specs=[row(a.shape[1]), full(wo), row(d), full(nw2), full(wr)],
        out_specs=[row(d), pl.BlockSpec((tm * d // LANES, LANES), lambda i: (i, 0)),
                   pl.BlockSpec((4, tm), lambda i: (0, i)), pl.BlockSpec((2, tm), lambda i: (0, i)),
                   pl.BlockSpec((MOE_EXPERTS, LANES), lambda i: (0, 0))],
        out_shape=[jax.ShapeDtypeStruct((t, d), F32), jax.ShapeDtypeStruct((t * d // LANES, LANES), F32),
                   jax.ShapeDtypeStruct((4, t), I32), jax.ShapeDtypeStruct((2, t), F32),
                   jax.ShapeDtypeStruct((MOE_EXPERTS, LANES), I32)],
        scratch_shapes=[pltpu.VMEM((MOE_EXPERTS, LANES), F32)],
        compiler_params=_cparams("arbitrary"),
        name="post_router",
    )(a, wo, x, nw2, wr)


def _row(ref, i):
    return ref.at[pl.ds(pl.multiple_of(i * ROW_SUB, ROW_SUB), ROW_SUB)]


def _fetch_slots(dest_hbm, i, idx_ref, sem):
    copies = [pltpu.make_async_copy(dest_hbm.at[a, i], idx_ref.at[a], sem) for a in range(2)]
    for cp in copies:
        cp.start()
    for cp in copies:
        cp.wait()


def _zero_fill(zinfo_ref, zero_ref, xs_hbm, sem, wait):
    ne = MOE_EXPERTS
    nblk = xs_hbm.shape[0] // (EXP_BLOCK * ROW_SUB)

    def copy(off_rows, n_rows):
        off = pl.multiple_of(off_rows * ROW_SUB, ROW_SUB)
        cp = pltpu.make_async_copy(zero_ref.at[pl.ds(0, n_rows * ROW_SUB)],
                                   xs_hbm.at[pl.ds(off, n_rows * ROW_SUB)], sem)
        cp.wait() if wait else cp.start()

    def expert(e, carry):
        pad = zinfo_ref[ne + e]
        for b in range(EXP_BLOCK.bit_length() - 1):
            @pl.when(((pad >> b) & 1) == 1)
            def _():
                copy(zinfo_ref[e] + (pad & ((1 << b) - 1)), 1 << b)
        return carry

    def tail(kk, carry):
        blk = zinfo_ref[2 * ne] + kk

        @pl.when(blk < nblk)
        def _():
            copy(blk * EXP_BLOCK, EXP_BLOCK)
        return carry

    lax.fori_loop(0, ne, expert, 0)
    lax.fori_loop(0, ne + 1, tail, 0)


def _dispatch_kernel(zinfo_ref, dest_hbm, h_ref, xs_hbm, idx_ref, zero_ref, sem_idx, sem_rows):
    i = pl.program_id(0)
    tr = h_ref.shape[0] // ROW_SUB

    @pl.when(i == 0)
    def _():
        zero_ref[...] = jnp.zeros_like(zero_ref)
        _zero_fill(zinfo_ref, zero_ref, xs_hbm, sem_rows, wait=False)
        _zero_fill(zinfo_ref, zero_ref, xs_hbm, sem_rows, wait=True)

    _fetch_slots(dest_hbm, i, idx_ref, sem_idx)

    def start(t, carry):
        pltpu.make_async_copy(_row(h_ref, t), _row(xs_hbm, idx_ref[0, t]), sem_rows).start()
        pltpu.make_async_copy(_row(h_ref, t), _row(xs_hbm, idx_ref[1, t]), sem_rows).start()
        return carry

    def wait(t, carry):
        pltpu.make_async_copy(_row(h_ref, 0), _row(xs_hbm, 0), sem_rows).wait()
        return carry

    lax.fori_loop(0, tr, start, 0)
    lax.fori_loop(0, 2 * tr, wait, 0)


def _dispatch(h, dest, zoff, n_slots):
    t = h.shape[0] // ROW_SUB
    tr = ROW_TILE
    nt = t // tr
    grid_spec = pltpu.PrefetchScalarGridSpec(
        num_scalar_prefetch=1,
        grid=(nt,),
        in_specs=[pl.BlockSpec(memory_space=pl.ANY), pl.BlockSpec((tr * ROW_SUB, LANES), lambda i, z: (i, 0))],
        out_specs=pl.BlockSpec(memory_space=pl.ANY),
        scratch_shapes=[pltpu.SMEM((2, tr), I32), pltpu.VMEM((EXP_BLOCK * ROW_SUB, LANES), F32),
                        pltpu.SemaphoreType.DMA, pltpu.SemaphoreType.DMA],
    )
    return pl.pallas_call(
        _dispatch_kernel,
        grid_spec=grid_spec,
        out_shape=jax.ShapeDtypeStruct((n_slots * ROW_SUB, LANES), F32),
        compiler_params=_cparams("arbitrary"),
        name="moe_dispatch",
    )(zoff, dest.reshape(2, nt, tr), h)


def _expert_kernel(be_ref, nused_ref, xs_ref, w1_ref, w3_ref, w2_ref, ys_ref, w1b_ref, w3b_ref, w2b_ref):
    i = pl.program_id(0)
    used = i < nused_ref[0]

    @pl.when(used & ((i == 0) | (be_ref[i] != be_ref[jnp.maximum(i - 1, 0)])))
    def _():
        w1b_ref[...] = w1_ref[0].astype(BF16)
        w3b_ref[...] = w3_ref[0].astype(BF16)
        w2b_ref[...] = w2_ref[0].astype(BF16)

    @pl.when(used)
    def _():
        xb = _load_rows(xs_ref, 0, EXP_BLOCK, ROW_SUB, BF16)
        a = _dot(xb, w1b_ref[...])
        b = _dot(xb, w3b_ref[...])
        hid = (a * jax.nn.sigmoid(a) * b).astype(BF16)
        _store_rows(ys_ref, _dot(hid, w2b_ref[...]))

    @pl.when(i >= nused_ref[0])
    def _():
        ys_ref[...] = jnp.zeros_like(ys_ref)


def _experts(xs, block_e, n_used, w1, w3, w2):
    bm = EXP_BLOCK
    n_slots = xs.shape[0] // ROW_SUB
    nblk = n_slots // bm
    d, ff = w1.shape[-2:]

    def xmap(i, be, nu):
        return (jnp.minimum(i, nu[0] - 1), 0)

    def wmap(i, be, nu):
        return (be[jnp.minimum(i, nu[0] - 1)], 0, 0)

    grid_spec = pltpu.PrefetchScalarGridSpec(
        num_scalar_prefetch=2,
        grid=(nblk,),
        in_specs=[pl.BlockSpec((bm * ROW_SUB, LANES), xmap), pl.BlockSpec((1, d, ff), wmap),
                  pl.BlockSpec((1, d, ff), wmap), pl.BlockSpec((1, ff, d), wmap)],
        out_specs=pl.BlockSpec((bm * ROW_SUB, LANES), lambda i, be, nu: (i, 0)),
        scratch_shapes=[pltpu.VMEM((d, ff), BF16), pltpu.VMEM((d, ff), BF16), pltpu.VMEM((ff, d), BF16)],
    )
    return pl.pallas_call(
        _expert_kernel,
        grid_spec=grid_spec,
        out_shape=jax.ShapeDtypeStruct((n_slots * ROW_SUB, LANES), F32),
        compiler_params=_cparams("arbitrary"),
        name="moe_experts",
    )(block_e, n_used, xs, w1, w3, w2)


def _combine_kernel(dest_hbm, ys_hbm, x_ref, g_ref, nw_ref, o_ref, idx_ref, buf_ref, sem_idx, sem_rows,
                    *, final_norm):
    i = pl.program_id(0)
    tr = x_ref.shape[0]
    _fetch_slots(dest_hbm, i, idx_ref, sem_idx)

    def start(t, carry):
        pltpu.make_async_copy(_row(ys_hbm, idx_ref[0, t]), _row(buf_ref, t), sem_rows).start()
        pltpu.make_async_copy(_row(ys_hbm, idx_ref[1, t]), _row(buf_ref, tr + t), sem_rows).start()
        return carry

    def wait(t, carry):
        pltpu.make_async_copy(_row(ys_hbm, 0), _row(buf_ref, 0), sem_rows).wait()
        return carry

    lax.fori_loop(0, tr, start, 0)
    lax.fori_loop(0, 2 * tr, wait, 0)
    g = g_ref[...]
    y1 = _load_rows(buf_ref, 0, tr, ROW_SUB, F32)
    y2 = _load_rows(buf_ref, tr, tr, ROW_SUB, F32)
    out = x_ref[...] + g[:, 0:1] * y1 + g[:, 1:2] * y2
    if final_norm:
        out = _rms(out, nw_ref[...])
    o_ref[...] = out


def _combine(ys, dest, x, gates, nw, final_norm):
    t, d = x.shape
    tr = ROW_TILE
    nt = t // tr
    g_col = gates.T
    nw2 = nw.reshape(1, d)
    return pl.pallas_call(
        functools.partial(_combine_kernel, final_norm=final_norm),
        grid=(nt,),
        in_specs=[pl.BlockSpec(memory_space=pl.ANY), pl.BlockSpec(memory_space=pl.ANY),
                  pl.BlockSpec((tr, d), lambda i: (i, 0)), pl.BlockSpec((tr, 2), lambda i: (i, 0)),
                  pl.BlockSpec((1, d), lambda i: (0, 0))],
        out_specs=pl.BlockSpec((tr, d), lambda i: (i, 0)),
        out_shape=jax.ShapeDtypeStruct((t, d), F32),
        scratch_shapes=[pltpu.SMEM((2, tr), I32), pltpu.VMEM((2 * tr * ROW_SUB, LANES), F32),
                        pltpu.SemaphoreType.DMA, pltpu.SemaphoreType.DMA],
        compiler_params=_cparams("arbitrary"),
        name="moe_combine",
    )(dest.reshape(2, nt, tr), ys, x, g_col, nw2)


def _slots_kernel(ids_ref, pst_ref, dest_ref):
    ids = ids_ref[...]
    ei = lax.broadcasted_iota(I32, (MOE_EXPERTS, ids.shape[1]), 0)
    pst = pst_ref[...]
    d1 = jnp.sum(jnp.where(ei == ids[0:1], pst, 0), axis=0, keepdims=True) + ids[2:3]
    d2 = jnp.sum(jnp.where(ei == ids[1:2], pst, 0), axis=0, keepdims=True) + ids[3:4]
    dest_ref[...] = jnp.concatenate([d1, d2], axis=0)


def _slots(ids, pstarts):
    t = ids.shape[1]
    tm = TOK_TILE
    return pl.pallas_call(
        _slots_kernel,
        grid=(t // tm,),
        in_specs=[pl.BlockSpec((4, tm), lambda i: (0, i)), pl.BlockSpec((MOE_EXPERTS, 1), lambda i: (0, 0))],
        out_specs=pl.BlockSpec((2, tm), lambda i: (0, i)),
        out_shape=jax.ShapeDtypeStruct((2, t), I32),
        compiler_params=_cparams("arbitrary"),
        name="moe_slots",
    )(ids, pstarts.reshape(MOE_EXPERTS, 1))


def _moe(h, x1, ids, gates, counts, w1, w3, w2, nw_final, final_norm):
    t = x1.shape[0]
    bm = EXP_BLOCK
    n_asg = 2 * t
    n_slots = n_asg + MOE_EXPERTS * bm
    cnt = counts[:, 0]
    padded = (cnt + bm - 1) // bm * bm
    pend = jnp.cumsum(padded)
    pstarts = (pend - padded).astype(I32)
    dest = _slots(ids, pstarts)
    nblk = n_slots // bm
    block_start = jnp.arange(nblk, dtype=I32) * bm
    block_e = jnp.minimum(jnp.sum(pend[None, :] <= block_start[:, None], axis=1), MOE_EXPERTS - 1).astype(I32)
    n_used = (pend[-1:] // bm).astype(I32)
    zinfo = jnp.concatenate([pstarts + cnt, padded - cnt, n_used]).astype(I32)
    xs = _dispatch(h, dest, zinfo, n_slots)
    ys = _experts(xs, block_e, n_used, w1, w3, w2)
    return _combine(ys, dest, x1, gates, nw_final, final_norm)


def _mla_in_kernel(x_ref, pos_ref, nw_ref, wq_ref, wkv_ref, wks_ref, qn_ref, kvn_ref, wqn_ref, wqr_ref, wqs_ref,
                   wuk_ref, freq_ref, qt_ref, k_ref, vt_ref):
    tm = x_ref.shape[0]
    tb = ATT_BLOCK
    hb = _rms(x_ref[...], nw_ref[...]).astype(BF16)
    c_q = _rms(_dot(hb, wq_ref[...]), qn_ref[...]).astype(BF16)
    kv_t = _dot_nt(wkv_ref[...], hb)
    ks_t = _dot_nt(wks_ref[...], hb)
    c_kv = kv_t[:MLA_KV_LORA]
    c_kv = c_kv * lax.rsqrt(jnp.mean(c_kv * c_kv, axis=0, keepdims=True) + EPS) * kvn_ref[...]
    ang = freq_ref[...] * pos_ref[...].astype(F32)
    cos32, sin32 = jnp.cos(ang), jnp.sin(ang)
    cos = jnp.concatenate([cos32] * 4, axis=0)
    sin = jnp.concatenate([-sin32, sin32] * 2, axis=0)
    k_rope = kv_t[MLA_KV_LORA:] * cos + ks_t * sin
    k_ref[...] = jnp.concatenate([c_kv, k_rope], axis=0).T.astype(BF16)
    c_kv16 = c_kv.astype(BF16)
    for c in range(tm // tb):
        vt_ref[c] = c_kv16[:, c * tb:(c + 1) * tb]
    q_nope = _dot_nt(wqn_ref[...], c_q).astype(BF16)
    q_r = _dot_nt(wqr_ref[...], c_q)
    q_s = _dot_nt(wqs_ref[...], c_q)
    for h in range(MLA_HEADS):
        sl = slice(h * LANES, (h + 1) * LANES)
        q_lat = (_dot(wuk_ref[h], q_nope[sl]) * MLA_SCALE).astype(BF16)
        q_rope = ((q_r[sl] * cos + q_s[sl] * sin) * MLA_SCALE).astype(BF16)
        for c in range(tm // tb):
            cs = slice(c * tb, (c + 1) * tb)
            qt_ref[c, 2 * h * LANES:(2 * h + 1) * LANES, :] = q_lat[:, cs]
            qt_ref[c, (2 * h + 1) * LANES:(2 * h + 2) * LANES, :] = q_rope[:, cs]


def _swap_halves(w):
    half = w.shape[-1] // 2
    return jnp.concatenate([w[..., half:], w[..., :half]], axis=-1)


def _mla_in(x, pos, nw, w_in, q_norm, w_uq, kv_norm, w_uk):
    t, d = x.shape
    tm = TOK_TILE
    tb = ATT_BLOCK
    nh = MLA_HEADS
    pad_r = LANES - MLA_ROPE
    w_kr = w_in[:, MLA_Q_LORA + MLA_KV_LORA:]
    wq = w_in[:, :MLA_Q_LORA].astype(BF16)
    wkv = jnp.pad(w_in[:, MLA_Q_LORA:], ((0, 0), (0, pad_r))).T.astype(BF16)
    wks = jnp.pad(_swap_halves(w_kr), ((0, 0), (0, pad_r))).T.astype(BF16)
    wqn = w_uq[:, :, :MLA_NOPE].reshape(MLA_Q_LORA, nh * MLA_NOPE).T.astype(BF16)
    w_r = w_uq[:, :, MLA_NOPE:]
    wqr = jnp.pad(w_r, ((0, 0), (0, 0), (0, pad_r))).reshape(MLA_Q_LORA, nh * LANES).T.astype(BF16)
    wqs = jnp.pad(_swap_halves(w_r), ((0, 0), (0, 0), (0, pad_r))).reshape(MLA_Q_LORA, nh * LANES).T.astype(BF16)
    wuk = w_uk.transpose(1, 0, 2).astype(BF16)
    inv_freq = 1.0 / (ROPE_THETA ** (jnp.arange(0, MLA_ROPE, 2, dtype=F32) / MLA_ROPE))
    full = lambda a: pl.BlockSpec(a.shape, lambda i: (0,) * a.ndim)
    args = (x, pos.reshape(1, t), nw.reshape(1, d), wq, wkv, wks, q_norm.reshape(1, -1), kv_norm.reshape(-1, 1),
            wqn, wqr, wqs, wuk, inv_freq.reshape(-1, 1))
    nsub = tm // tb
    return pl.pallas_call(
        _mla_in_kernel,
        grid=(t // tm,),
        in_specs=[pl.BlockSpec((tm, d), lambda i: (i, 0)), pl.BlockSpec((1, tm), lambda i: (0, i))]
        + [full(a) for a in args[2:]],
        out_specs=[pl.BlockSpec((nsub, 2 * nh * LANES, tb), lambda i: (i, 0, 0)),
                   pl.BlockSpec((tm, 2 * LANES), lambda i: (i, 0)),
                   pl.BlockSpec((nsub, MLA_KV_LORA, tb), lambda i: (i, 0, 0))],
        out_shape=[jax.ShapeDtypeStruct((t // tb, 2 * nh * LANES, tb), BF16),
                   jax.ShapeDtypeStruct((t, 2 * LANES), BF16),
                   jax.ShapeDtypeStruct((t // tb, MLA_KV_LORA, tb), BF16)],
        compiler_params=_cparams("arbitrary"),
        name="mla_in",
    )(*args)


def _mla_attn_kernel(qt_ref, k_ref, vt_ref, wuv_ref, o_ref, *state):
    i = pl.program_id(1)
    tb = ATT_BLOCK
    nh = MLA_HEADS
    dq = 2 * LANES
    s_refs, m_refs, l_refs, acc_refs = state[0::4], state[1::4], state[2::4], state[3::4]

    def scores(j, h):
        kb = k_ref[pl.ds(pl.multiple_of(j * tb, tb), tb), :]
        return _dot(kb, qt_ref[0, h * dq:(h + 1) * dq, :])

    def consume(j, h, s, masked):
        if masked:
            kpos = lax.broadcasted_iota(I32, (tb, tb), 0)
            qpos = lax.broadcasted_iota(I32, (tb, tb), 1)
            s = jnp.where(kpos <= qpos, s, -jnp.inf)
        m_old = m_refs[h][...]
        m_new = jnp.maximum(m_old, jnp.max(s, axis=0, keepdims=True))
        alpha = jnp.exp(m_old - m_new)
        p = jnp.exp(s - m_new)
        l_refs[h][...] = alpha * l_refs[h][...] + jnp.sum(p, axis=0, keepdims=True)
        acc_refs[h][...] = alpha * acc_refs[h][...] + _dot(vt_ref[j], p.astype(BF16))
        m_refs[h][...] = m_new

    for h in range(nh):
        m_refs[h][...] = jnp.full_like(m_refs[h], -jnp.inf)
        l_refs[h][...] = jnp.zeros_like(l_refs[h])
        acc_refs[h][...] = jnp.zeros_like(acc_refs[h])
        s_refs[h][...] = scores(0, h)

    def full_block(j, carry):
        for h in range(nh):
            s = s_refs[h][...]
            s_new = scores(j + 1, h)
            consume(j, h, s, False)
            s_refs[h][...] = s_new
        return carry

    lax.fori_loop(0, i, full_block, 0)
    for h in range(nh):
        consume(i, h, s_refs[h][...], True)
    for h in range(nh):
        o_lat = (acc_refs[h][...] / l_refs[h][...]).astype(BF16)
        o_ref[:, h * MLA_V:(h + 1) * MLA_V] = _dot_tn(o_lat, wuv_ref[h]).astype(BF16)


def _mla_attn(qt, k, vt, w_uv, bsz, seq):
    t = bsz * seq
    tb = ATT_BLOCK
    nq = seq // tb
    nh = MLA_HEADS
    wuv = w_uv.transpose(1, 0, 2).astype(BF16)
    return pl.pallas_call(
        _mla_attn_kernel,
        grid=(bsz, nq),
        in_specs=[pl.BlockSpec((1, 2 * nh * LANES, tb), lambda b, i: (b * nq + i, 0, 0)),
                  pl.BlockSpec((seq, 2 * LANES), lambda b, i: (b, 0)),
                  pl.BlockSpec((nq, MLA_KV_LORA, tb), lambda b, i: (b, 0, 0)),
                  pl.BlockSpec(wuv.shape, lambda b, i: (0, 0, 0))],
        out_specs=pl.BlockSpec((tb, nh * MLA_V), lambda b, i: (b * nq + i, 0)),
        out_shape=jax.ShapeDtypeStruct((t, nh * MLA_V), BF16),
        scratch_shapes=[pltpu.VMEM((tb, tb), F32), pltpu.VMEM((1, tb), F32), pltpu.VMEM((1, tb), F32),
                        pltpu.VMEM((MLA_KV_LORA, tb), F32)] * nh,
        compiler_params=_cparams("arbitrary", "arbitrary"),
        name="mla_attn",
    )(qt, k, vt, wuv)


def kernel(x, positions, attn_norm, ffn_norm, final_norm, gla_w_in, gla_w_gate, gla_b_gate, gla_out_norm, gla_w_o,
           mla_w_in, mla_q_norm, mla_w_uq, mla_kv_norm, mla_w_uk, mla_w_uv, mla_w_o,
           moe_w_group, moe_w_expert, moe_w1, moe_w3, moe_w2):
    bsz, seq, d = x.shape
    t = bsz * seq
    depth = attn_norm.shape[0]
    xf = x.reshape(t, d)
    pos = positions.reshape(t)
    for i in range(depth):
        j = i // 2
        if i % 2 == 0:
            q, k, v, r, la = _gla_in(xf, attn_norm[i], gla_w_in[j], gla_w_gate[j], gla_b_gate[j])
            mixed = _gla_rec(q, k, v, r, la, gla_out_norm[j], bsz, seq)
            w_o = gla_w_o[j]
        else:
            qt, kc, vt = _mla_in(xf, pos, attn_norm[i], mla_w_in[j], mla_q_norm[j], mla_w_uq[j], mla_kv_norm[j],
                                 mla_w_uk[j])
            mixed = _mla_attn(qt, kc, vt, mla_w_uv[j], bsz, seq)
            w_o = mla_w_o[j]
        x1, h, ids, gates, counts = _post(mixed, w_o, xf, ffn_norm[i], moe_w_group[i], moe_w_expert[i])
        last = i == depth - 1
        xf = _moe(h, x1, ids, gates, counts, moe_w1[i], moe_w3[i], moe_w2[i], final_norm, last)
    return xf.reshape(bsz, seq, d)
```

```python
import functools

import jax
import jax.numpy as jnp
from jax import lax
from jax.experimental import pallas as pl
from jax.experimental.pallas import tpu as pltpu

F32 = jnp.float32
BF16 = jnp.bfloat16
I32 = jnp.int32
HIGHEST = lax.Precision.HIGHEST

EPS = 1e-6
GLA_HEADS = 4
GLA_DK = 128
GLA_DV = 256
GLA_GATE_RANK = 16
GLA_GATE_TAU = 16.0
GLA_CHUNK = 64
MLA_HEADS = 8
MLA_NOPE = 128
MLA_ROPE = 64
MLA_V = 128
MLA_Q_LORA = 256
MLA_KV_LORA = 128
MLA_SCALE = (MLA_NOPE + MLA_ROPE) ** -0.5
Q_SCALE = MLA_SCALE * 1.4426950408889634
VT_ROWS = MLA_KV_LORA + 16
ROPE_THETA = 10000.0
MOE_GROUPS = 8
MOE_PER_GROUP = 8
MOE_EXPERTS = MOE_GROUPS * MOE_PER_GROUP
MOE_FF = 256

D_MODEL = 1024
LANES = 128
ROW_SUB = D_MODEL // LANES
VMEM_LIMIT = 48 * 1024 * 1024

TOK_TILE = 512
GLA_BLOCK = 256
ATT_BLOCK = 256
ROW_TILE = 256
EXP_BLOCK = 256


def _cparams(*sem):
    return pltpu.CompilerParams(dimension_semantics=sem, vmem_limit_bytes=VMEM_LIMIT)


def _rms(x, w):
    return x * lax.rsqrt(jnp.mean(x * x, axis=-1, keepdims=True) + EPS) * w


def _dot(a, b):
    return jnp.dot(a, b, preferred_element_type=F32)


def _dot_nt(a, b):
    return lax.dot_general(a, b, (((1,), (1,)), ((), ())), preferred_element_type=F32)


def _dot_tn(a, b):
    return lax.dot_general(a, b, (((0,), (0,)), ((), ())), preferred_element_type=F32)


def _gla_in_kernel(x_ref, nw_ref, wq_ref, wk_ref, wv_ref, wr_ref, wa_ref, wg_ref, bg_ref,
                   q_ref, k_ref, v_ref, r_ref, la_ref):
    hb = _rms(x_ref[...], nw_ref[...]).astype(BF16)
    q_ref[...] = _dot(hb, wq_ref[...]) * (GLA_DK ** -0.5)
    k_ref[...] = _dot(hb, wk_ref[...])
    v_ref[...] = _dot(hb, wv_ref[...]).astype(BF16)
    r_ref[...] = _dot(hb, wr_ref[...])
    a_lr = _dot(hb, wa_ref[...])
    z = jnp.dot(a_lr, wg_ref[...], precision=HIGHEST, preferred_element_type=F32) + bg_ref[...]
    log_sig = jnp.minimum(z, 0.0) - jnp.log1p(jnp.exp(-jnp.abs(z)))
    la_ref[...] = log_sig * (1.0 / GLA_GATE_TAU)


def _gla_in(x, nw, w_in, w_gate, b_gate):
    t, d = x.shape
    qk = GLA_HEADS * GLA_DK
    vd = GLA_HEADS * GLA_DV
    wq = w_in[:, :qk].astype(BF16)
    wk = w_in[:, qk:2 * qk].astype(BF16)
    wv = w_in[:, 2 * qk:2 * qk + vd].astype(BF16)
    wa = jnp.pad(w_in[:, 2 * qk + vd:2 * qk + vd + GLA_GATE_RANK], ((0, 0), (0, LANES - GLA_GATE_RANK))).astype(BF16)
    wr = w_in[:, 2 * qk + vd + GLA_GATE_RANK:].astype(BF16)
    wg = jnp.pad(w_gate, ((0, LANES - GLA_GATE_RANK), (0, 0)))
    tm = TOK_TILE
    row = lambda n: pl.BlockSpec((tm, n), lambda i: (i, 0))
    full = lambda a: pl.BlockSpec(a.shape, lambda i: (0, 0))
    nw2, bg2 = nw.reshape(1, d), b_gate.reshape(1, qk)
    return pl.pallas_call(
        _gla_in_kernel,
        grid=(t // tm,),
        in_specs=[row(d), full(nw2), full(wq), full(wk), full(wv), full(wr), full(wa), full(wg), full(bg2)],
        out_specs=[row(qk), row(qk), row(vd), row(vd), row(qk)],
        out_shape=[jax.ShapeDtypeStruct((t, qk), F32), jax.ShapeDtypeStruct((t, qk), F32),
                   jax.ShapeDtypeStruct((t, vd), BF16), jax.ShapeDtypeStruct((t, vd), F32),
                   jax.ShapeDtypeStruct((t, qk), F32)],
        compiler_params=_cparams("arbitrary"),
        name="gla_in",
    )(x, nw2, wq, wk, wv, wr, wa, wg, bg2)


def _gla_rec_kernel(q_ref, k_ref, v_ref, r_ref, la_ref, onw_ref, o_ref, st_ref):
    c = GLA_CHUNK

    @pl.when(pl.program_id(1) == 0)
    def _():
        st_ref[...] = jnp.zeros_like(st_ref)

    rows = lax.broadcasted_iota(I32, (c, c), 0)
    cols = lax.broadcasted_iota(I32, (c, c), 1)
    causal = cols <= rows
    tril = causal.astype(F32)
    onw = onw_ref[...]
    for ci in range(GLA_BLOCK // c):
        sl = pl.ds(ci * c, c)
        g_all = jnp.dot(tril, la_ref[sl, :], precision=HIGHEST, preferred_element_type=F32)
        for h in range(GLA_HEADS):
            ks = pl.ds(h * GLA_DK, GLA_DK)
            vs = pl.ds(h * GLA_DV, GLA_DV)
            g = g_all[:, h * GLA_DK:(h + 1) * GLA_DK]
            g_mid = g[c // 2:c // 2 + 1]
            g_last = g[c - 1:c]
            q = q_ref[sl, ks]
            k = k_ref[sl, ks]
            v = v_ref[sl, vs]
            a = _dot_nt((q * jnp.exp(g - g_mid)).astype(BF16), (k * jnp.exp(g_mid - g)).astype(BF16))
            a = jnp.where(causal, a, 0.0).astype(BF16)
            st = st_ref[h]
            o = _dot(a, v) + _dot_nt((q * jnp.exp(g)).astype(BF16), st.astype(BF16))
            k_state = (k * jnp.exp(g_last - g)).astype(BF16)
            st_ref[h] = st * jnp.exp(g_last) + _dot_tn(v, k_state)
            o = _rms(o, onw)
            r = r_ref[sl, vs]
            o_ref[sl, vs] = (o * (r * jax.nn.sigmoid(r))).astype(BF16)


def _gla_rec(q, k, v, r, la, onw, bsz, seq):
    t = bsz * seq
    qk = GLA_HEADS * GLA_DK
    vd = GLA_HEADS * GLA_DV
    nb = seq // GLA_BLOCK
    row = lambda n: pl.BlockSpec((GLA_BLOCK, n), lambda b, j: (b * nb + j, 0))
    onw2 = onw.reshape(1, GLA_DV)
    return pl.pallas_call(
        _gla_rec_kernel,
        grid=(bsz, nb),
        in_specs=[row(qk), row(qk), row(vd), row(vd), row(qk), pl.BlockSpec((1, GLA_DV), lambda b, j: (0, 0))],
        out_specs=row(vd),
        out_shape=jax.ShapeDtypeStruct((t, vd), BF16),
        scratch_shapes=[pltpu.VMEM((GLA_HEADS, GLA_DV, GLA_DK), F32)],
        compiler_params=_cparams("arbitrary", "arbitrary"),
        name="gla_rec",
    )(q, k, v, r, la, onw2)


def _store_rows(ref, val):
    n, d = val.shape
    nc = d // LANES
    for c in range(nc):
        ref[pl.ds(c, n, stride=nc), :] = val[:, c * LANES:(c + 1) * LANES]


def _load_rows(ref, start, n, nc, dtype):
    return jnp.concatenate([ref[pl.ds(start * nc + c, n, stride=nc), :].astype(dtype) for c in range(nc)], axis=1)


def _post_kernel(a_ref, wo_ref, x_ref, nw_ref, wr_ref,
                 x1_ref, h_ref, ids_ref, gates_ref, cnt_ref, carry_ref):
    tm = a_ref.shape[0]
    ne = MOE_EXPERTS

    @pl.when(pl.program_id(0) == 0)
    def _():
        carry_ref[...] = jnp.zeros_like(carry_ref)

    x1 = x_ref[...] + _dot(a_ref[...], wo_ref[...])
    x1_ref[...] = x1
    h = _rms(x1, nw_ref[...])
    _store_rows(h_ref, h)
    lt = lax.dot_general(wr_ref[...], h, (((1,), (1,)), ((), ())), precision=HIGHEST,
                         preferred_element_type=F32)
    gl = lt[0:MOE_GROUPS]
    gmax = jnp.max(gl, axis=0, keepdims=True)
    gi = lax.broadcasted_iota(I32, gl.shape, 0)
    g_sel = jnp.min(jnp.where(gl == gmax, gi, MOE_GROUPS), axis=0, keepdims=True)
    g_w = 1.0 / jnp.sum(jnp.exp(gl - gmax), axis=0, keepdims=True)
    el = lt[MOE_GROUPS:MOE_GROUPS + ne]
    ei = lax.broadcasted_iota(I32, el.shape, 0)
    in_group = (ei // MOE_PER_GROUP) == g_sel
    neg = jnp.float32(-jnp.inf)
    el1 = jnp.where(in_group, el, neg)
    l1 = jnp.max(el1, axis=0, keepdims=True)
    i1 = jnp.min(jnp.where(el1 == l1, ei, ne), axis=0, keepdims=True)
    el2 = jnp.where(ei == i1, neg, el1)
    l2 = jnp.max(el2, axis=0, keepdims=True)
    i2 = jnp.min(jnp.where(el2 == l2, ei, ne), axis=0, keepdims=True)
    e2 = jnp.exp(l2 - l1)
    gate1 = g_w / (1.0 + e2)
    gate2 = g_w * e2 / (1.0 + e2)
    oh1 = ei == i1
    oh2 = ei == i2
    both = jnp.where(oh1 | oh2, 1.0, 0.0)
    su = lax.broadcasted_iota(I32, (tm, tm), 0)
    tu = lax.broadcasted_iota(I32, (tm, tm), 1)
    upper = jnp.where(su < tu, 1.0, 0.0).astype(BF16)
    prefix = _dot(both.astype(BF16), upper) + carry_ref[:, 0:1]
    rank1 = jnp.sum(jnp.where(oh1, prefix, 0.0), axis=0, keepdims=True)
    rank2 = jnp.sum(jnp.where(oh2, prefix, 0.0), axis=0, keepdims=True)
    carry = carry_ref[...] + jnp.sum(both, axis=1, keepdims=True)
    carry_ref[...] = carry
    cnt_ref[...] = carry.astype(I32)
    ids_ref[...] = jnp.concatenate([i1, i2, rank1.astype(I32), rank2.astype(I32)], axis=0)
    gates_ref[...] = jnp.concatenate([gate1, gate2], axis=0)


def _post(a, wo, x, nw, w_group, w_expert):
    t, d = x.shape
    tm = TOK_TILE
    wr = jnp.concatenate([w_group.T, w_expert.T], axis=0)
    wr = jnp.pad(wr, ((0, LANES - wr.shape[0]), (0, 0)))
    wo = wo.astype(BF16)
    nw2 = nw.reshape(1, d)
    row = lambda n: pl.BlockSpec((tm, n), lambda i: (i, 0))
    full = lambda arr: pl.BlockSpec(arr.shape, lambda i: (0, 0))
    return pl.pallas_call(
        _post_kernel,
        grid=(t // tm,),
        in_specs=[row(a.shape[1]), full(wo), row(d), full(nw2), full(wr)],
        out_specs=[row(d), pl.BlockSpec((tm * d // LANES, LANES), lambda i: (i, 0)),
                   pl.BlockSpec((4, tm), lambda i: (0, i)), pl.BlockSpec((2, tm), lambda i: (0, i)),
                   pl.BlockSpec((MOE_EXPERTS, LANES), lambda i: (0, 0))],
        out_shape=[jax.ShapeDtypeStruct((t, d), F32), jax.ShapeDtypeStruct((t * d // LANES, LANES), F32),
                   jax.ShapeDtypeStruct((4, t), I32), jax.ShapeDtypeStruct((2, t), F32),
                   jax.ShapeDtypeStruct((MOE_EXPERTS, LANES), I32)],
        scratch_shapes=[pltpu.VMEM((MOE_EXPERTS, LANES), F32)],
        compiler_params=_cparams("arbitrary"),
        name="post_router",
    )(a, wo, x, nw2, wr)


def _row(ref, i):
    return ref.at[pl.ds(pl.multiple_of(i * ROW_SUB, ROW_SUB), ROW_SUB)]


def _zero_fill(zinfo_ref, zero_ref, xs_hbm, sem, wait):
    ne = MOE_EXPERTS
    nblk = xs_hbm.shape[0] // (EXP_BLOCK * ROW_SUB)

    def copy(off_rows, n_rows):
        off = pl.multiple_of(off_rows * ROW_SUB, ROW_SUB)
        cp = pltpu.make_async_copy(zero_ref.at[pl.ds(0, n_rows * ROW_SUB)],
                                   xs_hbm.at[pl.ds(off, n_rows * ROW_SUB)], sem)
        cp.wait() if wait else cp.start()

    def expert(e, carry):
        pad = zinfo_ref[ne + e]
        for b in range(EXP_BLOCK.bit_length() - 1):
            @pl.when(((pad >> b) & 1) == 1)
            def _():
                copy(zinfo_ref[e] + (pad & ((1 << b) - 1)), 1 << b)
        return carry

    def tail(kk, carry):
        blk = zinfo_ref[2 * ne] + kk

        @pl.when(blk < nblk)
        def _():
            copy(blk * EXP_BLOCK, EXP_BLOCK)
        return carry

    lax.fori_loop(0, ne, expert, 0)
    lax.fori_loop(0, ne + 1, tail, 0)


RING = 3


def _slot_copies(dest_hbm, i, idx_ref, slot, sem):
    return [pltpu.make_async_copy(dest_hbm.at[a, i], idx_ref.at[slot, a], sem.at[slot]) for a in range(2)]


def _dispatch_kernel(zinfo_ref, dest_hbm, h_hbm, xs_hbm, idx_ref, hbuf_ref, zero_ref, sem_idx, sem_h, sem_rows):
    i = pl.program_id(0)
    nt = pl.num_programs(0)
    tr = ROW_TILE
    rows = tr * ROW_SUB

    def tile_copies(j):
        slot = j % RING
        src = h_hbm.at[pl.ds(pl.multiple_of(j * rows, rows), rows)]
        return [pltpu.make_async_copy(src, hbuf_ref.at[slot], sem_h.at[slot])] + \
            _slot_copies(dest_hbm, j, idx_ref, slot, sem_idx)

    def wait_rows(j):
        def body(t, carry):
            pltpu.make_async_copy(_row(hbuf_ref.at[0], 0), _row(xs_hbm, 0), sem_rows.at[j % 2]).wait()
            return carry
        lax.fori_loop(0, 2 * tr, body, 0)

    @pl.when(i == 0)
    def _():
        for cp in tile_copies(0):
            cp.start()
        zero_ref[...] = jnp.zeros_like(zero_ref)
        _zero_fill(zinfo_ref, zero_ref, xs_hbm, sem_rows.at[1], wait=False)
        _zero_fill(zinfo_ref, zero_ref, xs_hbm, sem_rows.at[1], wait=True)

    @pl.when(i + 1 < nt)
    def _():
        for cp in tile_copies(i + 1):
            cp.start()

    for cp in tile_copies(i):
        cp.wait()
    slot = i % RING

    def start(t, carry):
        src = _row(hbuf_ref.at[slot], t)
        pltpu.make_async_copy(src, _row(xs_hbm, idx_ref[slot, 0, t]), sem_rows.at[i % 2]).start()
        pltpu.make_async_copy(src, _row(xs_hbm, idx_ref[slot, 1, t]), sem_rows.at[i % 2]).start()
        return carry

    lax.fori_loop(0, tr, start, 0)

    @pl.when(i > 0)
    def _():
        wait_rows(i - 1)

    @pl.when(i == nt - 1)
    def _():
        wait_rows(i)


def _dispatch(h, dest, zoff, n_slots):
    t = h.shape[0] // ROW_SUB
    tr = ROW_TILE
    nt = t // tr
    grid_spec = pltpu.PrefetchScalarGridSpec(
        num_scalar_prefetch=1,
        grid=(nt,),
        in_specs=[pl.BlockSpec(memory_space=pl.ANY), pl.BlockSpec(memory_space=pl.ANY)],
        out_specs=pl.BlockSpec(memory_space=pl.ANY),
        scratch_shapes=[pltpu.SMEM((RING, 2, tr), I32), pltpu.VMEM((RING, tr * ROW_SUB, LANES), F32),
                        pltpu.VMEM((EXP_BLOCK * ROW_SUB, LANES), F32),
                        pltpu.SemaphoreType.DMA((RING,)), pltpu.SemaphoreType.DMA((RING,)),
                        pltpu.SemaphoreType.DMA((2,))],
    )
    return pl.pallas_call(
        _dispatch_kernel,
        grid_spec=grid_spec,
        out_shape=jax.ShapeDtypeStruct((n_slots * ROW_SUB, LANES), F32),
        compiler_params=_cparams("arbitrary"),
        name="moe_dispatch",
    )(zoff, dest.reshape(2, nt, tr), h)


def _expert_kernel(be_ref, nused_ref, xs_ref, w1_ref, w3_ref, w2_ref, ys_ref, w1b_ref, w3b_ref, w2b_ref):
    i = pl.program_id(0)
    used = i < nused_ref[0]

    @pl.when(used & ((i == 0) | (be_ref[i] != be_ref[jnp.maximum(i - 1, 0)])))
    def _():
        w1b_ref[...] = w1_ref[0, 0].astype(BF16)
        w3b_ref[...] = w3_ref[0, 0].astype(BF16)
        w2b_ref[...] = w2_ref[0, 0].astype(BF16)

    @pl.when(used)
    def _():
        xb = _load_rows(xs_ref, 0, EXP_BLOCK, ROW_SUB, BF16)
        a = _dot(xb, w1b_ref[...])
        b = _dot(xb, w3b_ref[...])
        hid = (a * jax.nn.sigmoid(a) * b).astype(BF16)
        _store_rows(ys_ref, _dot(hid, w2b_ref[...]))

    @pl.when(i >= nused_ref[0])
    def _():
        ys_ref[...] = jnp.zeros_like(ys_ref)


def _experts(xs, block_e, n_used, w1, w3, w2, layer):
    bm = EXP_BLOCK
    n_slots = xs.shape[0] // ROW_SUB
    nblk = n_slots // bm
    d, ff = w1.shape[-2:]

    def last_used(i, nu):
        return jnp.maximum(jnp.minimum(i, nu[0] - 1), 0)

    def xmap(i, be, nu):
        return (last_used(i, nu), 0)

    def wmap(i, be, nu):
        return (layer, be[last_used(i, nu)], 0, 0)

    grid_spec = pltpu.PrefetchScalarGridSpec(
        num_scalar_prefetch=2,
        grid=(nblk,),
        in_specs=[pl.BlockSpec((bm * ROW_SUB, LANES), xmap), pl.BlockSpec((1, 1, d, ff), wmap),
                  pl.BlockSpec((1, 1, d, ff), wmap), pl.BlockSpec((1, 1, ff, d), wmap)],
        out_specs=pl.BlockSpec((bm * ROW_SUB, LANES), lambda i, be, nu: (i, 0)),
        scratch_shapes=[pltpu.VMEM((d, ff), BF16), pltpu.VMEM((d, ff), BF16), pltpu.VMEM((ff, d), BF16)],
    )
    return pl.pallas_call(
        _expert_kernel,
        grid_spec=grid_spec,
        out_shape=jax.ShapeDtypeStruct((n_slots * ROW_SUB, LANES), F32),
        compiler_params=_cparams("arbitrary"),
        name="moe_experts",
    )(block_e, n_used, xs, w1, w3, w2)


def _combine_kernel(dest_hbm, ys_hbm, x_ref, g_ref, nw_ref, o_ref, idx_ref, buf_ref, sem_idx, sem_rows,
                    *, final_norm):
    i = pl.program_id(0)
    nt = pl.num_programs(0)
    tr = x_ref.shape[0]

    def gather(j):
        islot = j % RING
        buf = buf_ref.at[j % 2]

        def start(t, carry):
            pltpu.make_async_copy(_row(ys_hbm, idx_ref[islot, 0, t]), _row(buf, t), sem_rows.at[j % 2]).start()
            pltpu.make_async_copy(_row(ys_hbm, idx_ref[islot, 1, t]), _row(buf, tr + t), sem_rows.at[j % 2]).start()
            return carry

        lax.fori_loop(0, tr, start, 0)

    @pl.when(i == 0)
    def _():
        for cp in _slot_copies(dest_hbm, 0, idx_ref, 0, sem_idx):
            cp.start()
        for cp in _slot_copies(dest_hbm, 0, idx_ref, 0, sem_idx):
            cp.wait()
        gather(0)

        @pl.when(nt > 1)
        def _():
            for cp in _slot_copies(dest_hbm, 1, idx_ref, 1, sem_idx):
                cp.start()

    @pl.when(i + 1 < nt)
    def _():
        for cp in _slot_copies(dest_hbm, i + 1, idx_ref, (i + 1) % RING, sem_idx):
            cp.wait()
        gather(i + 1)

        @pl.when(i + 2 < nt)
        def _():
            for cp in _slot_copies(dest_hbm, i + 2, idx_ref, (i + 2) % RING, sem_idx):
                cp.start()

    def wait(t, carry):
        pltpu.make_async_copy(_row(ys_hbm, 0), _row(buf_ref.at[0], 0), sem_rows.at[i % 2]).wait()
        return carry

    lax.fori_loop(0, 2 * tr, wait, 0)
    g = g_ref[...]
    buf = buf_ref.at[i % 2]
    y1 = _load_rows(buf, 0, tr, ROW_SUB, F32)
    y2 = _load_rows(buf, tr, tr, ROW_SUB, F32)
    out = x_ref[...] + g[:, 0:1] * y1 + g[:, 1:2] * y2
    if final_norm:
        out = _rms(out, nw_ref[...])
    o_ref[...] = out


def _combine(ys, dest, x, gates, nw, final_norm):
    t, d = x.shape
    tr = ROW_TILE
    nt = t // tr
    g_col = gates.T
    nw2 = nw.reshape(1, d)
    return pl.pallas_call(
        functools.partial(_combine_kernel, final_norm=final_norm),
        grid=(nt,),
        in_specs=[pl.BlockSpec(memory_space=pl.ANY), pl.BlockSpec(memory_space=pl.ANY),
                  pl.BlockSpec((tr, d), lambda i: (i, 0)), pl.BlockSpec((tr, 2), lambda i: (i, 0)),
                  pl.BlockSpec((1, d), lambda i: (0, 0))],
        out_specs=pl.BlockSpec((tr, d), lambda i: (i, 0)),
        out_shape=jax.ShapeDtypeStruct((t, d), F32),
        scratch_shapes=[pltpu.SMEM((RING, 2, tr), I32), pltpu.VMEM((2, 2 * tr * ROW_SUB, LANES), F32),
                        pltpu.SemaphoreType.DMA((RING,)), pltpu.SemaphoreType.DMA((2,))],
        compiler_params=_cparams("arbitrary"),
        name="moe_combine",
    )(dest.reshape(2, nt, tr), ys, x, g_col, nw2)


def _slots_kernel(ids_ref, pst_ref, dest_ref):
    ids = ids_ref[...]
    ei = lax.broadcasted_iota(I32, (MOE_EXPERTS, ids.shape[1]), 0)
    pst = pst_ref[...]
    d1 = jnp.sum(jnp.where(ei == ids[0:1], pst, 0), axis=0, keepdims=True) + ids[2:3]
    d2 = jnp.sum(jnp.where(ei == ids[1:2], pst, 0), axis=0, keepdims=True) + ids[3:4]
    dest_ref[...] = jnp.concatenate([d1, d2], axis=0)


def _slots(ids, pstarts):
    t = ids.shape[1]
    tm = TOK_TILE
    return pl.pallas_call(
        _slots_kernel,
        grid=(t // tm,),
        in_specs=[pl.BlockSpec((4, tm), lambda i: (0, i)), pl.BlockSpec((MOE_EXPERTS, 1), lambda i: (0, 0))],
        out_specs=pl.BlockSpec((2, tm), lambda i: (0, i)),
        out_shape=jax.ShapeDtypeStruct((2, t), I32),
        compiler_params=_cparams("arbitrary"),
        name="moe_slots",
    )(ids, pstarts.reshape(MOE_EXPERTS, 1))


def _moe(h, x1, ids, gates, counts, w1, w3, w2, layer, nw_final, final_norm):
    t = x1.shape[0]
    bm = EXP_BLOCK
    n_asg = 2 * t
    n_slots = n_asg + MOE_EXPERTS * bm
    cnt = counts[:, 0]
    padded = (cnt + bm - 1) // bm * bm
    pend = jnp.cumsum(padded)
    pstarts = (pend - padded).astype(I32)
    dest = _slots(ids, pstarts)
    nblk = n_slots // bm
    block_start = jnp.arange(nblk, dtype=I32) * bm
    block_e = jnp.minimum(jnp.sum(pend[None, :] <= block_start[:, None], axis=1), MOE_EXPERTS - 1).astype(I32)
    n_used = (pend[-1:] // bm).astype(I32)
    zinfo = jnp.concatenate([pstarts + cnt, padded - cnt, n_used]).astype(I32)
    xs = _dispatch(h, dest, zinfo, n_slots)
    ys = _experts(xs, block_e, n_used, w1, w3, w2, layer)
    return _combine(ys, dest, x1, gates, nw_final, final_norm)


def _mla_in_kernel(x_ref, pos_ref, nw_ref, wq_ref, wkv_ref, wks_ref, qn_ref, kvn_ref, wqn_ref, wqr_ref, wqs_ref,
                   wuk_ref, freq_ref, qt_ref, k_ref, vt_ref):
    tm = x_ref.shape[0]
    tb = ATT_BLOCK
    hb = _rms(x_ref[...], nw_ref[...]).astype(BF16)
    c_q = _rms(_dot(hb, wq_ref[...]), qn_ref[...]).astype(BF16)
    kv_t = _dot_nt(wkv_ref[...], hb)
    ks_t = _dot_nt(wks_ref[...], hb)
    c_kv = kv_t[:MLA_KV_LORA]
    c_kv = c_kv * lax.rsqrt(jnp.mean(c_kv * c_kv, axis=0, keepdims=True) + EPS) * kvn_ref[...]
    ang = freq_ref[...] * pos_ref[...].astype(F32)
    cos32, sin32 = jnp.cos(ang), jnp.sin(ang)
    cos = jnp.concatenate([cos32] * 4, axis=0)
    sin = jnp.concatenate([-sin32, sin32] * 2, axis=0)
    k_rope = kv_t[MLA_KV_LORA:] * cos + ks_t * sin
    k_ref[...] = jnp.concatenate([c_kv, k_rope], axis=0).T.astype(BF16)
    v_ext = jnp.concatenate([c_kv, jnp.ones((VT_ROWS - MLA_KV_LORA, tm), F32)], axis=0).astype(BF16)
    for c in range(tm // tb):
        vt_ref[c] = v_ext[:, c * tb:(c + 1) * tb]
    q_nope = _dot_nt(wqn_ref[...], c_q).astype(BF16)
    q_r = _dot_nt(wqr_ref[...], c_q)
    q_s = _dot_nt(wqs_ref[...], c_q)
    for h in range(MLA_HEADS):
        sl = slice(h * LANES, (h + 1) * LANES)
        q_lat = (_dot(wuk_ref[h], q_nope[sl]) * Q_SCALE).astype(BF16)
        q_rope = ((q_r[sl] * cos + q_s[sl] * sin) * Q_SCALE).astype(BF16)
        for c in range(tm // tb):
            cs = slice(c * tb, (c + 1) * tb)
            qt_ref[c, 2 * h * LANES:(2 * h + 1) * LANES, :] = q_lat[:, cs]
            qt_ref[c, (2 * h + 1) * LANES:(2 * h + 2) * LANES, :] = q_rope[:, cs]


def _swap_halves(w):
    half = w.shape[-1] // 2
    return jnp.concatenate([w[..., half:], w[..., :half]], axis=-1)


def _mla_in(x, pos, nw, w_in, q_norm, w_uq, kv_norm, w_uk):
    t, d = x.shape
    tm = TOK_TILE
    tb = ATT_BLOCK
    nh = MLA_HEADS
    pad_r = LANES - MLA_ROPE
    w_kr = w_in[:, MLA_Q_LORA + MLA_KV_LORA:]
    wq = w_in[:, :MLA_Q_LORA].astype(BF16)
    wkv = jnp.pad(w_in[:, MLA_Q_LORA:], ((0, 0), (0, pad_r))).T.astype(BF16)
    wks = jnp.pad(_swap_halves(w_kr), ((0, 0), (0, pad_r))).T.astype(BF16)
    wqn = w_uq[:, :, :MLA_NOPE].reshape(MLA_Q_LORA, nh * MLA_NOPE).T.astype(BF16)
    w_r = w_uq[:, :, MLA_NOPE:]
    wqr = jnp.pad(w_r, ((0, 0), (0, 0), (0, pad_r))).reshape(MLA_Q_LORA, nh * LANES).T.astype(BF16)
    wqs = jnp.pad(_swap_halves(w_r), ((0, 0), (0, 0), (0, pad_r))).reshape(MLA_Q_LORA, nh * LANES).T.astype(BF16)
    wuk = w_uk.transpose(1, 0, 2).astype(BF16)
    inv_freq = 1.0 / (ROPE_THETA ** (jnp.arange(0, MLA_ROPE, 2, dtype=F32) / MLA_ROPE))
    full = lambda a: pl.BlockSpec(a.shape, lambda i: (0,) * a.ndim)
    args = (x, pos.reshape(1, t), nw.reshape(1, d), wq, wkv, wks, q_norm.reshape(1, -1), kv_norm.reshape(-1, 1),
            wqn, wqr, wqs, wuk, inv_freq.reshape(-1, 1))
    nsub = tm // tb
    return pl.pallas_call(
        _mla_in_kernel,
        grid=(t // tm,),
        in_specs=[pl.BlockSpec((tm, d), lambda i: (i, 0)), pl.BlockSpec((1, tm), lambda i: (0, i))]
        + [full(a) for a in args[2:]],
        out_specs=[pl.BlockSpec((nsub, 2 * nh * LANES, tb), lambda i: (i, 0, 0)),
                   pl.BlockSpec((tm, 2 * LANES), lambda i: (i, 0)),
                   pl.BlockSpec((nsub, VT_ROWS, tb), lambda i: (i, 0, 0))],
        out_shape=[jax.ShapeDtypeStruct((t // tb, 2 * nh * LANES, tb), BF16),
                   jax.ShapeDtypeStruct((t, 2 * LANES), BF16),
                   jax.ShapeDtypeStruct((t // tb, VT_ROWS, tb), BF16)],
        compiler_params=_cparams("arbitrary"),
        name="mla_in",
    )(*args)


def _mla_attn_kernel(qt_ref, k_ref, vt_ref, wuv_ref, o_ref, *state):
    i = pl.program_id(1)
    tb = ATT_BLOCK
    nh = MLA_HEADS
    dq = 2 * LANES
    s_refs, m_refs, acc_refs = state[0::3], state[1::3], state[2::3]

    def scores(j, h):
        kb = k_ref[pl.ds(pl.multiple_of(j * tb, tb), tb), :]
        return _dot(kb, qt_ref[0, h * dq:(h + 1) * dq, :])

    def consume(j, h, s, masked):
        if masked:
            kpos = lax.broadcasted_iota(I32, (tb, tb), 0)
            qpos = lax.broadcasted_iota(I32, (tb, tb), 1)
            s = jnp.where(kpos <= qpos, s, -jnp.inf)
        m_old = m_refs[h][...]
        m_new = jnp.maximum(m_old, jnp.max(s, axis=0, keepdims=True))
        alpha = jnp.exp2(m_old - m_new)
        p = jnp.exp2(s - m_new).astype(BF16)
        acc_refs[h][...] = alpha * acc_refs[h][...] + _dot(vt_ref[j], p)
        m_refs[h][...] = m_new

    for h in range(nh):
        m_refs[h][...] = jnp.full_like(m_refs[h], -jnp.inf)
        acc_refs[h][...] = jnp.zeros_like(acc_refs[h])
        s_refs[h][...] = scores(0, h)

    def full_block(j, carry):
        for h in range(nh):
            s = s_refs[h][...]
            s_new = scores(j + 1, h)
            consume(j, h, s, False)
            s_refs[h][...] = s_new
        return carry

    lax.fori_loop(0, i, full_block, 0)
    for h in range(nh):
        consume(i, h, s_refs[h][...], True)
    for h in range(nh):
        acc = acc_refs[h][...]
        o_lat = (acc[:MLA_KV_LORA] / acc[MLA_KV_LORA:MLA_KV_LORA + 1]).astype(BF16)
        o_ref[:, h * MLA_V:(h + 1) * MLA_V] = _dot_tn(o_lat, wuv_ref[h]).astype(BF16)


def _mla_attn(qt, k, vt, w_uv, bsz, seq):
    t = bsz * seq
    tb = ATT_BLOCK
    nq = seq // tb
    nh = MLA_HEADS
    wuv = w_uv.transpose(1, 0, 2).astype(BF16)
    return pl.pallas_call(
        _mla_attn_kernel,
        grid=(bsz, nq),
        in_specs=[pl.BlockSpec((1, 2 * nh * LANES, tb), lambda b, i: (b * nq + i, 0, 0)),
                  pl.BlockSpec((seq, 2 * LANES), lambda b, i: (b, 0)),
                  pl.BlockSpec((nq, VT_ROWS, tb), lambda b, i: (b, 0, 0)),
                  pl.BlockSpec(wuv.shape, lambda b, i: (0, 0, 0))],
        out_specs=pl.BlockSpec((tb, nh * MLA_V), lambda b, i: (b * nq + i, 0)),
        out_shape=jax.ShapeDtypeStruct((t, nh * MLA_V), BF16),
        scratch_shapes=[pltpu.VMEM((tb, tb), F32), pltpu.VMEM((1, tb), F32),
                        pltpu.VMEM((VT_ROWS, tb), F32)] * nh,
        compiler_params=_cparams("arbitrary", "arbitrary"),
        name="mla_attn",
    )(qt, k, vt, wuv)


def kernel(x, positions, attn_norm, ffn_norm, final_norm, gla_w_in, gla_w_gate, gla_b_gate, gla_out_norm, gla_w_o,
           mla_w_in, mla_q_norm, mla_w_uq, mla_kv_norm, mla_w_uk, mla_w_uv, mla_w_o,
           moe_w_group, moe_w_expert, moe_w1, moe_w3, moe_w2):
    bsz, seq, d = x.shape
    t = bsz * seq
    depth = attn_norm.shape[0]
    xf = x.reshape(t, d)
    pos = positions.reshape(t)
    for i in range(depth):
        j = i // 2
        if i % 2 == 0:
            q, k, v, r, la = _gla_in(xf, attn_norm[i], gla_w_in[j], gla_w_gate[j], gla_b_gate[j])
            mixed = _gla_rec(q, k, v, r, la, gla_out_norm[j], bsz, seq)
            w_o = gla_w_o[j]
        else:
            qt, kc, vt = _mla_in(xf, pos, attn_norm[i], mla_w_in[j], mla_q_norm[j], mla_w_uq[j], mla_kv_norm[j],
                                 mla_w_uk[j])
            mixed = _mla_attn(qt, kc, vt, mla_w_uv[j], bsz, seq)
            w_o = mla_w_o[j]
        x1, h, ids, gates, counts = _post(mixed, w_o, xf, ffn_norm[i], moe_w_group[i], moe_w_expert[i])
        last = i == depth - 1
        xf = _moe(h, x1, ids, gates, counts, moe_w1, moe_w3, moe_w2, i, final_norm, last)
    return xf.reshape(bsz, seq, d)
```

```python
import functools

import jax
import jax.numpy as jnp
from jax import lax
from jax.experimental import pallas as pl
from jax.experimental.pallas import tpu as pltpu

F32 = jnp.float32
BF16 = jnp.bfloat16
I32 = jnp.int32
HIGHEST = lax.Precision.HIGHEST

EPS = 1e-6
GLA_HEADS = 4
GLA_DK = 128
GLA_DV = 256
GLA_GATE_RANK = 16
GLA_GATE_TAU = 16.0
GLA_CHUNK = 64
MLA_HEADS = 8
MLA_NOPE = 128
MLA_ROPE = 64
MLA_V = 128
MLA_Q_LORA = 256
MLA_KV_LORA = 128
MLA_SCALE = (MLA_NOPE + MLA_ROPE) ** -0.5
Q_SCALE = MLA_SCALE * 1.4426950408889634
VT_ROWS = MLA_KV_LORA + 16
ROPE_THETA = 10000.0
MOE_GROUPS = 8
MOE_PER_GROUP = 8
MOE_EXPERTS = MOE_GROUPS * MOE_PER_GROUP
MOE_FF = 256

D_MODEL = 1024
LANES = 128
ROW_SUB = D_MODEL // LANES
VMEM_LIMIT = 48 * 1024 * 1024

TOK_TILE = 512
GLA_BLOCK = 256
ATT_BLOCK = 256
ROW_TILE = 256
EXP_BLOCK = 256


def _cparams(*sem):
    return pltpu.CompilerParams(dimension_semantics=sem, vmem_limit_bytes=VMEM_LIMIT)


def _rms(x, w):
    return x * lax.rsqrt(jnp.mean(x * x, axis=-1, keepdims=True) + EPS) * w


def _dot(a, b):
    return jnp.dot(a, b, preferred_element_type=F32)


def _dot_nt(a, b):
    return lax.dot_general(a, b, (((1,), (1,)), ((), ())), preferred_element_type=F32)


def _dot_tn(a, b):
    return lax.dot_general(a, b, (((0,), (0,)), ((), ())), preferred_element_type=F32)


def _gla_in_kernel(x_ref, nw_ref, wq_ref, wk_ref, wv_ref, wr_ref, wa_ref, wg_ref, bg_ref,
                   q_ref, k_ref, v_ref, r_ref, la_ref):
    hb = _rms(x_ref[...], nw_ref[...]).astype(BF16)
    q_ref[...] = _dot(hb, wq_ref[...]) * (GLA_DK ** -0.5)
    k_ref[...] = _dot(hb, wk_ref[...])
    v_ref[...] = _dot(hb, wv_ref[...]).astype(BF16)
    r_ref[...] = _dot(hb, wr_ref[...])
    a_lr = _dot(hb, wa_ref[...])
    z = jnp.dot(a_lr, wg_ref[...], precision=HIGHEST, preferred_element_type=F32) + bg_ref[...]
    log_sig = jnp.minimum(z, 0.0) - jnp.log1p(jnp.exp(-jnp.abs(z)))
    la_ref[...] = log_sig * (1.0 / GLA_GATE_TAU)


def _gla_in(x, nw, w_in, w_gate, b_gate):
    t, d = x.shape
    qk = GLA_HEADS * GLA_DK
    vd = GLA_HEADS * GLA_DV
    wq = w_in[:, :qk].astype(BF16)
    wk = w_in[:, qk:2 * qk].astype(BF16)
    wv = w_in[:, 2 * qk:2 * qk + vd].astype(BF16)
    wa = jnp.pad(w_in[:, 2 * qk + vd:2 * qk + vd + GLA_GATE_RANK], ((0, 0), (0, LANES - GLA_GATE_RANK))).astype(BF16)
    wr = w_in[:, 2 * qk + vd + GLA_GATE_RANK:].astype(BF16)
    wg = jnp.pad(w_gate, ((0, LANES - GLA_GATE_RANK), (0, 0)))
    tm = TOK_TILE
    row = lambda n: pl.BlockSpec((tm, n), lambda i: (i, 0))
    full = lambda a: pl.BlockSpec(a.shape, lambda i: (0, 0))
    nw2, bg2 = nw.reshape(1, d), b_gate.reshape(1, qk)
    return pl.pallas_call(
        _gla_in_kernel,
        grid=(t // tm,),
        in_specs=[row(d), full(nw2), full(wq), full(wk), full(wv), full(wr), full(wa), full(wg), full(bg2)],
        out_specs=[row(qk), row(qk), row(vd), row(vd), row(qk)],
        out_shape=[jax.ShapeDtypeStruct((t, qk), F32), jax.ShapeDtypeStruct((t, qk), F32),
                   jax.ShapeDtypeStruct((t, vd), BF16), jax.ShapeDtypeStruct((t, vd), F32),
                   jax.ShapeDtypeStruct((t, qk), F32)],
        compiler_params=_cparams("arbitrary"),
        name="gla_in",
    )(x, nw2, wq, wk, wv, wr, wa, wg, bg2)


def _gla_rec_kernel(q_ref, k_ref, v_ref, r_ref, la_ref, onw_ref, o_ref, st_ref):
    c = GLA_CHUNK

    @pl.when(pl.program_id(1) == 0)
    def _():
        st_ref[...] = jnp.zeros_like(st_ref)

    rows = lax.broadcasted_iota(I32, (c, c), 0)
    cols = lax.broadcasted_iota(I32, (c, c), 1)
    causal = cols <= rows
    tril = causal.astype(F32)
    onw = onw_ref[...]
    for ci in range(GLA_BLOCK // c):
        sl = pl.ds(ci * c, c)
        g_all = jnp.dot(tril, la_ref[sl, :], precision=HIGHEST, preferred_element_type=F32)
        for h in range(GLA_HEADS):
            ks = pl.ds(h * GLA_DK, GLA_DK)
            vs = pl.ds(h * GLA_DV, GLA_DV)
            g = g_all[:, h * GLA_DK:(h + 1) * GLA_DK]
            g_mid = g[c // 2:c // 2 + 1]
            g_last = g[c - 1:c]
            q = q_ref[sl, ks]
            k = k_ref[sl, ks]
            v = v_ref[sl, vs]
            a = _dot_nt((q * jnp.exp(g - g_mid)).astype(BF16), (k * jnp.exp(g_mid - g)).astype(BF16))
            a = jnp.where(causal, a, 0.0).astype(BF16)
            st = st_ref[h]
            o = _dot(a, v) + _dot_nt((q * jnp.exp(g)).astype(BF16), st.astype(BF16))
            k_state = (k * jnp.exp(g_last - g)).astype(BF16)
            st_ref[h] = st * jnp.exp(g_last) + _dot_tn(v, k_state)
            o = _rms(o, onw)
            r = r_ref[sl, vs]
            o_ref[sl, vs] = (o * (r * jax.nn.sigmoid(r))).astype(BF16)


def _gla_rec(q, k, v, r, la, onw, bsz, seq):
    t = bsz * seq
    qk = GLA_HEADS * GLA_DK
    vd = GLA_HEADS * GLA_DV
    nb = seq // GLA_BLOCK
    row = lambda n: pl.BlockSpec((GLA_BLOCK, n), lambda b, j: (b * nb + j, 0))
    onw2 = onw.reshape(1, GLA_DV)
    return pl.pallas_call(
        _gla_rec_kernel,
        grid=(bsz, nb),
        in_specs=[row(qk), row(qk), row(vd), row(vd), row(qk), pl.BlockSpec((1, GLA_DV), lambda b, j: (0, 0))],
        out_specs=row(vd),
        out_shape=jax.ShapeDtypeStruct((t, vd), BF16),
        scratch_shapes=[pltpu.VMEM((GLA_HEADS, GLA_DV, GLA_DK), F32)],
        compiler_params=_cparams("arbitrary", "arbitrary"),
        name="gla_rec",
    )(q, k, v, r, la, onw2)


def _store_rows(ref, val):
    n, d = val.shape
    nc = d // LANES
    for c in range(nc):
        ref[pl.ds(c, n, stride=nc), :] = val[:, c * LANES:(c + 1) * LANES]


def _load_rows(ref, start, n, nc, dtype):
    return jnp.concatenate([ref[pl.ds(start * nc + c, n, stride=nc), :].astype(dtype) for c in range(nc)], axis=1)


def _post_kernel(a_ref, wo_ref, x_ref, nw_ref, wr_ref,
                 x1_ref, h_ref, ids_ref, gates_ref, cnt_ref, carry_ref):
    tm = a_ref.shape[0]
    ne = MOE_EXPERTS

    @pl.when(pl.program_id(0) == 0)
    def _():
        carry_ref[...] = jnp.zeros_like(carry_ref)

    x1 = x_ref[...] + _dot(a_ref[...], wo_ref[...])
    x1_ref[...] = x1
    h = _rms(x1, nw_ref[...])
    _store_rows(h_ref, h)
    lt = lax.dot_general(wr_ref[...], h, (((1,), (1,)), ((), ())), precision=HIGHEST,
                         preferred_element_type=F32)
    gl = lt[0:MOE_GROUPS]
    gmax = jnp.max(gl, axis=0, keepdims=True)
    gi = lax.broadcasted_iota(I32, gl.shape, 0)
    g_sel = jnp.min(jnp.where(gl == gmax, gi, MOE_GROUPS), axis=0, keepdims=True)
    g_w = 1.0 / jnp.sum(jnp.exp(gl - gmax), axis=0, keepdims=True)
    el = lt[MOE_GROUPS:MOE_GROUPS + ne]
    ei = lax.broadcasted_iota(I32, el.shape, 0)
    in_group = (ei // MOE_PER_GROUP) == g_sel
    neg = jnp.float32(-jnp.inf)
    el1 = jnp.where(in_group, el, neg)
    l1 = jnp.max(el1, axis=0, keepdims=True)
    i1 = jnp.min(jnp.where(el1 == l1, ei, ne), axis=0, keepdims=True)
    el2 = jnp.where(ei == i1, neg, el1)
    l2 = jnp.max(el2, axis=0, keepdims=True)
    i2 = jnp.min(jnp.where(el2 == l2, ei, ne), axis=0, keepdims=True)
    e2 = jnp.exp(l2 - l1)
    gate1 = g_w / (1.0 + e2)
    gate2 = g_w * e2 / (1.0 + e2)
    oh1 = ei == i1
    oh2 = ei == i2
    both = jnp.where(oh1 | oh2, 1.0, 0.0)
    su = lax.broadcasted_iota(I32, (tm, tm), 0)
    tu = lax.broadcasted_iota(I32, (tm, tm), 1)
    upper = jnp.where(su < tu, 1.0, 0.0).astype(BF16)
    prefix = _dot(both.astype(BF16), upper) + carry_ref[:, 0:1]
    rank1 = jnp.sum(jnp.where(oh1, prefix, 0.0), axis=0, keepdims=True)
    rank2 = jnp.sum(jnp.where(oh2, prefix, 0.0), axis=0, keepdims=True)
    carry = carry_ref[...] + jnp.sum(both, axis=1, keepdims=True)
    carry_ref[...] = carry
    cnt_ref[...] = carry.astype(I32)
    ids_ref[...] = jnp.concatenate([i1, i2, rank1.astype(I32), rank2.astype(I32)], axis=0)
    gates_ref[...] = jnp.concatenate([gate1, gate2], axis=0)


def _post(a, wo, x, nw, w_group, w_expert):
    t, d = x.shape
    tm = TOK_TILE
    wr = jnp.concatenate([w_group.T, w_expert.T], axis=0)
    wr = jnp.pad(wr, ((0, LANES - wr.shape[0]), (0, 0)))
    wo = wo.astype(BF16)
    nw2 = nw.reshape(1, d)
    row = lambda n: pl.BlockSpec((tm, n), lambda i: (i, 0))
    full = lambda arr: pl.BlockSpec(arr.shape, lambda i: (0, 0))
    return pl.pallas_call(
        _post_kernel,
        grid=(t // tm,),
        in_specs=[row(a.shape[1]), full(wo), row(d), full(nw2), full(wr)],
        out_specs=[row(d), pl.BlockSpec((tm * d // LANES, LANES), lambda i: (i, 0)),
                   pl.BlockSpec((4, tm), lambda i: (0, i)), pl.BlockSpec((2, tm), lambda i: (0, i)),
                   pl.BlockSpec((MOE_EXPERTS, LANES), lambda i: (0, 0))],
        out_shape=[jax.ShapeDtypeStruct((t, d), F32), jax.ShapeDtypeStruct((t * d // LANES, LANES), F32),
                   jax.ShapeDtypeStruct((4, t), I32), jax.ShapeDtypeStruct((2, t), F32),
                   jax.ShapeDtypeStruct((MOE_EXPERTS, LANES), I32)],
        scratch_shapes=[pltpu.VMEM((MOE_EXPERTS, LANES), F32)],
        compiler_params=_cparams("arbitrary"),
        name="post_router",
    )(a, wo, x, nw2, wr)


def _row(ref, i):
    return ref.at[pl.ds(pl.multiple_of(i * ROW_SUB, ROW_SUB), ROW_SUB)]


def _zero_fill(zinfo_ref, zero_ref, xs_hbm, sem, wait):
    ne = MOE_EXPERTS
    nblk = xs_hbm.shape[0] // (EXP_BLOCK * ROW_SUB)

    def copy(off_rows, n_rows):
        off = pl.multiple_of(off_rows * ROW_SUB, ROW_SUB)
        cp = pltpu.make_async_copy(zero_ref.at[pl.ds(0, n_rows * ROW_SUB)],
                                   xs_hbm.at[pl.ds(off, n_rows * ROW_SUB)], sem)
        cp.wait() if wait else cp.start()

    def expert(e, carry):
        pad = zinfo_ref[ne + e]
        for b in range(EXP_BLOCK.bit_length() - 1):
            @pl.when(((pad >> b) & 1) == 1)
            def _():
                copy(zinfo_ref[e] + (pad & ((1 << b) - 1)), 1 << b)
        return carry

    def tail(kk, carry):
        blk = zinfo_ref[2 * ne] + kk

        @pl.when(blk < nblk)
        def _():
            copy(blk * EXP_BLOCK, EXP_BLOCK)
        return carry

    lax.fori_loop(0, ne, expert, 0)
    lax.fori_loop(0, ne + 1, tail, 0)


RING = 3
ISSUE_UNROLL = 8


def _slot_copies(dest_hbm, i, idx_ref, slot, sem):
    return [pltpu.make_async_copy(dest_hbm.at[a, i], idx_ref.at[slot, a], sem.at[slot]) for a in range(2)]


def _dispatch_kernel(zinfo_ref, dest_hbm, h_hbm, xs_hbm, idx_ref, hbuf_ref, zero_ref, sem_idx, sem_h, sem_rows):
    i = pl.program_id(0)
    nt = pl.num_programs(0)
    tr = ROW_TILE
    rows = tr * ROW_SUB

    def tile_copies(j):
        slot = j % RING
        src = h_hbm.at[pl.ds(pl.multiple_of(j * rows, rows), rows)]
        return [pltpu.make_async_copy(src, hbuf_ref.at[slot], sem_h.at[slot])] + \
            _slot_copies(dest_hbm, j, idx_ref, slot, sem_idx)

    def scatter_rows(slot):
        def body(t, carry):
            src = _row(hbuf_ref.at[slot], t)
            pltpu.make_async_copy(src, _row(xs_hbm, idx_ref[slot, 0, t]), sem_rows.at[slot]).start()
            pltpu.make_async_copy(src, _row(xs_hbm, idx_ref[slot, 1, t]), sem_rows.at[slot]).start()
            return carry
        lax.fori_loop(0, tr, body, 0, unroll=ISSUE_UNROLL)

    def wait_rows(slot):
        def body(t, carry):
            pltpu.make_async_copy(_row(hbuf_ref.at[0], 0), _row(xs_hbm, 0), sem_rows.at[slot]).wait()
            return carry
        lax.fori_loop(0, 2 * tr, body, 0, unroll=ISSUE_UNROLL)

    @pl.when(i == 0)
    def _():
        for cp in tile_copies(0):
            cp.start()
        zero_ref[...] = jnp.zeros_like(zero_ref)
        _zero_fill(zinfo_ref, zero_ref, xs_hbm, sem_rows.at[1], wait=False)
        _zero_fill(zinfo_ref, zero_ref, xs_hbm, sem_rows.at[1], wait=True)

    @pl.when(i + 1 < nt)
    def _():
        for cp in tile_copies(i + 1):
            cp.start()

    for cp in tile_copies(i):
        cp.wait()

    for slot in range(RING):
        @pl.when(i % RING == slot)
        def _():
            scatter_rows(slot)

            @pl.when(i > 0)
            def _():
                wait_rows((slot - 1) % RING)

            @pl.when(i == nt - 1)
            def _():
                wait_rows(slot)


def _dispatch(h, dest, zoff, n_slots):
    t = h.shape[0] // ROW_SUB
    tr = ROW_TILE
    nt = t // tr
    grid_spec = pltpu.PrefetchScalarGridSpec(
        num_scalar_prefetch=1,
        grid=(nt,),
        in_specs=[pl.BlockSpec(memory_space=pl.ANY), pl.BlockSpec(memory_space=pl.ANY)],
        out_specs=pl.BlockSpec(memory_space=pl.ANY),
        scratch_shapes=[pltpu.SMEM((RING, 2, tr), I32), pltpu.VMEM((RING, tr * ROW_SUB, LANES), F32),
                        pltpu.VMEM((EXP_BLOCK * ROW_SUB, LANES), F32),
                        pltpu.SemaphoreType.DMA((RING,)), pltpu.SemaphoreType.DMA((RING,)),
                        pltpu.SemaphoreType.DMA((RING,))],
    )
    return pl.pallas_call(
        _dispatch_kernel,
        grid_spec=grid_spec,
        out_shape=jax.ShapeDtypeStruct((n_slots * ROW_SUB, LANES), F32),
        compiler_params=_cparams("arbitrary"),
        name="moe_dispatch",
    )(zoff, dest.reshape(2, nt, tr), h)


def _expert_kernel(be_ref, nused_ref, xs_ref, w1_ref, w3_ref, w2_ref, ys_ref, w1b_ref, w3b_ref, w2b_ref):
    i = pl.program_id(0)
    used = i < nused_ref[0]

    @pl.when(used & ((i == 0) | (be_ref[i] != be_ref[jnp.maximum(i - 1, 0)])))
    def _():
        w1b_ref[...] = w1_ref[0, 0].astype(BF16)
        w3b_ref[...] = w3_ref[0, 0].astype(BF16)
        w2b_ref[...] = w2_ref[0, 0].astype(BF16)

    @pl.when(used)
    def _():
        xb = _load_rows(xs_ref, 0, EXP_BLOCK, ROW_SUB, BF16)
        a = _dot(xb, w1b_ref[...])
        b = _dot(xb, w3b_ref[...])
        hid = (a * jax.nn.sigmoid(a) * b).astype(BF16)
        _store_rows(ys_ref, _dot(hid, w2b_ref[...]))

    @pl.when(i >= nused_ref[0])
    def _():
        ys_ref[...] = jnp.zeros_like(ys_ref)


def _experts(xs, block_e, n_used, w1, w3, w2, layer):
    bm = EXP_BLOCK
    n_slots = xs.shape[0] // ROW_SUB
    nblk = n_slots // bm
    d, ff = w1.shape[-2:]

    def last_used(i, nu):
        return jnp.maximum(jnp.minimum(i, nu[0] - 1), 0)

    def xmap(i, be, nu):
        return (last_used(i, nu), 0)

    def wmap(i, be, nu):
        return (layer, be[last_used(i, nu)], 0, 0)

    grid_spec = pltpu.PrefetchScalarGridSpec(
        num_scalar_prefetch=2,
        grid=(nblk,),
        in_specs=[pl.BlockSpec((bm * ROW_SUB, LANES), xmap), pl.BlockSpec((1, 1, d, ff), wmap),
                  pl.BlockSpec((1, 1, d, ff), wmap), pl.BlockSpec((1, 1, ff, d), wmap)],
        out_specs=pl.BlockSpec((bm * ROW_SUB, LANES), lambda i, be, nu: (i, 0)),
        scratch_shapes=[pltpu.VMEM((d, ff), BF16), pltpu.VMEM((d, ff), BF16), pltpu.VMEM((ff, d), BF16)],
    )
    return pl.pallas_call(
        _expert_kernel,
        grid_spec=grid_spec,
        out_shape=jax.ShapeDtypeStruct((n_slots * ROW_SUB, LANES), F32),
        compiler_params=_cparams("arbitrary"),
        name="moe_experts",
    )(block_e, n_used, xs, w1, w3, w2)


def _combine_kernel(dest_hbm, ys_hbm, x_ref, g_ref, nw_ref, o_ref, idx_ref, buf_ref, sem_idx, sem_rows,
                    *, final_norm):
    i = pl.program_id(0)
    nt = pl.num_programs(0)
    tr = x_ref.shape[0]

    def gather(slot):
        buf = buf_ref.at[slot]

        def body(t, carry):
            pltpu.make_async_copy(_row(ys_hbm, idx_ref[slot, 0, t]), _row(buf, t), sem_rows.at[slot]).start()
            pltpu.make_async_copy(_row(ys_hbm, idx_ref[slot, 1, t]), _row(buf, tr + t), sem_rows.at[slot]).start()
            return carry

        lax.fori_loop(0, tr, body, 0, unroll=ISSUE_UNROLL)

    def wait_rows(slot):
        def body(t, carry):
            pltpu.make_async_copy(_row(ys_hbm, 0), _row(buf_ref.at[0], 0), sem_rows.at[slot]).wait()
            return carry
        lax.fori_loop(0, 2 * tr, body, 0, unroll=ISSUE_UNROLL)

    @pl.when(i == 0)
    def _():
        for cp in _slot_copies(dest_hbm, 0, idx_ref, 0, sem_idx):
            cp.start()
        for cp in _slot_copies(dest_hbm, 0, idx_ref, 0, sem_idx):
            cp.wait()
        gather(0)

        @pl.when(nt > 1)
        def _():
            for cp in _slot_copies(dest_hbm, 1, idx_ref, 1, sem_idx):
                cp.start()

    for slot in range(2):
        @pl.when(i % 2 == slot)
        def _():
            nxt = 1 - slot

            @pl.when(i + 1 < nt)
            def _():
                for cp in _slot_copies(dest_hbm, i + 1, idx_ref, nxt, sem_idx):
                    cp.wait()
                gather(nxt)

            @pl.when(i + 2 < nt)
            def _():
                for cp in _slot_copies(dest_hbm, i + 2, idx_ref, slot, sem_idx):
                    cp.start()

            wait_rows(slot)
            g = g_ref[...]
            y1 = _load_rows(buf_ref.at[slot], 0, tr, ROW_SUB, F32)
            y2 = _load_rows(buf_ref.at[slot], tr, tr, ROW_SUB, F32)
            out = x_ref[...] + g[:, 0:1] * y1 + g[:, 1:2] * y2
            if final_norm:
                out = _rms(out, nw_ref[...])
            o_ref[...] = out


def _combine(ys, dest, x, gates, nw, final_norm):
    t, d = x.shape
    tr = ROW_TILE
    nt = t // tr
    g_col = gates.T
    nw2 = nw.reshape(1, d)
    return pl.pallas_call(
        functools.partial(_combine_kernel, final_norm=final_norm),
        grid=(nt,),
        in_specs=[pl.BlockSpec(memory_space=pl.ANY), pl.BlockSpec(memory_space=pl.ANY),
                  pl.BlockSpec((tr, d), lambda i: (i, 0)), pl.BlockSpec((tr, 2), lambda i: (i, 0)),
                  pl.BlockSpec((1, d), lambda i: (0, 0))],
        out_specs=pl.BlockSpec((tr, d), lambda i: (i, 0)),
        out_shape=jax.ShapeDtypeStruct((t, d), F32),
        scratch_shapes=[pltpu.SMEM((2, 2, tr), I32), pltpu.VMEM((2, 2 * tr * ROW_SUB, LANES), F32),
                        pltpu.SemaphoreType.DMA((2,)), pltpu.SemaphoreType.DMA((2,))],
        compiler_params=_cparams("arbitrary"),
        name="moe_combine",
    )(dest.reshape(2, nt, tr), ys, x, g_col, nw2)


def _slots_kernel(ids_ref, pst_ref, dest_ref):
    ids = ids_ref[...]
    ei = lax.broadcasted_iota(I32, (MOE_EXPERTS, ids.shape[1]), 0)
    pst = pst_ref[...]
    d1 = jnp.sum(jnp.where(ei == ids[0:1], pst, 0), axis=0, keepdims=True) + ids[2:3]
    d2 = jnp.sum(jnp.where(ei == ids[1:2], pst, 0), axis=0, keepdims=True) + ids[3:4]
    dest_ref[...] = jnp.concatenate([d1, d2], axis=0)


def _slots(ids, pstarts):
    t = ids.shape[1]
    tm = TOK_TILE
    return pl.pallas_call(
        _slots_kernel,
        grid=(t // tm,),
        in_specs=[pl.BlockSpec((4, tm), lambda i: (0, i)), pl.BlockSpec((MOE_EXPERTS, 1), lambda i: (0, 0))],
        out_specs=pl.BlockSpec((2, tm), lambda i: (0, i)),
        out_shape=jax.ShapeDtypeStruct((2, t), I32),
        compiler_params=_cparams("arbitrary"),
        name="moe_slots",
    )(ids, pstarts.reshape(MOE_EXPERTS, 1))


def _moe(h, x1, ids, gates, counts, w1, w3, w2, layer, nw_final, final_norm):
    t = x1.shape[0]
    bm = EXP_BLOCK
    n_asg = 2 * t
    n_slots = n_asg + MOE_EXPERTS * bm
    cnt = counts[:, 0]
    padded = (cnt + bm - 1) // bm * bm
    pend = jnp.cumsum(padded)
    pstarts = (pend - padded).astype(I32)
    dest = _slots(ids, pstarts)
    nblk = n_slots // bm
    block_start = jnp.arange(nblk, dtype=I32) * bm
    block_e = jnp.minimum(jnp.sum(pend[None, :] <= block_start[:, None], axis=1), MOE_EXPERTS - 1).astype(I32)
    n_used = (pend[-1:] // bm).astype(I32)
    zinfo = jnp.concatenate([pstarts + cnt, padded - cnt, n_used]).astype(I32)
    xs = _dispatch(h, dest, zinfo, n_slots)
    ys = _experts(xs, block_e, n_used, w1, w3, w2, layer)
    return _combine(ys, dest, x1, gates, nw_final, final_norm)


def _mla_in_kernel(x_ref, pos_ref, nw_ref, wq_ref, wkv_ref, wks_ref, qn_ref, kvn_ref, wqn_ref, wqr_ref, wqs_ref,
                   wuk_ref, freq_ref, qt_ref, k_ref, vt_ref):
    tm = x_ref.shape[0]
    tb = ATT_BLOCK
    hb = _rms(x_ref[...], nw_ref[...]).astype(BF16)
    c_q = _rms(_dot(hb, wq_ref[...]), qn_ref[...]).astype(BF16)
    kv_t = _dot_nt(wkv_ref[...], hb)
    ks_t = _dot_nt(wks_ref[...], hb)
    c_kv = kv_t[:MLA_KV_LORA]
    c_kv = c_kv * lax.rsqrt(jnp.mean(c_kv * c_kv, axis=0, keepdims=True) + EPS) * kvn_ref[...]
    ang = freq_ref[...] * pos_ref[...].astype(F32)
    cos32, sin32 = jnp.cos(ang), jnp.sin(ang)
    cos = jnp.concatenate([cos32] * 4, axis=0)
    sin = jnp.concatenate([-sin32, sin32] * 2, axis=0)
    k_rope = kv_t[MLA_KV_LORA:] * cos + ks_t * sin
    k_ref[...] = jnp.concatenate([c_kv, k_rope], axis=0).T.astype(BF16)
    v_ext = jnp.concatenate([c_kv, jnp.ones((VT_ROWS - MLA_KV_LORA, tm), F32)], axis=0).astype(BF16)
    for c in range(tm // tb):
        vt_ref[c] = v_ext[:, c * tb:(c + 1) * tb]
    q_nope = _dot_nt(wqn_ref[...], c_q).astype(BF16)
    q_r = _dot_nt(wqr_ref[...], c_q)
    q_s = _dot_nt(wqs_ref[...], c_q)
    for h in range(MLA_HEADS):
        sl = slice(h * LANES, (h + 1) * LANES)
        q_lat = (_dot(wuk_ref[h], q_nope[sl]) * Q_SCALE).astype(BF16)
        q_rope = ((q_r[sl] * cos + q_s[sl] * sin) * Q_SCALE).astype(BF16)
        for c in range(tm // tb):
            cs = slice(c * tb, (c + 1) * tb)
            qt_ref[c, 2 * h * LANES:(2 * h + 1) * LANES, :] = q_lat[:, cs]
            qt_ref[c, (2 * h + 1) * LANES:(2 * h + 2) * LANES, :] = q_rope[:, cs]


def _swap_halves(w):
    half = w.shape[-1] // 2
    return jnp.concatenate([w[..., half:], w[..., :half]], axis=-1)


def _mla_in(x, pos, nw, w_in, q_norm, w_uq, kv_norm, w_uk):
    t, d = x.shape
    tm = TOK_TILE
    tb = ATT_BLOCK
    nh = MLA_HEADS
    pad_r = LANES - MLA_ROPE
    w_kr = w_in[:, MLA_Q_LORA + MLA_KV_LORA:]
    wq = w_in[:, :MLA_Q_LORA].astype(BF16)
    wkv = jnp.pad(w_in[:, MLA_Q_LORA:], ((0, 0), (0, pad_r))).T.astype(BF16)
    wks = jnp.pad(_swap_halves(w_kr), ((0, 0), (0, pad_r))).T.astype(BF16)
    wqn = w_uq[:, :, :MLA_NOPE].reshape(MLA_Q_LORA, nh * MLA_NOPE).T.astype(BF16)
    w_r = w_uq[:, :, MLA_NOPE:]
    wqr = jnp.pad(w_r, ((0, 0), (0, 0), (0, pad_r))).reshape(MLA_Q_LORA, nh * LANES).T.astype(BF16)
    wqs = jnp.pad(_swap_halves(w_r), ((0, 0), (0, 0), (0, pad_r))).reshape(MLA_Q_LORA, nh * LANES).T.astype(BF16)
    wuk = w_uk.transpose(1, 0, 2).astype(BF16)
    inv_freq = 1.0 / (ROPE_THETA ** (jnp.arange(0, MLA_ROPE, 2, dtype=F32) / MLA_ROPE))
    full = lambda a: pl.BlockSpec(a.shape, lambda i: (0,) * a.ndim)
    args = (x, pos.reshape(1, t), nw.reshape(1, d), wq, wkv, wks, q_norm.reshape(1, -1), kv_norm.reshape(-1, 1),
            wqn, wqr, wqs, wuk, inv_freq.reshape(-1, 1))
    nsub = tm // tb
    return pl.pallas_call(
        _mla_in_kernel,
        grid=(t // tm,),
        in_specs=[pl.BlockSpec((tm, d), lambda i: (i, 0)), pl.BlockSpec((1, tm), lambda i: (0, i))]
        + [full(a) for a in args[2:]],
        out_specs=[pl.BlockSpec((nsub, 2 * nh * LANES, tb), lambda i: (i, 0, 0)),
                   pl.BlockSpec((tm, 2 * LANES), lambda i: (i, 0)),
                   pl.BlockSpec((nsub, VT_ROWS, tb), lambda i: (i, 0, 0))],
        out_shape=[jax.ShapeDtypeStruct((t // tb, 2 * nh * LANES, tb), BF16),
                   jax.ShapeDtypeStruct((t, 2 * LANES), BF16),
                   jax.ShapeDtypeStruct((t // tb, VT_ROWS, tb), BF16)],
        compiler_params=_cparams("arbitrary"),
        name="mla_in",
    )(*args)


def _mla_attn_kernel(qt_ref, k_ref, vt_ref, wuv_ref, o_ref, *state):
    i = pl.program_id(1)
    tb = ATT_BLOCK
    nh = MLA_HEADS
    dq = 2 * LANES
    s_refs, m_refs, acc_refs = state[0::3], state[1::3], state[2::3]

    def scores(j, h):
        kb = k_ref[pl.ds(pl.multiple_of(j * tb, tb), tb), :]
        return _dot(kb, qt_ref[0, h * dq:(h + 1) * dq, :])

    def consume(j, h, s, masked):
        if masked:
            kpos = lax.broadcasted_iota(I32, (tb, tb), 0)
            qpos = lax.broadcasted_iota(I32, (tb, tb), 1)
            s = jnp.where(kpos <= qpos, s, -jnp.inf)
        m_old = m_refs[h][...]
        m_new = jnp.maximum(m_old, jnp.max(s, axis=0, keepdims=True))
        alpha = jnp.exp2(m_old - m_new)
        p = jnp.exp2(s - m_new).astype(BF16)
        acc_refs[h][...] = alpha * acc_refs[h][...] + _dot(vt_ref[j], p)
        m_refs[h][...] = m_new

    for h in range(nh):
        m_refs[h][...] = jnp.full_like(m_refs[h], -jnp.inf)
        acc_refs[h][...] = jnp.zeros_like(acc_refs[h])
        s_refs[h][...] = scores(0, h)

    def full_block(j, carry):
        for h in range(nh):
            s = s_refs[h][...]
            s_new = scores(j + 1, h)
            consume(j, h, s, False)
            s_refs[h][...] = s_new
        return carry

    lax.fori_loop(0, i, full_block, 0)
    for h in range(nh):
        consume(i, h, s_refs[h][...], True)
    for h in range(nh):
        acc = acc_refs[h][...]
        o_lat = (acc[:MLA_KV_LORA] / acc[MLA_KV_LORA:MLA_KV_LORA + 1]).astype(BF16)
        o_ref[:, h * MLA_V:(h + 1) * MLA_V] = _dot_tn(o_lat, wuv_ref[h]).astype(BF16)


def _mla_attn(qt, k, vt, w_uv, bsz, seq):
    t = bsz * seq
    tb = ATT_BLOCK
    nq = seq // tb
    nh = MLA_HEADS
    wuv = w_uv.transpose(1, 0, 2).astype(BF16)
    return pl.pallas_call(
        _mla_attn_kernel,
        grid=(bsz, nq),
        in_specs=[pl.BlockSpec((1, 2 * nh * LANES, tb), lambda b, i: (b * nq + i, 0, 0)),
                  pl.BlockSpec((seq, 2 * LANES), lambda b, i: (b, 0)),
                  pl.BlockSpec((nq, VT_ROWS, tb), lambda b, i: (b, 0, 0)),
                  pl.BlockSpec(wuv.shape, lambda b, i: (0, 0, 0))],
        out_specs=pl.BlockSpec((tb, nh * MLA_V), lambda b, i: (b * nq + i, 0)),
        out_shape=jax.ShapeDtypeStruct((t, nh * MLA_V), BF16),
        scratch_shapes=[pltpu.VMEM((tb, tb), F32), pltpu.VMEM((1, tb), F32),
                        pltpu.VMEM((VT_ROWS, tb), F32)] * nh,
        compiler_params=_cparams("arbitrary", "arbitrary"),
        name="mla_attn",
    )(qt, k, vt, wuv)


def kernel(x, positions, attn_norm, ffn_norm, final_norm, gla_w_in, gla_w_gate, gla_b_gate, gla_out_norm, gla_w_o,
           mla_w_in, mla_q_norm, mla_w_uq, mla_kv_norm, mla_w_uk, mla_w_uv, mla_w_o,
           moe_w_group, moe_w_expert, moe_w1, moe_w3, moe_w2):
    bsz, seq, d = x.shape
    t = bsz * seq
    depth = attn_norm.shape[0]
    xf = x.reshape(t, d)
    pos = positions.reshape(t)
    for i in range(depth):
        j = i // 2
        if i % 2 == 0:
            q, k, v, r, la = _gla_in(xf, attn_norm[i], gla_w_in[j], gla_w_gate[j], gla_b_gate[j])
            mixed = _gla_rec(q, k, v, r, la, gla_out_norm[j], bsz, seq)
            w_o = gla_w_o[j]
        else:
            qt, kc, vt = _mla_in(xf, pos, attn_norm[i], mla_w_in[j], mla_q_norm[j], mla_w_uq[j], mla_kv_norm[j],
                                 mla_w_uk[j])
            mixed = _mla_attn(qt, kc, vt, mla_w_uv[j], bsz, seq)
            w_o = mla_w_o[j]
        x1, h, ids, gates, counts = _post(mixed, w_o, xf, ffn_norm[i], moe_w_group[i], moe_w_expert[i])
        last = i == depth - 1
        xf = _moe(h, x1, ids, gates, counts, moe_w1, moe_w3, moe_w2, i, final_norm, last)
    return xf.reshape(bsz, seq, d)
```

```python
import functools

import jax
import jax.numpy as jnp
from jax import lax
from jax.experimental import pallas as pl
from jax.experimental.pallas import tpu as pltpu

F32 = jnp.float32
BF16 = jnp.bfloat16
I32 = jnp.int32
HIGHEST = lax.Precision.HIGHEST

EPS = 1e-6
GLA_HEADS = 4
GLA_DK = 128
GLA_DV = 256
GLA_GATE_RANK = 16
GLA_GATE_TAU = 16.0
GLA_CHUNK = 64
MLA_HEADS = 8
MLA_NOPE = 128
MLA_ROPE = 64
MLA_V = 128
MLA_Q_LORA = 256
MLA_KV_LORA = 128
MLA_SCALE = (MLA_NOPE + MLA_ROPE) ** -0.5
Q_SCALE = MLA_SCALE * 1.4426950408889634
VT_ROWS = MLA_KV_LORA + 16
ROPE_THETA = 10000.0
MOE_GROUPS = 8
MOE_PER_GROUP = 8
MOE_EXPERTS = MOE_GROUPS * MOE_PER_GROUP
MOE_FF = 256

D_MODEL = 1024
LANES = 128
ROW_SUB = D_MODEL // LANES
VMEM_LIMIT = 48 * 1024 * 1024

TOK_TILE = 512
GLA_BLOCK = 256
ATT_BLOCK = 256
ROW_TILE = 256
EXP_BLOCK = 512


def _cparams(*sem):
    return pltpu.CompilerParams(dimension_semantics=sem, vmem_limit_bytes=VMEM_LIMIT)


def _rms(x, w):
    return x * lax.rsqrt(jnp.mean(x * x, axis=-1, keepdims=True) + EPS) * w


def _dot(a, b):
    return jnp.dot(a, b, preferred_element_type=F32)


def _dot_nt(a, b):
    return lax.dot_general(a, b, (((1,), (1,)), ((), ())), preferred_element_type=F32)


def _dot_tn(a, b):
    return lax.dot_general(a, b, (((0,), (0,)), ((), ())), preferred_element_type=F32)


def _split_bf16(x, pieces):
    out = []
    for _ in range(pieces - 1):
        hi = x.astype(BF16)
        out.append(hi)
        x = x - hi.astype(F32)
    out.append(x.astype(BF16))
    return out


def _gla_in_kernel(x_ref, nw_ref, wq_ref, wk_ref, wv_ref, wr_ref, wa_ref, wgh_ref, wgl_ref, bg_ref,
                   q_ref, k_ref, v_ref, r_ref, la_ref):
    hb = _rms(x_ref[...], nw_ref[...]).astype(BF16)
    a_hi, a_lo = _split_bf16(_dot(hb, wa_ref[...]), 2)
    z = _dot(a_hi, wgh_ref[...]) + _dot(a_hi, wgl_ref[...]) + _dot(a_lo, wgh_ref[...]) + bg_ref[...]
    log_sig = jnp.minimum(z, 0.0) - jnp.log1p(jnp.exp(-jnp.abs(z)))
    la_ref[...] = log_sig * (1.0 / GLA_GATE_TAU)
    q_ref[...] = _dot(hb, wq_ref[...]) * (GLA_DK ** -0.5)
    k_ref[...] = _dot(hb, wk_ref[...])
    v_ref[...] = _dot(hb, wv_ref[...]).astype(BF16)
    r_ref[...] = _dot(hb, wr_ref[...])


def _gla_in(x, nw, w_in, w_gate, b_gate):
    t, d = x.shape
    qk = GLA_HEADS * GLA_DK
    vd = GLA_HEADS * GLA_DV
    wq = w_in[:, :qk].astype(BF16)
    wk = w_in[:, qk:2 * qk].astype(BF16)
    wv = w_in[:, 2 * qk:2 * qk + vd].astype(BF16)
    wa = jnp.pad(w_in[:, 2 * qk + vd:2 * qk + vd + GLA_GATE_RANK], ((0, 0), (0, LANES - GLA_GATE_RANK))).astype(BF16)
    wr = w_in[:, 2 * qk + vd + GLA_GATE_RANK:].astype(BF16)
    wg = jnp.pad(w_gate, ((0, LANES - GLA_GATE_RANK), (0, 0)))
    wgh = wg.astype(BF16)
    wgl = (wg - wgh.astype(F32)).astype(BF16)
    tm = TOK_TILE
    row = lambda n: pl.BlockSpec((tm, n), lambda i: (i, 0))
    full = lambda a: pl.BlockSpec(a.shape, lambda i: (0, 0))
    nw2, bg2 = nw.reshape(1, d), b_gate.reshape(1, qk)
    return pl.pallas_call(
        _gla_in_kernel,
        grid=(t // tm,),
        in_specs=[row(d), full(nw2), full(wq), full(wk), full(wv), full(wr), full(wa), full(wgh), full(wgl),
                  full(bg2)],
        out_specs=[row(qk), row(qk), row(vd), row(vd), row(qk)],
        out_shape=[jax.ShapeDtypeStruct((t, qk), F32), jax.ShapeDtypeStruct((t, qk), F32),
                   jax.ShapeDtypeStruct((t, vd), BF16), jax.ShapeDtypeStruct((t, vd), F32),
                   jax.ShapeDtypeStruct((t, qk), F32)],
        compiler_params=_cparams("arbitrary"),
        name="gla_in",
    )(x, nw2, wq, wk, wv, wr, wa, wgh, wgl, bg2)


def _gla_rec_kernel(q_ref, k_ref, v_ref, r_ref, la_ref, onw_ref, o_ref, st_ref):
    c = GLA_CHUNK

    @pl.when(pl.program_id(1) == 0)
    def _():
        st_ref[...] = jnp.zeros_like(st_ref)

    rows = lax.broadcasted_iota(I32, (c, c), 0)
    cols = lax.broadcasted_iota(I32, (c, c), 1)
    causal = cols <= rows
    tril = causal.astype(F32)
    onw = onw_ref[...]
    for ci in range(GLA_BLOCK // c):
        sl = pl.ds(ci * c, c)
        g_all = jnp.dot(tril, la_ref[sl, :], precision=HIGHEST, preferred_element_type=F32)
        for h in range(GLA_HEADS):
            ks = pl.ds(h * GLA_DK, GLA_DK)
            vs = pl.ds(h * GLA_DV, GLA_DV)
            g = g_all[:, h * GLA_DK:(h + 1) * GLA_DK]
            g_mid = g[c // 2:c // 2 + 1]
            g_last = g[c - 1:c]
            q = q_ref[sl, ks]
            k = k_ref[sl, ks]
            v = v_ref[sl, vs]
            a = _dot_nt((q * jnp.exp(g - g_mid)).astype(BF16), (k * jnp.exp(g_mid - g)).astype(BF16))
            a = jnp.where(causal, a, 0.0).astype(BF16)
            st = st_ref[h]
            o = _dot(a, v) + _dot_nt((q * jnp.exp(g)).astype(BF16), st.astype(BF16))
            k_state = (k * jnp.exp(g_last - g)).astype(BF16)
            st_ref[h] = st * jnp.exp(g_last) + _dot_tn(v, k_state)
            o = _rms(o, onw)
            r = r_ref[sl, vs]
            o_ref[sl, vs] = (o * (r * jax.nn.sigmoid(r))).astype(BF16)


def _gla_rec(q, k, v, r, la, onw, bsz, seq):
    t = bsz * seq
    qk = GLA_HEADS * GLA_DK
    vd = GLA_HEADS * GLA_DV
    nb = seq // GLA_BLOCK
    row = lambda n: pl.BlockSpec((GLA_BLOCK, n), lambda b, j: (b * nb + j, 0))
    onw2 = onw.reshape(1, GLA_DV)
    return pl.pallas_call(
        _gla_rec_kernel,
        grid=(bsz, nb),
        in_specs=[row(qk), row(qk), row(vd), row(vd), row(qk), pl.BlockSpec((1, GLA_DV), lambda b, j: (0, 0))],
        out_specs=row(vd),
        out_shape=jax.ShapeDtypeStruct((t, vd), BF16),
        scratch_shapes=[pltpu.VMEM((GLA_HEADS, GLA_DV, GLA_DK), F32)],
        compiler_params=_cparams("arbitrary", "arbitrary"),
        name="gla_rec",
    )(q, k, v, r, la, onw2)


def _store_rows(ref, val):
    n, d = val.shape
    ref[...] = val.reshape(n * d // LANES, LANES)


def _load_rows(ref, start, n, nc, dtype):
    return jnp.concatenate([ref[pl.ds(start * nc + c, n, stride=nc), :].astype(dtype) for c in range(nc)], axis=1)


def _post_kernel(a_ref, wo_ref, x_ref, nw_ref, wrh_ref, wrl_ref,
                 x1_ref, h_ref, ids_ref, gates_ref, cnt_ref, carry_ref):
    tm = a_ref.shape[0]
    ne = MOE_EXPERTS

    @pl.when(pl.program_id(0) == 0)
    def _():
        carry_ref[...] = jnp.zeros_like(carry_ref)

    x1 = x_ref[...] + _dot(a_ref[...], wo_ref[...])
    x1_ref[...] = x1
    h = _rms(x1, nw_ref[...])
    _store_rows(h_ref, h)
    h_hi, h_lo = _split_bf16(h, 2)
    lt = _dot_nt(wrh_ref[...], h_hi) + _dot_nt(wrh_ref[...], h_lo) + _dot_nt(wrl_ref[...], h_hi)
    gl = lt[0:MOE_GROUPS]
    gmax = jnp.max(gl, axis=0, keepdims=True)
    gi = lax.broadcasted_iota(I32, gl.shape, 0)
    g_sel = jnp.min(jnp.where(gl == gmax, gi, MOE_GROUPS), axis=0, keepdims=True)
    g_w = 1.0 / jnp.sum(jnp.exp(gl - gmax), axis=0, keepdims=True)
    el = lt[MOE_GROUPS:MOE_GROUPS + ne]
    ei = lax.broadcasted_iota(I32, el.shape, 0)
    in_group = (ei // MOE_PER_GROUP) == g_sel
    neg = jnp.float32(-jnp.inf)
    el1 = jnp.where(in_group, el, neg)
    l1 = jnp.max(el1, axis=0, keepdims=True)
    i1 = jnp.min(jnp.where(el1 == l1, ei, ne), axis=0, keepdims=True)
    el2 = jnp.where(ei == i1, neg, el1)
    l2 = jnp.max(el2, axis=0, keepdims=True)
    i2 = jnp.min(jnp.where(el2 == l2, ei, ne), axis=0, keepdims=True)
    e2 = jnp.exp(l2 - l1)
    gate1 = g_w / (1.0 + e2)
    gate2 = g_w * e2 / (1.0 + e2)
    oh1 = ei == i1
    oh2 = ei == i2
    both = jnp.where(oh1 | oh2, 1.0, 0.0)
    su = lax.broadcasted_iota(I32, (tm, tm), 0)
    tu = lax.broadcasted_iota(I32, (tm, tm), 1)
    upper = jnp.where(su < tu, 1.0, 0.0).astype(BF16)
    prefix = _dot(both.astype(BF16), upper) + carry_ref[:, 0:1]
    rank1 = jnp.sum(jnp.where(oh1, prefix, 0.0), axis=0, keepdims=True)
    rank2 = jnp.sum(jnp.where(oh2, prefix, 0.0), axis=0, keepdims=True)
    carry = carry_ref[...] + jnp.sum(both, axis=1, keepdims=True)
    carry_ref[...] = carry
    cnt_ref[...] = carry.astype(I32)
    ids_ref[...] = jnp.concatenate([i1, i2, rank1.astype(I32), rank2.astype(I32)], axis=0)
    gates_ref[...] = jnp.concatenate([gate1, gate2], axis=0)


def _post(a, wo, x, nw, w_group, w_expert):
    t, d = x.shape
    tm = TOK_TILE
    wr = jnp.concatenate([w_group.T, w_expert.T], axis=0)
    wr = jnp.pad(wr, ((0, LANES - wr.shape[0]), (0, 0)))
    wrh = wr.astype(BF16)
    wrl = (wr - wrh.astype(F32)).astype(BF16)
    wo = wo.astype(BF16)
    nw2 = nw.reshape(1, d)
    row = lambda n: pl.BlockSpec((tm, n), lambda i: (i, 0))
    full = lambda arr: pl.BlockSpec(arr.shape, lambda i: (0, 0))
    return pl.pallas_call(
        _post_kernel,
        grid=(t // tm,),
        in_specs=[row(a.shape[1]), full(wo), row(d), full(nw2), full(wrh), full(wrl)],
        out_specs=[row(d), pl.BlockSpec((tm * d // LANES, LANES), lambda i: (i, 0)),
                   pl.BlockSpec((4, tm), lambda i: (0, i)), pl.BlockSpec((2, tm), lambda i: (0, i)),
                   pl.BlockSpec((MOE_EXPERTS, LANES), lambda i: (0, 0))],
        out_shape=[jax.ShapeDtypeStruct((t, d), F32), jax.ShapeDtypeStruct((t * d // LANES, LANES), F32),
                   jax.ShapeDtypeStruct((4, t), I32), jax.ShapeDtypeStruct((2, t), F32),
                   jax.ShapeDtypeStruct((MOE_EXPERTS, LANES), I32)],
        scratch_shapes=[pltpu.VMEM((MOE_EXPERTS, LANES), F32)],
        compiler_params=_cparams("arbitrary"),
        name="post_router",
    )(a, wo, x, nw2, wrh, wrl)


def _row(ref, i):
    return ref.at[pl.ds(pl.multiple_of(i * ROW_SUB, ROW_SUB), ROW_SUB)]


def _zero_fill(zinfo_ref, zero_ref, xs_hbm, sem, wait):
    ne = MOE_EXPERTS
    nblk = xs_hbm.shape[0] // (EXP_BLOCK * ROW_SUB)

    def copy(off_rows, n_rows):
        off = pl.multiple_of(off_rows * ROW_SUB, ROW_SUB)
        cp = pltpu.make_async_copy(zero_ref.at[pl.ds(0, n_rows * ROW_SUB)],
                                   xs_hbm.at[pl.ds(off, n_rows * ROW_SUB)], sem)
        cp.wait() if wait else cp.start()

    def expert(e, carry):
        pad = zinfo_ref[ne + e]
        for b in range(EXP_BLOCK.bit_length() - 1):
            @pl.when(((pad >> b) & 1) == 1)
            def _():
                copy(zinfo_ref[e] + (pad & ((1 << b) - 1)), 1 << b)
        return carry

    def tail(kk, carry):
        blk = zinfo_ref[2 * ne] + kk

        @pl.when(blk < nblk)
        def _():
            copy(blk * EXP_BLOCK, EXP_BLOCK)
        return carry

    lax.fori_loop(0, ne, expert, 0)
    lax.fori_loop(0, ne + 1, tail, 0)


RING = 3
ISSUE_UNROLL = 8


def _slot_copies(dest_hbm, i, idx_ref, slot, sem):
    return [pltpu.make_async_copy(dest_hbm.at[a, i], idx_ref.at[slot, a], sem.at[slot]) for a in range(2)]


def _dispatch_kernel(zinfo_ref, dest_hbm, h_hbm, xs_hbm, idx_ref, hbuf_ref, zero_ref, sem_idx, sem_h, sem_rows):
    i = pl.program_id(0)
    nt = pl.num_programs(0)
    tr = ROW_TILE
    rows = tr * ROW_SUB

    def tile_copies(j):
        slot = j % RING
        src = h_hbm.at[pl.ds(pl.multiple_of(j * rows, rows), rows)]
        return [pltpu.make_async_copy(src, hbuf_ref.at[slot], sem_h.at[slot])] + \
            _slot_copies(dest_hbm, j, idx_ref, slot, sem_idx)

    def scatter_rows(slot):
        def body(t, carry):
            src = _row(hbuf_ref.at[slot], t)
            pltpu.make_async_copy(src, _row(xs_hbm, idx_ref[slot, 0, t]), sem_rows.at[slot]).start()
            pltpu.make_async_copy(src, _row(xs_hbm, idx_ref[slot, 1, t]), sem_rows.at[slot]).start()
            return carry
        lax.fori_loop(0, tr, body, 0, unroll=ISSUE_UNROLL)

    def wait_rows(slot):
        def body(t, carry):
            pltpu.make_async_copy(_row(hbuf_ref.at[0], 0), _row(xs_hbm, 0), sem_rows.at[slot]).wait()
            return carry
        lax.fori_loop(0, 2 * tr, body, 0, unroll=ISSUE_UNROLL)

    @pl.when(i == 0)
    def _():
        for cp in tile_copies(0):
            cp.start()
        zero_ref[...] = jnp.zeros_like(zero_ref)
        _zero_fill(zinfo_ref, zero_ref, xs_hbm, sem_rows.at[1], wait=False)
        _zero_fill(zinfo_ref, zero_ref, xs_hbm, sem_rows.at[1], wait=True)

    @pl.when(i + 1 < nt)
    def _():
        for cp in tile_copies(i + 1):
            cp.start()

    for cp in tile_copies(i):
        cp.wait()

    for slot in range(RING):
        @pl.when(i % RING == slot)
        def _():
            scatter_rows(slot)

            @pl.when(i > 0)
            def _():
                wait_rows((slot - 1) % RING)

            @pl.when(i == nt - 1)
            def _():
                wait_rows(slot)


def _dispatch(h, dest, zoff, n_slots):
    t = h.shape[0] // ROW_SUB
    tr = ROW_TILE
    nt = t // tr
    grid_spec = pltpu.PrefetchScalarGridSpec(
        num_scalar_prefetch=1,
        grid=(nt,),
        in_specs=[pl.BlockSpec(memory_space=pl.ANY), pl.BlockSpec(memory_space=pl.ANY)],
        out_specs=pl.BlockSpec(memory_space=pl.ANY),
        scratch_shapes=[pltpu.SMEM((RING, 2, tr), I32), pltpu.VMEM((RING, tr * ROW_SUB, LANES), F32),
                        pltpu.VMEM((EXP_BLOCK * ROW_SUB, LANES), F32),
                        pltpu.SemaphoreType.DMA((RING,)), pltpu.SemaphoreType.DMA((RING,)),
                        pltpu.SemaphoreType.DMA((RING,))],
    )
    return pl.pallas_call(
        _dispatch_kernel,
        grid_spec=grid_spec,
        out_shape=jax.ShapeDtypeStruct((n_slots * ROW_SUB, LANES), F32),
        compiler_params=_cparams("arbitrary"),
        name="moe_dispatch",
    )(zoff, dest.reshape(2, nt, tr), h)


def _expert_kernel(be_ref, nused_ref, xs_ref, w1_ref, w3_ref, w2_ref, ys_ref, w1b_ref, w3b_ref, w2b_ref):
    i = pl.program_id(0)
    used = i < nused_ref[0]

    @pl.when(used & ((i == 0) | (be_ref[i] != be_ref[jnp.maximum(i - 1, 0)])))
    def _():
        w1b_ref[...] = w1_ref[0, 0].astype(BF16)
        w3b_ref[...] = w3_ref[0, 0].astype(BF16)
        w2b_ref[...] = w2_ref[0, 0].astype(BF16)

    @pl.when(used)
    def _():
        xb = _load_rows(xs_ref, 0, EXP_BLOCK, ROW_SUB, BF16)
        a = _dot(xb, w1b_ref[...])
        b = _dot(xb, w3b_ref[...])
        hid = (a * jax.nn.sigmoid(a) * b).astype(BF16)
        _store_rows(ys_ref, _dot(hid, w2b_ref[...]))

    @pl.when(i >= nused_ref[0])
    def _():
        ys_ref[...] = jnp.zeros_like(ys_ref)


def _experts(xs, block_e, n_used, w1, w3, w2, layer):
    bm = EXP_BLOCK
    n_slots = xs.shape[0] // ROW_SUB
    nblk = n_slots // bm
    d, ff = w1.shape[-2:]

    def last_used(i, nu):
        return jnp.maximum(jnp.minimum(i, nu[0] - 1), 0)

    def xmap(i, be, nu):
        return (last_used(i, nu), 0)

    def wmap(i, be, nu):
        return (layer, be[last_used(i, nu)], 0, 0)

    grid_spec = pltpu.PrefetchScalarGridSpec(
        num_scalar_prefetch=2,
        grid=(nblk,),
        in_specs=[pl.BlockSpec((bm * ROW_SUB, LANES), xmap), pl.BlockSpec((1, 1, d, ff), wmap),
                  pl.BlockSpec((1, 1, d, ff), wmap), pl.BlockSpec((1, 1, ff, d), wmap)],
        out_specs=pl.BlockSpec((bm * ROW_SUB, LANES), lambda i, be, nu: (i, 0)),
        scratch_shapes=[pltpu.VMEM((d, ff), BF16), pltpu.VMEM((d, ff), BF16), pltpu.VMEM((ff, d), BF16)],
    )
    return pl.pallas_call(
        _expert_kernel,
        grid_spec=grid_spec,
        out_shape=jax.ShapeDtypeStruct((n_slots * ROW_SUB, LANES), F32),
        compiler_params=_cparams("arbitrary"),
        name="moe_experts",
    )(block_e, n_used, xs, w1, w3, w2)


def _combine_kernel(dest_hbm, ys_hbm, x_ref, g_ref, nw_ref, o_ref, idx_ref, buf_ref, sem_idx, sem_rows,
                    *, final_norm):
    i = pl.program_id(0)
    nt = pl.num_programs(0)
    tr = x_ref.shape[0]

    def gather(slot):
        buf = buf_ref.at[slot]

        def body(t, carry):
            pltpu.make_async_copy(_row(ys_hbm, idx_ref[slot, 0, t]), _row(buf, t), sem_rows.at[slot]).start()
            pltpu.make_async_copy(_row(ys_hbm, idx_ref[slot, 1, t]), _row(buf, tr + t), sem_rows.at[slot]).start()
            return carry

        lax.fori_loop(0, tr, body, 0, unroll=ISSUE_UNROLL)

    def wait_rows(slot):
        def body(t, carry):
            pltpu.make_async_copy(_row(ys_hbm, 0), _row(buf_ref.at[0], 0), sem_rows.at[slot]).wait()
            return carry
        lax.fori_loop(0, 2 * tr, body, 0, unroll=ISSUE_UNROLL)

    @pl.when(i == 0)
    def _():
        for cp in _slot_copies(dest_hbm, 0, idx_ref, 0, sem_idx):
            cp.start()
        for cp in _slot_copies(dest_hbm, 0, idx_ref, 0, sem_idx):
            cp.wait()
        gather(0)

        @pl.when(nt > 1)
        def _():
            for cp in _slot_copies(dest_hbm, 1, idx_ref, 1, sem_idx):
                cp.start()

    for slot in range(2):
        @pl.when(i % 2 == slot)
        def _():
            nxt = 1 - slot

            @pl.when(i + 1 < nt)
            def _():
                for cp in _slot_copies(dest_hbm, i + 1, idx_ref, nxt, sem_idx):
                    cp.wait()
                gather(nxt)

            @pl.when(i + 2 < nt)
            def _():
                for cp in _slot_copies(dest_hbm, i + 2, idx_ref, slot, sem_idx):
                    cp.start()

            wait_rows(slot)
            g = g_ref[...]
            y1 = _load_rows(buf_ref.at[slot], 0, tr, ROW_SUB, F32)
            y2 = _load_rows(buf_ref.at[slot], tr, tr, ROW_SUB, F32)
            out = x_ref[...] + g[:, 0:1] * y1 + g[:, 1:2] * y2
            if final_norm:
                out = _rms(out, nw_ref[...])
            o_ref[...] = out


def _combine(ys, dest, x, gates, nw, final_norm):
    t, d = x.shape
    tr = ROW_TILE
    nt = t // tr
    g_col = gates.T
    nw2 = nw.reshape(1, d)
    return pl.pallas_call(
        functools.partial(_combine_kernel, final_norm=final_norm),
        grid=(nt,),
        in_specs=[pl.BlockSpec(memory_space=pl.ANY), pl.BlockSpec(memory_space=pl.ANY),
                  pl.BlockSpec((tr, d), lambda i: (i, 0)), pl.BlockSpec((tr, 2), lambda i: (i, 0)),
                  pl.BlockSpec((1, d), lambda i: (0, 0))],
        out_specs=pl.BlockSpec((tr, d), lambda i: (i, 0)),
        out_shape=jax.ShapeDtypeStruct((t, d), F32),
        scratch_shapes=[pltpu.SMEM((2, 2, tr), I32), pltpu.VMEM((2, 2 * tr * ROW_SUB, LANES), F32),
                        pltpu.SemaphoreType.DMA((2,)), pltpu.SemaphoreType.DMA((2,))],
        compiler_params=_cparams("arbitrary"),
        name="moe_combine",
    )(dest.reshape(2, nt, tr), ys, x, g_col, nw2)


def _slots_kernel(ids_ref, pst_ref, dest_ref):
    ids = ids_ref[...]
    ei = lax.broadcasted_iota(I32, (MOE_EXPERTS, ids.shape[1]), 0)
    pst = pst_ref[...]
    d1 = jnp.sum(jnp.where(ei == ids[0:1], pst, 0), axis=0, keepdims=True) + ids[2:3]
    d2 = jnp.sum(jnp.where(ei == ids[1:2], pst, 0), axis=0, keepdims=True) + ids[3:4]
    dest_ref[...] = jnp.concatenate([d1, d2], axis=0)


def _slots(ids, pstarts):
    t = ids.shape[1]
    tm = TOK_TILE
    return pl.pallas_call(
        _slots_kernel,
        grid=(t // tm,),
        in_specs=[pl.BlockSpec((4, tm), lambda i: (0, i)), pl.BlockSpec((MOE_EXPERTS, 1), lambda i: (0, 0))],
        out_specs=pl.BlockSpec((2, tm), lambda i: (0, i)),
        out_shape=jax.ShapeDtypeStruct((2, t), I32),
        compiler_params=_cparams("arbitrary"),
        name="moe_slots",
    )(ids, pstarts.reshape(MOE_EXPERTS, 1))


def _moe(h, x1, ids, gates, counts, w1, w3, w2, layer, nw_final, final_norm):
    t = x1.shape[0]
    bm = EXP_BLOCK
    n_asg = 2 * t
    n_slots = n_asg + MOE_EXPERTS * bm
    cnt = counts[:, 0]
    padded = (cnt + bm - 1) // bm * bm
    pend = jnp.cumsum(padded)
    pstarts = (pend - padded).astype(I32)
    dest = _slots(ids, pstarts)
    nblk = n_slots // bm
    block_start = jnp.arange(nblk, dtype=I32) * bm
    block_e = jnp.minimum(jnp.sum(pend[None, :] <= block_start[:, None], axis=1), MOE_EXPERTS - 1).astype(I32)
    n_used = (pend[-1:] // bm).astype(I32)
    zinfo = jnp.concatenate([pstarts + cnt, padded - cnt, n_used]).astype(I32)
    xs = _dispatch(h, dest, zinfo, n_slots)
    ys = _experts(xs, block_e, n_used, w1, w3, w2, layer)
    return _combine(ys, dest, x1, gates, nw_final, final_norm)


def _mla_in_kernel(x_ref, pos_ref, nw_ref, wq_ref, wkv_ref, wks_ref, qn_ref, kvn_ref, wqn_ref, wqr_ref, wqs_ref,
                   wuk_ref, freq_ref, qt_ref, k_ref, vt_ref):
    tm = x_ref.shape[0]
    tb = ATT_BLOCK
    hb = _rms(x_ref[...], nw_ref[...]).astype(BF16)
    c_q = _rms(_dot(hb, wq_ref[...]), qn_ref[...]).astype(BF16)
    kv_t = _dot_nt(wkv_ref[...], hb)
    ks_t = _dot_nt(wks_ref[...], hb)
    c_kv = kv_t[:MLA_KV_LORA]
    c_kv = c_kv * lax.rsqrt(jnp.mean(c_kv * c_kv, axis=0, keepdims=True) + EPS) * kvn_ref[...]
    ang = freq_ref[...] * pos_ref[...].astype(F32)
    cos32, sin32 = jnp.cos(ang), jnp.sin(ang)
    cos = jnp.concatenate([cos32] * 4, axis=0)
    sin = jnp.concatenate([-sin32, sin32] * 2, axis=0)
    k_rope = kv_t[MLA_KV_LORA:] * cos + ks_t * sin
    k_ref[...] = jnp.concatenate([c_kv, k_rope], axis=0).T.astype(BF16)
    v_ext = jnp.concatenate([c_kv, jnp.ones((VT_ROWS - MLA_KV_LORA, tm), F32)], axis=0).astype(BF16)
    for c in range(tm // tb):
        vt_ref[c] = v_ext[:, c * tb:(c + 1) * tb]
    q_nope = _dot_nt(wqn_ref[...], c_q).astype(BF16)
    q_r = _dot_nt(wqr_ref[...], c_q)
    q_s = _dot_nt(wqs_ref[...], c_q)
    for h in range(MLA_HEADS):
        sl = slice(h * LANES, (h + 1) * LANES)
        q_lat = (_dot(wuk_ref[h], q_nope[sl]) * Q_SCALE).astype(BF16)
        q_rope = ((q_r[sl] * cos + q_s[sl] * sin) * Q_SCALE).astype(BF16)
        for c in range(tm // tb):
            cs = slice(c * tb, (c + 1) * tb)
            qt_ref[c, 2 * h * LANES:(2 * h + 1) * LANES, :] = q_lat[:, cs]
            qt_ref[c, (2 * h + 1) * LANES:(2 * h + 2) * LANES, :] = q_rope[:, cs]


def _swap_halves(w):
    half = w.shape[-1] // 2
    return jnp.concatenate([w[..., half:], w[..., :half]], axis=-1)


def _mla_in(x, pos, nw, w_in, q_norm, w_uq, kv_norm, w_uk):
    t, d = x.shape
    tm = TOK_TILE
    tb = ATT_BLOCK
    nh = MLA_HEADS
    pad_r = LANES - MLA_ROPE
    w_kr = w_in[:, MLA_Q_LORA + MLA_KV_LORA:]
    wq = w_in[:, :MLA_Q_LORA].astype(BF16)
    wkv = jnp.pad(w_in[:, MLA_Q_LORA:], ((0, 0), (0, pad_r))).T.astype(BF16)
    wks = jnp.pad(_swap_halves(w_kr), ((0, 0), (0, pad_r))).T.astype(BF16)
    wqn = w_uq[:, :, :MLA_NOPE].reshape(MLA_Q_LORA, nh * MLA_NOPE).T.astype(BF16)
    w_r = w_uq[:, :, MLA_NOPE:]
    wqr = jnp.pad(w_r, ((0, 0), (0, 0), (0, pad_r))).reshape(MLA_Q_LORA, nh * LANES).T.astype(BF16)
    wqs = jnp.pad(_swap_halves(w_r), ((0, 0), (0, 0), (0, pad_r))).reshape(MLA_Q_LORA, nh * LANES).T.astype(BF16)
    wuk = w_uk.transpose(1, 0, 2).astype(BF16)
    inv_freq = 1.0 / (ROPE_THETA ** (jnp.arange(0, MLA_ROPE, 2, dtype=F32) / MLA_ROPE))
    full = lambda a: pl.BlockSpec(a.shape, lambda i: (0,) * a.ndim)
    args = (x, pos.reshape(1, t), nw.reshape(1, d), wq, wkv, wks, q_norm.reshape(1, -1), kv_norm.reshape(-1, 1),
            wqn, wqr, wqs, wuk, inv_freq.reshape(-1, 1))
    nsub = tm // tb
    return pl.pallas_call(
        _mla_in_kernel,
        grid=(t // tm,),
        in_specs=[pl.BlockSpec((tm, d), lambda i: (i, 0)), pl.BlockSpec((1, tm), lambda i: (0, i))]
        + [full(a) for a in args[2:]],
        out_specs=[pl.BlockSpec((nsub, 2 * nh * LANES, tb), lambda i: (i, 0, 0)),
                   pl.BlockSpec((tm, 2 * LANES), lambda i: (i, 0)),
                   pl.BlockSpec((nsub, VT_ROWS, tb), lambda i: (i, 0, 0))],
        out_shape=[jax.ShapeDtypeStruct((t // tb, 2 * nh * LANES, tb), BF16),
                   jax.ShapeDtypeStruct((t, 2 * LANES), BF16),
                   jax.ShapeDtypeStruct((t // tb, VT_ROWS, tb), BF16)],
        compiler_params=_cparams("arbitrary"),
        name="mla_in",
    )(*args)


def _mla_attn_kernel(qt_ref, k_ref, vt_ref, wuv_ref, o_ref, *state):
    i = pl.program_id(1)
    tb = ATT_BLOCK
    nh = MLA_HEADS
    dq = 2 * LANES
    s_refs, smax_refs, m_refs, acc_refs = state[0::4], state[1::4], state[2::4], state[3::4]

    def scores(j, h):
        kb = k_ref[pl.ds(pl.multiple_of(j * tb, tb), tb), :]
        s = _dot(kb, qt_ref[0, h * dq:(h + 1) * dq, :])
        return s, jnp.max(s, axis=0, keepdims=True)

    def consume(j, h, s, s_max):
        m_old = m_refs[h][...]
        m_new = jnp.maximum(m_old, s_max)
        alpha = jnp.exp2(m_old - m_new)
        p = jnp.exp2(s - m_new).astype(BF16)
        acc_refs[h][...] = alpha * acc_refs[h][...] + _dot(vt_ref[j], p)
        m_refs[h][...] = m_new

    for h in range(nh):
        m_refs[h][...] = jnp.full_like(m_refs[h], -jnp.inf)
        acc_refs[h][...] = jnp.zeros_like(acc_refs[h])
        s_refs[h][...], smax_refs[h][...] = scores(0, h)

    def full_block(j, carry):
        for h in range(nh):
            s, s_max = s_refs[h][...], smax_refs[h][...]
            s_new, s_max_new = scores(j + 1, h)
            consume(j, h, s, s_max)
            s_refs[h][...], smax_refs[h][...] = s_new, s_max_new
        return carry

    lax.fori_loop(0, i, full_block, 0)
    kpos = lax.broadcasted_iota(I32, (tb, tb), 0)
    qpos = lax.broadcasted_iota(I32, (tb, tb), 1)
    for h in range(nh):
        s = jnp.where(kpos <= qpos, s_refs[h][...], -jnp.inf)
        consume(i, h, s, jnp.max(s, axis=0, keepdims=True))
    for h in range(nh):
        acc = acc_refs[h][...]
        o_lat = (acc[:MLA_KV_LORA] / acc[MLA_KV_LORA:MLA_KV_LORA + 1]).astype(BF16)
        o_ref[:, h * MLA_V:(h + 1) * MLA_V] = _dot_tn(o_lat, wuv_ref[h]).astype(BF16)


def _mla_attn(qt, k, vt, w_uv, bsz, seq):
    t = bsz * seq
    tb = ATT_BLOCK
    nq = seq // tb
    nh = MLA_HEADS
    wuv = w_uv.transpose(1, 0, 2).astype(BF16)
    return pl.pallas_call(
        _mla_attn_kernel,
        grid=(bsz, nq),
        in_specs=[pl.BlockSpec((1, 2 * nh * LANES, tb), lambda b, i: (b * nq + i, 0, 0)),
                  pl.BlockSpec((seq, 2 * LANES), lambda b, i: (b, 0)),
                  pl.BlockSpec((nq, VT_ROWS, tb), lambda b, i: (b, 0, 0)),
                  pl.BlockSpec(wuv.shape, lambda b, i: (0, 0, 0))],
        out_specs=pl.BlockSpec((tb, nh * MLA_V), lambda b, i: (b * nq + i, 0)),
        out_shape=jax.ShapeDtypeStruct((t, nh * MLA_V), BF16),
        scratch_shapes=[pltpu.VMEM((tb, tb), F32), pltpu.VMEM((1, tb), F32), pltpu.VMEM((1, tb), F32),
                        pltpu.VMEM((VT_ROWS, tb), F32)] * nh,
        compiler_params=_cparams("arbitrary", "arbitrary"),
        name="mla_attn",
    )(qt, k, vt, wuv)


def kernel(x, positions, attn_norm, ffn_norm, final_norm, gla_w_in, gla_w_gate, gla_b_gate, gla_out_norm, gla_w_o,
           mla_w_in, mla_q_norm, mla_w_uq, mla_kv_norm, mla_w_uk, mla_w_uv, mla_w_o,
           moe_w_group, moe_w_expert, moe_w1, moe_w3, moe_w2):
    bsz, seq, d = x.shape
    t = bsz * seq
    depth = attn_norm.shape[0]
    xf = x.reshape(t, d)
    pos = positions.reshape(t)
    for i in range(depth):
        j = i // 2
        if i % 2 == 0:
            q, k, v, r, la = _gla_in(xf, attn_norm[i], gla_w_in[j], gla_w_gate[j], gla_b_gate[j])
            mixed = _gla_rec(q, k, v, r, la, gla_out_norm[j], bsz, seq)
            w_o = gla_w_o[j]
        else:
            qt, kc, vt = _mla_in(xf, pos, attn_norm[i], mla_w_in[j], mla_q_norm[j], mla_w_uq[j], mla_kv_norm[j],
                                 mla_w_uk[j])
            mixed = _mla_attn(qt, kc, vt, mla_w_uv[j], bsz, seq)
            w_o = mla_w_o[j]
        x1, h, ids, gates, counts = _post(mixed, w_o, xf, ffn_norm[i], moe_w_group[i], moe_w_expert[i])
        last = i == depth - 1
        xf = _moe(h, x1, ids, gates, counts, moe_w1, moe_w3, moe_w2, i, final_norm, last)
    return xf.reshape(bsz, seq, d)
```

```python
import functools

import jax
import jax.numpy as jnp
from jax import lax
from jax.experimental import pallas as pl
from jax.experimental.pallas import tpu as pltpu

F32 = jnp.float32
BF16 = jnp.bfloat16
I32 = jnp.int32
HIGHEST = lax.Precision.HIGHEST

EPS = 1e-6
GLA_HEADS = 4
GLA_DK = 128
GLA_DV = 256
GLA_GATE_RANK = 16
GLA_GATE_TAU = 16.0
GLA_CHUNK = 64
MLA_HEADS = 8
MLA_NOPE = 128
MLA_ROPE = 64
MLA_V = 128
MLA_Q_LORA = 256
MLA_KV_LORA = 128
MLA_SCALE = (MLA_NOPE + MLA_ROPE) ** -0.5
Q_SCALE = MLA_SCALE * 1.4426950408889634
VT_ROWS = MLA_KV_LORA + 16
ROPE_THETA = 10000.0
MOE_GROUPS = 8
MOE_PER_GROUP = 8
MOE_EXPERTS = MOE_GROUPS * MOE_PER_GROUP
MOE_FF = 256

D_MODEL = 1024
LANES = 128
ROW_SUB = D_MODEL // LANES
VMEM_LIMIT = 48 * 1024 * 1024

TOK_TILE = 512
GLA_BLOCK = 256
ATT_BLOCK = 256
ROW_TILE = 256
EXP_BLOCK = 512


def _cparams(*sem):
    return pltpu.CompilerParams(dimension_semantics=sem, vmem_limit_bytes=VMEM_LIMIT)


def _rms(x, w):
    return x * lax.rsqrt(jnp.mean(x * x, axis=-1, keepdims=True) + EPS) * w


def _dot(a, b):
    return jnp.dot(a, b, preferred_element_type=F32)


def _dot_nt(a, b):
    return lax.dot_general(a, b, (((1,), (1,)), ((), ())), preferred_element_type=F32)


def _dot_tn(a, b):
    return lax.dot_general(a, b, (((0,), (0,)), ((), ())), preferred_element_type=F32)


def _split_bf16(x, pieces):
    out = []
    for _ in range(pieces - 1):
        hi = x.astype(BF16)
        out.append(hi)
        x = x - hi.astype(F32)
    out.append(x.astype(BF16))
    return out


def _gla_in_kernel(x_ref, nw_ref, wq_ref, wk_ref, wv_ref, wr_ref, wa_ref, wgh_ref, wgl_ref, bg_ref,
                   q_ref, k_ref, v_ref, r_ref, la_ref):
    hb = _rms(x_ref[...], nw_ref[...]).astype(BF16)
    a_hi, a_lo = _split_bf16(_dot(hb, wa_ref[...]), 2)
    z = _dot(a_hi, wgh_ref[...]) + _dot(a_hi, wgl_ref[...]) + _dot(a_lo, wgh_ref[...]) + bg_ref[...]
    log_sig = jnp.minimum(z, 0.0) - jnp.log1p(jnp.exp(-jnp.abs(z)))
    la_ref[...] = log_sig * (1.0 / GLA_GATE_TAU)
    q_ref[...] = _dot(hb, wq_ref[...]) * (GLA_DK ** -0.5)
    k_ref[...] = _dot(hb, wk_ref[...])
    v_ref[...] = _dot(hb, wv_ref[...]).astype(BF16)
    r_ref[...] = _dot(hb, wr_ref[...])


def _gla_in(x, nw, w_in, w_gate, b_gate):
    t, d = x.shape
    qk = GLA_HEADS * GLA_DK
    vd = GLA_HEADS * GLA_DV
    wq = w_in[:, :qk].astype(BF16)
    wk = w_in[:, qk:2 * qk].astype(BF16)
    wv = w_in[:, 2 * qk:2 * qk + vd].astype(BF16)
    wa = jnp.pad(w_in[:, 2 * qk + vd:2 * qk + vd + GLA_GATE_RANK], ((0, 0), (0, LANES - GLA_GATE_RANK))).astype(BF16)
    wr = w_in[:, 2 * qk + vd + GLA_GATE_RANK:].astype(BF16)
    wg = jnp.pad(w_gate, ((0, LANES - GLA_GATE_RANK), (0, 0)))
    wgh = wg.astype(BF16)
    wgl = (wg - wgh.astype(F32)).astype(BF16)
    tm = TOK_TILE
    row = lambda n: pl.BlockSpec((tm, n), lambda i: (i, 0))
    full = lambda a: pl.BlockSpec(a.shape, lambda i: (0, 0))
    nw2, bg2 = nw.reshape(1, d), b_gate.reshape(1, qk)
    return pl.pallas_call(
        _gla_in_kernel,
        grid=(t // tm,),
        in_specs=[row(d), full(nw2), full(wq), full(wk), full(wv), full(wr), full(wa), full(wgh), full(wgl),
                  full(bg2)],
        out_specs=[row(qk), row(qk), row(vd), row(vd), row(qk)],
        out_shape=[jax.ShapeDtypeStruct((t, qk), F32), jax.ShapeDtypeStruct((t, qk), F32),
                   jax.ShapeDtypeStruct((t, vd), BF16), jax.ShapeDtypeStruct((t, vd), F32),
                   jax.ShapeDtypeStruct((t, qk), F32)],
        compiler_params=_cparams("arbitrary"),
        name="gla_in",
    )(x, nw2, wq, wk, wv, wr, wa, wgh, wgl, bg2)


def _gla_rec_kernel(q_ref, k_ref, v_ref, r_ref, la_ref, onw_ref, o_ref, st_ref):
    c = GLA_CHUNK
    blk = GLA_BLOCK
    nc = blk // c

    @pl.when(pl.program_id(1) == 0)
    def _():
        st_ref[...] = jnp.zeros_like(st_ref)

    rows = lax.broadcasted_iota(I32, (blk, blk), 0)
    cols = lax.broadcasted_iota(I32, (blk, blk), 1)
    causal = (cols <= rows) & (cols // c == rows // c)
    tri = jnp.where(causal, 1.0, 0.0).astype(BF16)
    g = sum(_dot(tri, piece) for piece in _split_bf16(la_ref[...], 3))
    g_mid = jnp.concatenate(
        [jnp.broadcast_to(g[ci * c + c // 2:ci * c + c // 2 + 1], (c, g.shape[1])) for ci in range(nc)], axis=0)
    g_last = jnp.concatenate(
        [jnp.broadcast_to(g[ci * c + c - 1:ci * c + c], (c, g.shape[1])) for ci in range(nc)], axis=0)
    q = q_ref[...]
    k = k_ref[...]
    q_intra = (q * jnp.exp(g - g_mid)).astype(BF16)
    k_intra = (k * jnp.exp(g_mid - g)).astype(BF16)
    q_inter = (q * jnp.exp(g)).astype(BF16)
    k_state = (k * jnp.exp(g_last - g)).astype(BF16)
    onw = onw_ref[...]
    heads = range(GLA_HEADS)
    ksl = [slice(h * GLA_DK, (h + 1) * GLA_DK) for h in heads]
    vsl = [slice(h * GLA_DV, (h + 1) * GLA_DV) for h in heads]
    v = [v_ref[:, vsl[h]] for h in heads]
    a = [jnp.where(causal, _dot_nt(q_intra[:, ksl[h]], k_intra[:, ksl[h]]), 0.0).astype(BF16) for h in heads]
    kv = [[_dot_tn(v[h][ci * c:(ci + 1) * c], k_state[ci * c:(ci + 1) * c, ksl[h]]) for ci in range(nc)]
          for h in heads]
    o_intra = [_dot(a[h], v[h]) for h in heads]
    for h in heads:
        st = st_ref[h]
        o_inter = []
        for ci in range(nc):
            o_inter.append(_dot_nt(q_inter[ci * c:(ci + 1) * c, ksl[h]], st.astype(BF16)))
            st = st * jnp.exp(g[ci * c + c - 1:ci * c + c, ksl[h]]) + kv[h][ci]
        st_ref[h] = st
        o = _rms(o_intra[h] + jnp.concatenate(o_inter, axis=0), onw)
        r = r_ref[:, vsl[h]]
        o_ref[:, vsl[h]] = (o * (r * jax.nn.sigmoid(r))).astype(BF16)


def _gla_rec(q, k, v, r, la, onw, bsz, seq):
    t = bsz * seq
    qk = GLA_HEADS * GLA_DK
    vd = GLA_HEADS * GLA_DV
    nb = seq // GLA_BLOCK
    row = lambda n: pl.BlockSpec((GLA_BLOCK, n), lambda b, j: (b * nb + j, 0))
    onw2 = onw.reshape(1, GLA_DV)
    return pl.pallas_call(
        _gla_rec_kernel,
        grid=(bsz, nb),
        in_specs=[row(qk), row(qk), row(vd), row(vd), row(qk), pl.BlockSpec((1, GLA_DV), lambda b, j: (0, 0))],
        out_specs=row(vd),
        out_shape=jax.ShapeDtypeStruct((t, vd), BF16),
        scratch_shapes=[pltpu.VMEM((GLA_HEADS, GLA_DV, GLA_DK), F32)],
        compiler_params=_cparams("arbitrary", "arbitrary"),
        name="gla_rec",
    )(q, k, v, r, la, onw2)


def _store_rows(ref, val):
    n, d = val.shape
    ref[...] = val.reshape(n * d // LANES, LANES)


def _load_rows(ref, start, n, nc, dtype):
    return jnp.concatenate([ref[pl.ds(start * nc + c, n, stride=nc), :].astype(dtype) for c in range(nc)], axis=1)


def _post_kernel(a_ref, wo_ref, x_ref, nw_ref, wrh_ref, wrl_ref,
                 x1_ref, h_ref, ids_ref, gates_ref, cnt_ref, carry_ref):
    tm = a_ref.shape[0]
    ne = MOE_EXPERTS

    @pl.when(pl.program_id(0) == 0)
    def _():
        carry_ref[...] = jnp.zeros_like(carry_ref)

    x1 = x_ref[...] + _dot(a_ref[...], wo_ref[...])
    x1_ref[...] = x1
    h = _rms(x1, nw_ref[...])
    _store_rows(h_ref, h)
    h_hi, h_lo = _split_bf16(h, 2)
    lt = _dot_nt(wrh_ref[...], h_hi) + _dot_nt(wrh_ref[...], h_lo) + _dot_nt(wrl_ref[...], h_hi)
    gl = lt[0:MOE_GROUPS]
    gmax = jnp.max(gl, axis=0, keepdims=True)
    gi = lax.broadcasted_iota(I32, gl.shape, 0)
    g_sel = jnp.min(jnp.where(gl == gmax, gi, MOE_GROUPS), axis=0, keepdims=True)
    g_w = 1.0 / jnp.sum(jnp.exp(gl - gmax), axis=0, keepdims=True)
    el = lt[MOE_GROUPS:MOE_GROUPS + ne]
    ei = lax.broadcasted_iota(I32, el.shape, 0)
    in_group = (ei // MOE_PER_GROUP) == g_sel
    neg = jnp.float32(-jnp.inf)
    el1 = jnp.where(in_group, el, neg)
    l1 = jnp.max(el1, axis=0, keepdims=True)
    i1 = jnp.min(jnp.where(el1 == l1, ei, ne), axis=0, keepdims=True)
    el2 = jnp.where(ei == i1, neg, el1)
    l2 = jnp.max(el2, axis=0, keepdims=True)
    i2 = jnp.min(jnp.where(el2 == l2, ei, ne), axis=0, keepdims=True)
    e2 = jnp.exp(l2 - l1)
    gate1 = g_w / (1.0 + e2)
    gate2 = g_w * e2 / (1.0 + e2)
    oh1 = ei == i1
    oh2 = ei == i2
    both = jnp.where(oh1 | oh2, 1.0, 0.0)
    su = lax.broadcasted_iota(I32, (tm, tm), 0)
    tu = lax.broadcasted_iota(I32, (tm, tm), 1)
    upper = jnp.where(su < tu, 1.0, 0.0).astype(BF16)
    prefix = _dot(both.astype(BF16), upper) + carry_ref[:, 0:1]
    rank1 = jnp.sum(jnp.where(oh1, prefix, 0.0), axis=0, keepdims=True)
    rank2 = jnp.sum(jnp.where(oh2, prefix, 0.0), axis=0, keepdims=True)
    carry = carry_ref[...] + jnp.sum(both, axis=1, keepdims=True)
    carry_ref[...] = carry
    cnt_ref[...] = carry.astype(I32)
    ids_ref[...] = jnp.concatenate([i1, i2, rank1.astype(I32), rank2.astype(I32)], axis=0)
    gates_ref[...] = jnp.concatenate([gate1, gate2], axis=0)


def _post(a, wo, x, nw, w_group, w_expert):
    t, d = x.shape
    tm = TOK_TILE
    wr = jnp.concatenate([w_group.T, w_expert.T], axis=0)
    wr = jnp.pad(wr, ((0, LANES - wr.shape[0]), (0, 0)))
    wrh = wr.astype(BF16)
    wrl = (wr - wrh.astype(F32)).astype(BF16)
    wo = wo.astype(BF16)
    nw2 = nw.reshape(1, d)
    row = lambda n: pl.BlockSpec((tm, n), lambda i: (i, 0))
    full = lambda arr: pl.BlockSpec(arr.shape, lambda i: (0, 0))
    return pl.pallas_call(
        _post_kernel,
        grid=(t // tm,),
        in_specs=[row(a.shape[1]), full(wo), row(d), full(nw2), full(wrh), full(wrl)],
        out_specs=[row(d), pl.BlockSpec((tm * d // LANES, LANES), lambda i: (i, 0)),
                   pl.BlockSpec((4, tm), lambda i: (0, i)), pl.BlockSpec((2, tm), lambda i: (0, i)),
                   pl.BlockSpec((MOE_EXPERTS, LANES), lambda i: (0, 0))],
        out_shape=[jax.ShapeDtypeStruct((t, d), F32), jax.ShapeDtypeStruct((t * d // LANES, LANES), F32),
                   jax.ShapeDtypeStruct((4, t), I32), jax.ShapeDtypeStruct((2, t), F32),
                   jax.ShapeDtypeStruct((MOE_EXPERTS, LANES), I32)],
        scratch_shapes=[pltpu.VMEM((MOE_EXPERTS, LANES), F32)],
        compiler_params=_cparams("arbitrary"),
        name="post_router",
    )(a, wo, x, nw2, wrh, wrl)


def _row(ref, i):
    return ref.at[pl.ds(pl.multiple_of(i * ROW_SUB, ROW_SUB), ROW_SUB)]


def _zero_fill(zinfo_ref, zero_ref, xs_hbm, sem, wait):
    ne = MOE_EXPERTS
    nblk = xs_hbm.shape[0] // (EXP_BLOCK * ROW_SUB)

    def copy(off_rows, n_rows):
        off = pl.multiple_of(off_rows * ROW_SUB, ROW_SUB)
        cp = pltpu.make_async_copy(zero_ref.at[pl.ds(0, n_rows * ROW_SUB)],
                                   xs_hbm.at[pl.ds(off, n_rows * ROW_SUB)], sem)
        cp.wait() if wait else cp.start()

    def expert(e, carry):
        pad = zinfo_ref[ne + e]
        for b in range(EXP_BLOCK.bit_length() - 1):
            @pl.when(((pad >> b) & 1) == 1)
            def _():
                copy(zinfo_ref[e] + (pad & ((1 << b) - 1)), 1 << b)
        return carry

    def tail(kk, carry):
        blk = zinfo_ref[2 * ne] + kk

        @pl.when(blk < nblk)
        def _():
            copy(blk * EXP_BLOCK, EXP_BLOCK)
        return carry

    lax.fori_loop(0, ne, expert, 0)
    lax.fori_loop(0, ne + 1, tail, 0)


RING = 3
ISSUE_UNROLL = 8


def _slot_copies(dest_hbm, i, idx_ref, slot, sem):
    return [pltpu.make_async_copy(dest_hbm.at[a, i], idx_ref.at[slot, a], sem.at[slot]) for a in range(2)]


def _dispatch_kernel(zinfo_ref, dest_hbm, h_hbm, xs_hbm, idx_ref, hbuf_ref, zero_ref, sem_idx, sem_h, sem_rows):
    i = pl.program_id(0)
    nt = pl.num_programs(0)
    tr = ROW_TILE
    rows = tr * ROW_SUB

    def tile_copies(j):
        slot = j % RING
        src = h_hbm.at[pl.ds(pl.multiple_of(j * rows, rows), rows)]
        return [pltpu.make_async_copy(src, hbuf_ref.at[slot], sem_h.at[slot])] + \
            _slot_copies(dest_hbm, j, idx_ref, slot, sem_idx)

    def scatter_rows(slot):
        def body(t, carry):
            src = _row(hbuf_ref.at[slot], t)
            pltpu.make_async_copy(src, _row(xs_hbm, idx_ref[slot, 0, t]), sem_rows.at[slot]).start()
            pltpu.make_async_copy(src, _row(xs_hbm, idx_ref[slot, 1, t]), sem_rows.at[slot]).start()
            return carry
        lax.fori_loop(0, tr, body, 0, unroll=ISSUE_UNROLL)

    def wait_rows(slot):
        def body(t, carry):
            pltpu.make_async_copy(_row(hbuf_ref.at[0], 0), _row(xs_hbm, 0), sem_rows.at[slot]).wait()
            return carry
        lax.fori_loop(0, 2 * tr, body, 0, unroll=ISSUE_UNROLL)

    @pl.when(i == 0)
    def _():
        for cp in tile_copies(0):
            cp.start()
        zero_ref[...] = jnp.zeros_like(zero_ref)
        _zero_fill(zinfo_ref, zero_ref, xs_hbm, sem_rows.at[1], wait=False)
        _zero_fill(zinfo_ref, zero_ref, xs_hbm, sem_rows.at[1], wait=True)

    @pl.when(i + 1 < nt)
    def _():
        for cp in tile_copies(i + 1):
            cp.start()

    for cp in tile_copies(i):
        cp.wait()

    for slot in range(RING):
        @pl.when(i % RING == slot)
        def _():
            scatter_rows(slot)

            @pl.when(i > 0)
            def _():
                wait_rows((slot - 1) % RING)

            @pl.when(i == nt - 1)
            def _():
                wait_rows(slot)


def _dispatch(h, dest, zoff, n_slots):
    t = h.shape[0] // ROW_SUB
    tr = ROW_TILE
    nt = t // tr
    grid_spec = pltpu.PrefetchScalarGridSpec(
        num_scalar_prefetch=1,
        grid=(nt,),
        in_specs=[pl.BlockSpec(memory_space=pl.ANY), pl.BlockSpec(memory_space=pl.ANY)],
        out_specs=pl.BlockSpec(memory_space=pl.ANY),
        scratch_shapes=[pltpu.SMEM((RING, 2, tr), I32), pltpu.VMEM((RING, tr * ROW_SUB, LANES), F32),
                        pltpu.VMEM((EXP_BLOCK * ROW_SUB, LANES), F32),
                        pltpu.SemaphoreType.DMA((RING,)), pltpu.SemaphoreType.DMA((RING,)),
                        pltpu.SemaphoreType.DMA((RING,))],
    )
    return pl.pallas_call(
        _dispatch_kernel,
        grid_spec=grid_spec,
        out_shape=jax.ShapeDtypeStruct((n_slots * ROW_SUB, LANES), F32),
        compiler_params=_cparams("arbitrary"),
        name="moe_dispatch",
    )(zoff, dest.reshape(2, nt, tr), h)


def _expert_kernel(be_ref, nused_ref, xs_ref, w1_ref, w3_ref, w2_ref, ys_ref, w1b_ref, w3b_ref, w2b_ref):
    i = pl.program_id(0)
    used = i < nused_ref[0]

    @pl.when(used & ((i == 0) | (be_ref[i] != be_ref[jnp.maximum(i - 1, 0)])))
    def _():
        w1b_ref[...] = w1_ref[0, 0].astype(BF16)
        w3b_ref[...] = w3_ref[0, 0].astype(BF16)
        w2b_ref[...] = w2_ref[0, 0].astype(BF16)

    @pl.when(used)
    def _():
        xb = _load_rows(xs_ref, 0, EXP_BLOCK, ROW_SUB, BF16)
        a = _dot(xb, w1b_ref[...])
        b = _dot(xb, w3b_ref[...])
        hid = (a * jax.nn.sigmoid(a) * b).astype(BF16)
        _store_rows(ys_ref, _dot(hid, w2b_ref[...]))

    @pl.when(i >= nused_ref[0])
    def _():
        ys_ref[...] = jnp.zeros_like(ys_ref)


def _experts(xs, block_e, n_used, w1, w3, w2, layer):
    bm = EXP_BLOCK
    n_slots = xs.shape[0] // ROW_SUB
    nblk = n_slots // bm
    d, ff = w1.shape[-2:]

    def last_used(i, nu):
        return jnp.maximum(jnp.minimum(i, nu[0] - 1), 0)

    def xmap(i, be, nu):
        return (last_used(i, nu), 0)

    def wmap(i, be, nu):
        return (layer, be[last_used(i, nu)], 0, 0)

    grid_spec = pltpu.PrefetchScalarGridSpec(
        num_scalar_prefetch=2,
        grid=(nblk,),
        in_specs=[pl.BlockSpec((bm * ROW_SUB, LANES), xmap), pl.BlockSpec((1, 1, d, ff), wmap),
                  pl.BlockSpec((1, 1, d, ff), wmap), pl.BlockSpec((1, 1, ff, d), wmap)],
        out_specs=pl.BlockSpec((bm * ROW_SUB, LANES), lambda i, be, nu: (i, 0)),
        scratch_shapes=[pltpu.VMEM((d, ff), BF16), pltpu.VMEM((d, ff), BF16), pltpu.VMEM((ff, d), BF16)],
    )
    return pl.pallas_call(
        _expert_kernel,
        grid_spec=grid_spec,
        out_shape=jax.ShapeDtypeStruct((n_slots * ROW_SUB, LANES), F32),
        compiler_params=_cparams("arbitrary"),
        name="moe_experts",
    )(block_e, n_used, xs, w1, w3, w2)


def _combine_tiles(dest_hbm, ys_hbm, x_ref, g_ref, o_ref, idx_ref, buf_ref, sem_idx, sem_rows, finish):
    i = pl.program_id(0)
    nt = pl.num_programs(0)
    tr = x_ref.shape[0]

    def gather(slot):
        buf = buf_ref.at[slot]

        def body(t, carry):
            pltpu.make_async_copy(_row(ys_hbm, idx_ref[slot, 0, t]), _row(buf, t), sem_rows.at[slot]).start()
            pltpu.make_async_copy(_row(ys_hbm, idx_ref[slot, 1, t]), _row(buf, tr + t), sem_rows.at[slot]).start()
            return carry

        lax.fori_loop(0, tr, body, 0, unroll=ISSUE_UNROLL)

    def wait_rows(slot):
        def body(t, carry):
            pltpu.make_async_copy(_row(ys_hbm, 0), _row(buf_ref.at[0], 0), sem_rows.at[slot]).wait()
            return carry
        lax.fori_loop(0, 2 * tr, body, 0, unroll=ISSUE_UNROLL)

    @pl.when(i == 0)
    def _():
        for cp in _slot_copies(dest_hbm, 0, idx_ref, 0, sem_idx):
            cp.start()
        for cp in _slot_copies(dest_hbm, 0, idx_ref, 0, sem_idx):
            cp.wait()
        gather(0)

        @pl.when(nt > 1)
        def _():
            for cp in _slot_copies(dest_hbm, 1, idx_ref, 1, sem_idx):
                cp.start()

    for slot in range(2):
        @pl.when(i % 2 == slot)
        def _():
            nxt = 1 - slot

            @pl.when(i + 1 < nt)
            def _():
                for cp in _slot_copies(dest_hbm, i + 1, idx_ref, nxt, sem_idx):
                    cp.wait()
                gather(nxt)

            @pl.when(i + 2 < nt)
            def _():
                for cp in _slot_copies(dest_hbm, i + 2, idx_ref, slot, sem_idx):
                    cp.start()

            wait_rows(slot)
            g = g_ref[...]
            y1 = _load_rows(buf_ref.at[slot], 0, tr, ROW_SUB, F32)
            y2 = _load_rows(buf_ref.at[slot], tr, tr, ROW_SUB, F32)
            o_ref[...] = finish(x_ref[...] + g[:, 0:1] * y1 + g[:, 1:2] * y2)


def _combine_scratch(tr):
    return [pltpu.SMEM((2, 2, tr), I32), pltpu.VMEM((2, 2 * tr * ROW_SUB, LANES), F32),
            pltpu.SemaphoreType.DMA((2,)), pltpu.SemaphoreType.DMA((2,))]


def _combine_norm_kernel(dest_hbm, ys_hbm, x_ref, g_ref, nw_ref, o_ref, *scratch):
    _combine_tiles(dest_hbm, ys_hbm, x_ref, g_ref, o_ref, *scratch, finish=lambda v: _rms(v, nw_ref[...]))


def _combine_norm(ys, dest, x, gates, nw):
    t, d = x.shape
    tr = ROW_TILE
    nt = t // tr
    return pl.pallas_call(
        _combine_norm_kernel,
        grid=(nt,),
        in_specs=[pl.BlockSpec(memory_space=pl.ANY), pl.BlockSpec(memory_space=pl.ANY),
                  pl.BlockSpec((tr, d), lambda i: (i, 0)), pl.BlockSpec((tr, 2), lambda i: (i, 0)),
                  pl.BlockSpec((1, d), lambda i: (0, 0))],
        out_specs=pl.BlockSpec((tr, d), lambda i: (i, 0)),
        out_shape=jax.ShapeDtypeStruct((t, d), F32),
        scratch_shapes=_combine_scratch(tr),
        compiler_params=_cparams("arbitrary"),
        name="moe_combine",
    )(dest.reshape(2, nt, tr), ys, x, gates.T, nw.reshape(1, d))


def _slots_kernel(ids_ref, pst_ref, dest_ref):
    ids = ids_ref[...]
    ei = lax.broadcasted_iota(I32, (MOE_EXPERTS, ids.shape[1]), 0)
    pst = pst_ref[...]
    d1 = jnp.sum(jnp.where(ei == ids[0:1], pst, 0), axis=0, keepdims=True) + ids[2:3]
    d2 = jnp.sum(jnp.where(ei == ids[1:2], pst, 0), axis=0, keepdims=True) + ids[3:4]
    dest_ref[...] = jnp.concatenate([d1, d2], axis=0)


def _slots(ids, pstarts):
    t = ids.shape[1]
    tm = TOK_TILE
    return pl.pallas_call(
        _slots_kernel,
        grid=(t // tm,),
        in_specs=[pl.BlockSpec((4, tm), lambda i: (0, i)), pl.BlockSpec((MOE_EXPERTS, 1), lambda i: (0, 0))],
        out_specs=pl.BlockSpec((2, tm), lambda i: (0, i)),
        out_shape=jax.ShapeDtypeStruct((2, t), I32),
        compiler_params=_cparams("arbitrary"),
        name="moe_slots",
    )(ids, pstarts.reshape(MOE_EXPERTS, 1))


def _combine_plain_kernel(dest_hbm, ys_hbm, x_ref, g_ref, o_ref, *scratch):
    _combine_tiles(dest_hbm, ys_hbm, x_ref, g_ref, o_ref, *scratch, finish=lambda v: v)


def _combine_plain(ys, dest, x, gates):
    t, d = x.shape
    tr = ROW_TILE
    nt = t // tr
    return pl.pallas_call(
        _combine_plain_kernel,
        grid=(nt,),
        in_specs=[pl.BlockSpec(memory_space=pl.ANY), pl.BlockSpec(memory_space=pl.ANY),
                  pl.BlockSpec((tr, d), lambda i: (i, 0)), pl.BlockSpec((tr, 2), lambda i: (i, 0))],
        out_specs=pl.BlockSpec((tr, d), lambda i: (i, 0)),
        out_shape=jax.ShapeDtypeStruct((t, d), F32),
        scratch_shapes=_combine_scratch(tr),
        compiler_params=_cparams("arbitrary"),
        name="moe_combine",
    )(dest.reshape(2, nt, tr), ys, x, gates.T)


def _moe(h, t, ids, counts, w1, w3, w2, layer):
    bm = EXP_BLOCK
    n_asg = 2 * t
    n_slots = n_asg + MOE_EXPERTS * bm
    cnt = counts[:, 0]
    padded = (cnt + bm - 1) // bm * bm
    pend = jnp.cumsum(padded)
    pstarts = (pend - padded).astype(I32)
    dest = _slots(ids, pstarts)
    nblk = n_slots // bm
    block_start = jnp.arange(nblk, dtype=I32) * bm
    block_e = jnp.minimum(jnp.sum(pend[None, :] <= block_start[:, None], axis=1), MOE_EXPERTS - 1).astype(I32)
    n_used = (pend[-1:] // bm).astype(I32)
    zinfo = jnp.concatenate([pstarts + cnt, padded - cnt, n_used]).astype(I32)
    xs = _dispatch(h, dest, zinfo, n_slots)
    ys = _experts(xs, block_e, n_used, w1, w3, w2, layer)
    return ys, dest


def _mla_in_combine_kernel(dest_hbm, ys_hbm, g_ref, x_ref, *rest):
    ins, (xc_ref, qt_ref, k_ref, vt_ref), scratch = rest[:12], rest[12:16], rest[16:]
    _combine_tiles(dest_hbm, ys_hbm, x_ref, g_ref, xc_ref, *scratch, finish=lambda v: v)
    _mla_in_kernel(xc_ref, *ins, qt_ref, k_ref, vt_ref)


def _mla_in_kernel(x_ref, pos_ref, nw_ref, wq_ref, wkv_ref, wks_ref, qn_ref, kvn_ref, wqn_ref, wqr_ref, wqs_ref,
                   wuk_ref, freq_ref, qt_ref, k_ref, vt_ref):
    tm = x_ref.shape[0]
    tb = ATT_BLOCK
    hb = _rms(x_ref[...], nw_ref[...]).astype(BF16)
    c_q = _rms(_dot(hb, wq_ref[...]), qn_ref[...]).astype(BF16)
    kv_t = _dot_nt(wkv_ref[...], hb)
    ks_t = _dot_nt(wks_ref[...], hb)
    c_kv = kv_t[:MLA_KV_LORA]
    c_kv = c_kv * lax.rsqrt(jnp.mean(c_kv * c_kv, axis=0, keepdims=True) + EPS) * kvn_ref[...]
    ang = freq_ref[...] * pos_ref[...].astype(F32)
    cos32, sin32 = jnp.cos(ang), jnp.sin(ang)
    cos = jnp.concatenate([cos32] * 4, axis=0)
    sin = jnp.concatenate([-sin32, sin32] * 2, axis=0)
    k_rope = kv_t[MLA_KV_LORA:] * cos + ks_t * sin
    k_ref[...] = jnp.concatenate([c_kv, k_rope], axis=0).T.astype(BF16)
    v_ext = jnp.concatenate([c_kv, jnp.ones((VT_ROWS - MLA_KV_LORA, tm), F32)], axis=0).astype(BF16)
    for c in range(tm // tb):
        vt_ref[c] = v_ext[:, c * tb:(c + 1) * tb]
    q_nope = _dot_nt(wqn_ref[...], c_q).astype(BF16)
    q_r = _dot_nt(wqr_ref[...], c_q)
    q_s = _dot_nt(wqs_ref[...], c_q)
    for h in range(MLA_HEADS):
        sl = slice(h * LANES, (h + 1) * LANES)
        q_lat = (_dot(wuk_ref[h], q_nope[sl]) * Q_SCALE).astype(BF16)
        q_rope = ((q_r[sl] * cos + q_s[sl] * sin) * Q_SCALE).astype(BF16)
        for c in range(tm // tb):
            cs = slice(c * tb, (c + 1) * tb)
            qt_ref[c, 2 * h * LANES:(2 * h + 1) * LANES, :] = q_lat[:, cs]
            qt_ref[c, (2 * h + 1) * LANES:(2 * h + 2) * LANES, :] = q_rope[:, cs]


def _swap_halves(w):
    half = w.shape[-1] // 2
    return jnp.concatenate([w[..., half:], w[..., :half]], axis=-1)


def _mla_in(x, pos, nw, w_in, q_norm, w_uq, kv_norm, w_uk, pending=None):
    t, d = x.shape
    tm = TOK_TILE
    tb = ATT_BLOCK
    nh = MLA_HEADS
    pad_r = LANES - MLA_ROPE
    w_kr = w_in[:, MLA_Q_LORA + MLA_KV_LORA:]
    wq = w_in[:, :MLA_Q_LORA].astype(BF16)
    wkv = jnp.pad(w_in[:, MLA_Q_LORA:], ((0, 0), (0, pad_r))).T.astype(BF16)
    wks = jnp.pad(_swap_halves(w_kr), ((0, 0), (0, pad_r))).T.astype(BF16)
    wqn = w_uq[:, :, :MLA_NOPE].reshape(MLA_Q_LORA, nh * MLA_NOPE).T.astype(BF16)
    w_r = w_uq[:, :, MLA_NOPE:]
    wqr = jnp.pad(w_r, ((0, 0), (0, 0), (0, pad_r))).reshape(MLA_Q_LORA, nh * LANES).T.astype(BF16)
    wqs = jnp.pad(_swap_halves(w_r), ((0, 0), (0, 0), (0, pad_r))).reshape(MLA_Q_LORA, nh * LANES).T.astype(BF16)
    wuk = w_uk.transpose(1, 0, 2).astype(BF16)
    inv_freq = 1.0 / (ROPE_THETA ** (jnp.arange(0, MLA_ROPE, 2, dtype=F32) / MLA_ROPE))
    full = lambda a: pl.BlockSpec(a.shape, lambda i: (0,) * a.ndim)
    args = (x, pos.reshape(1, t), nw.reshape(1, d), wq, wkv, wks, q_norm.reshape(1, -1), kv_norm.reshape(-1, 1),
            wqn, wqr, wqs, wuk, inv_freq.reshape(-1, 1))
    nsub = tm // tb
    nt = t // tm
    in_specs = [pl.BlockSpec((tm, d), lambda i: (i, 0)), pl.BlockSpec((1, tm), lambda i: (0, i))] \
        + [full(a) for a in args[2:]]
    out_specs = [pl.BlockSpec((nsub, 2 * nh * LANES, tb), lambda i: (i, 0, 0)),
                 pl.BlockSpec((tm, 2 * LANES), lambda i: (i, 0)),
                 pl.BlockSpec((nsub, VT_ROWS, tb), lambda i: (i, 0, 0))]
    out_shape = [jax.ShapeDtypeStruct((t // tb, 2 * nh * LANES, tb), BF16),
                 jax.ShapeDtypeStruct((t, 2 * LANES), BF16),
                 jax.ShapeDtypeStruct((t // tb, VT_ROWS, tb), BF16)]
    if pending is None:
        qt, k, vt = pl.pallas_call(
            _mla_in_kernel, grid=(nt,), in_specs=in_specs, out_specs=out_specs, out_shape=out_shape,
            compiler_params=_cparams("arbitrary"), name="mla_in",
        )(*args)
        return x, qt, k, vt
    ys, dest, gates = pending
    any_spec = pl.BlockSpec(memory_space=pl.ANY)
    return pl.pallas_call(
        _mla_in_combine_kernel,
        grid=(nt,),
        in_specs=[any_spec, any_spec, pl.BlockSpec((tm, 2), lambda i: (i, 0))] + in_specs,
        out_specs=[pl.BlockSpec((tm, d), lambda i: (i, 0))] + out_specs,
        out_shape=[jax.ShapeDtypeStruct((t, d), F32)] + out_shape,
        scratch_shapes=_combine_scratch(tm),
        compiler_params=_cparams("arbitrary"),
        name="mla_in_combine",
    )(dest.reshape(2, nt, tm), ys, gates.T, *args)


def _mla_attn_kernel(qt_ref, k_ref, vt_ref, wuv_ref, o_ref, *state):
    i = pl.program_id(1)
    tb = ATT_BLOCK
    nh = MLA_HEADS
    dq = 2 * LANES
    s_refs, smax_refs, m_refs, acc_refs = state[0::4], state[1::4], state[2::4], state[3::4]

    def scores(j, h):
        kb = k_ref[pl.ds(pl.multiple_of(j * tb, tb), tb), :]
        s = _dot(kb, qt_ref[0, h * dq:(h + 1) * dq, :])
        return s, jnp.max(s, axis=0, keepdims=True)

    def consume(j, h, s, s_max):
        m_old = m_refs[h][...]
        m_new = jnp.maximum(m_old, s_max)
        alpha = jnp.exp2(m_old - m_new)
        p = jnp.exp2(s - m_new).astype(BF16)
        acc_refs[h][...] = alpha * acc_refs[h][...] + _dot(vt_ref[j], p)
        m_refs[h][...] = m_new

    for h in range(nh):
        m_refs[h][...] = jnp.full_like(m_refs[h], -jnp.inf)
        acc_refs[h][...] = jnp.zeros_like(acc_refs[h])
        s_refs[h][...], smax_refs[h][...] = scores(0, h)

    def full_block(j, carry):
        for h in range(nh):
            s, s_max = s_refs[h][...], smax_refs[h][...]
            s_new, s_max_new = scores(j + 1, h)
            consume(j, h, s, s_max)
            s_refs[h][...], smax_refs[h][...] = s_new, s_max_new
        return carry

    lax.fori_loop(0, i, full_block, 0)
    kpos = lax.broadcasted_iota(I32, (tb, tb), 0)
    qpos = lax.broadcasted_iota(I32, (tb, tb), 1)
    for h in range(nh):
        s = jnp.where(kpos <= qpos, s_refs[h][...], -jnp.inf)
        consume(i, h, s, jnp.max(s, axis=0, keepdims=True))
    for h in range(nh):
        acc = acc_refs[h][...]
        o_lat = (acc[:MLA_KV_LORA] / acc[MLA_KV_LORA:MLA_KV_LORA + 1]).astype(BF16)
        o_ref[:, h * MLA_V:(h + 1) * MLA_V] = _dot_tn(o_lat, wuv_ref[h]).astype(BF16)


def _mla_attn(qt, k, vt, w_uv, bsz, seq):
    t = bsz * seq
    tb = ATT_BLOCK
    nq = seq // tb
    nh = MLA_HEADS
    wuv = w_uv.transpose(1, 0, 2).astype(BF16)
    return pl.pallas_call(
        _mla_attn_kernel,
        grid=(bsz, nq),
        in_specs=[pl.BlockSpec((1, 2 * nh * LANES, tb), lambda b, i: (b * nq + i, 0, 0)),
                  pl.BlockSpec((seq, 2 * LANES), lambda b, i: (b, 0)),
                  pl.BlockSpec((nq, VT_ROWS, tb), lambda b, i: (b, 0, 0)),
                  pl.BlockSpec(wuv.shape, lambda b, i: (0, 0, 0))],
        out_specs=pl.BlockSpec((tb, nh * MLA_V), lambda b, i: (b * nq + i, 0)),
        out_shape=jax.ShapeDtypeStruct((t, nh * MLA_V), BF16),
        scratch_shapes=[pltpu.VMEM((tb, tb), F32), pltpu.VMEM((1, tb), F32), pltpu.VMEM((1, tb), F32),
                        pltpu.VMEM((VT_ROWS, tb), F32)] * nh,
        compiler_params=_cparams("arbitrary", "arbitrary"),
        name="mla_attn",
    )(qt, k, vt, wuv)


def kernel(x, positions, attn_norm, ffn_norm, final_norm, gla_w_in, gla_w_gate, gla_b_gate, gla_out_norm, gla_w_o,
           mla_w_in, mla_q_norm, mla_w_uq, mla_kv_norm, mla_w_uk, mla_w_uv, mla_w_o,
           moe_w_group, moe_w_expert, moe_w1, moe_w3, moe_w2):
    bsz, seq, d = x.shape
    t = bsz * seq
    depth = attn_norm.shape[0]
    xf = x.reshape(t, d)
    pos = positions.reshape(t)
    pending = None
    for i in range(depth):
        j = i // 2
        if i % 2 == 0:
            if pending is not None:
                xf = _combine_plain(pending[0], pending[1], xf, pending[2])
            q, k, v, r, la = _gla_in(xf, attn_norm[i], gla_w_in[j], gla_w_gate[j], gla_b_gate[j])
            mixed = _gla_rec(q, k, v, r, la, gla_out_norm[j], bsz, seq)
            w_o = gla_w_o[j]
        else:
            xf, qt, kc, vt = _mla_in(xf, pos, attn_norm[i], mla_w_in[j], mla_q_norm[j], mla_w_uq[j],
                                     mla_kv_norm[j], mla_w_uk[j], pending)
            mixed = _mla_attn(qt, kc, vt, mla_w_uv[j], bsz, seq)
            w_o = mla_w_o[j]
        xf, h, ids, gates, counts = _post(mixed, w_o, xf, ffn_norm[i], moe_w_group[i], moe_w_expert[i])
        ys, dest = _moe(h, t, ids, counts, moe_w1, moe_w3, moe_w2, i)
        pending = (ys, dest, gates)
    out = _combine_norm(pending[0], pending[1], xf, pending[2], final_norm)
    return out.reshape(bsz, seq, d)
```

```python
import functools

import jax
import jax.numpy as jnp
from jax import lax
from jax.experimental import pallas as pl
from jax.experimental.pallas import tpu as pltpu

F32 = jnp.float32
BF16 = jnp.bfloat16
I32 = jnp.int32
HIGHEST = lax.Precision.HIGHEST

EPS = 1e-6
GLA_HEADS = 4
GLA_DK = 128
GLA_DV = 256
GLA_GATE_RANK = 16
GLA_GATE_TAU = 16.0
GLA_CHUNK = 64
MLA_HEADS = 8
MLA_NOPE = 128
MLA_ROPE = 64
MLA_V = 128
MLA_Q_LORA = 256
MLA_KV_LORA = 128
MLA_SCALE = (MLA_NOPE + MLA_ROPE) ** -0.5
Q_SCALE = MLA_SCALE * 1.4426950408889634
VT_ROWS = MLA_KV_LORA + 16
ROPE_THETA = 10000.0
MOE_GROUPS = 8
MOE_PER_GROUP = 8
MOE_EXPERTS = MOE_GROUPS * MOE_PER_GROUP
MOE_FF = 256

D_MODEL = 1024
LANES = 128
ROW_SUB = D_MODEL // LANES
VMEM_LIMIT = 48 * 1024 * 1024

TOK_TILE = 512
GLA_BLOCK = 256
ATT_BLOCK = 256
ATT_KV = 512
ROW_TILE = 256
EXP_BLOCK = 512


def _cparams(*sem):
    return pltpu.CompilerParams(dimension_semantics=sem, vmem_limit_bytes=VMEM_LIMIT)


def _rms(x, w):
    return x * lax.rsqrt(jnp.mean(x * x, axis=-1, keepdims=True) + EPS) * w


def _dot(a, b):
    return jnp.dot(a, b, preferred_element_type=F32)


def _dot_nt(a, b):
    return lax.dot_general(a, b, (((1,), (1,)), ((), ())), preferred_element_type=F32)


def _dot_tn(a, b):
    return lax.dot_general(a, b, (((0,), (0,)), ((), ())), preferred_element_type=F32)


def _split_bf16(x, pieces):
    out = []
    for _ in range(pieces - 1):
        hi = x.astype(BF16)
        out.append(hi)
        x = x - hi.astype(F32)
    out.append(x.astype(BF16))
    return out


def _gla_in_kernel(x_ref, nw_ref, wq_ref, wk_ref, wv_ref, wr_ref, wa_ref, wgh_ref, wgl_ref, bg_ref,
                   q_ref, k_ref, v_ref, r_ref, la_ref):
    hb = _rms(x_ref[...], nw_ref[...]).astype(BF16)
    a_hi, a_lo = _split_bf16(_dot(hb, wa_ref[...]), 2)
    z = _dot(a_hi, wgh_ref[...]) + _dot(a_hi, wgl_ref[...]) + _dot(a_lo, wgh_ref[...]) + bg_ref[...]
    log_sig = jnp.minimum(z, 0.0) - jnp.log1p(jnp.exp(-jnp.abs(z)))
    la_ref[...] = log_sig * (1.0 / GLA_GATE_TAU)
    q_ref[...] = _dot(hb, wq_ref[...]) * (GLA_DK ** -0.5)
    k_ref[...] = _dot(hb, wk_ref[...])
    v_ref[...] = _dot(hb, wv_ref[...]).astype(BF16)
    r_ref[...] = _dot(hb, wr_ref[...])


def _gla_in(x, nw, w_in, w_gate, b_gate):
    t, d = x.shape
    qk = GLA_HEADS * GLA_DK
    vd = GLA_HEADS * GLA_DV
    wq = w_in[:, :qk].astype(BF16)
    wk = w_in[:, qk:2 * qk].astype(BF16)
    wv = w_in[:, 2 * qk:2 * qk + vd].astype(BF16)
    wa = jnp.pad(w_in[:, 2 * qk + vd:2 * qk + vd + GLA_GATE_RANK], ((0, 0), (0, LANES - GLA_GATE_RANK))).astype(BF16)
    wr = w_in[:, 2 * qk + vd + GLA_GATE_RANK:].astype(BF16)
    wg = jnp.pad(w_gate, ((0, LANES - GLA_GATE_RANK), (0, 0)))
    wgh = wg.astype(BF16)
    wgl = (wg - wgh.astype(F32)).astype(BF16)
    tm = TOK_TILE
    row = lambda n: pl.BlockSpec((tm, n), lambda i: (i, 0))
    full = lambda a: pl.BlockSpec(a.shape, lambda i: (0, 0))
    nw2, bg2 = nw.reshape(1, d), b_gate.reshape(1, qk)
    return pl.pallas_call(
        _gla_in_kernel,
        grid=(t // tm,),
        in_specs=[row(d), full(nw2), full(wq), full(wk), full(wv), full(wr), full(wa), full(wgh), full(wgl),
                  full(bg2)],
        out_specs=[row(qk), row(qk), row(vd), row(vd), row(qk)],
        out_shape=[jax.ShapeDtypeStruct((t, qk), F32), jax.ShapeDtypeStruct((t, qk), F32),
                   jax.ShapeDtypeStruct((t, vd), BF16), jax.ShapeDtypeStruct((t, vd), F32),
                   jax.ShapeDtypeStruct((t, qk), F32)],
        compiler_params=_cparams("arbitrary"),
        name="gla_in",
    )(x, nw2, wq, wk, wv, wr, wa, wgh, wgl, bg2)


def _gla_rec_kernel(q_ref, k_ref, v_ref, r_ref, la_ref, onw_ref, o_ref, st_ref):
    c = GLA_CHUNK
    blk = GLA_BLOCK
    nc = blk // c

    @pl.when(pl.program_id(1) == 0)
    def _():
        st_ref[...] = jnp.zeros_like(st_ref)

    rows = lax.broadcasted_iota(I32, (blk, blk), 0)
    cols = lax.broadcasted_iota(I32, (blk, blk), 1)
    causal = (cols <= rows) & (cols // c == rows // c)
    tri = jnp.where(causal, 1.0, 0.0).astype(BF16)
    g = sum(_dot(tri, piece) for piece in _split_bf16(la_ref[...], 3))
    g_mid = jnp.concatenate(
        [jnp.broadcast_to(g[ci * c + c // 2:ci * c + c // 2 + 1], (c, g.shape[1])) for ci in range(nc)], axis=0)
    g_last = jnp.concatenate(
        [jnp.broadcast_to(g[ci * c + c - 1:ci * c + c], (c, g.shape[1])) for ci in range(nc)], axis=0)
    q = q_ref[...]
    k = k_ref[...]
    q_intra = (q * jnp.exp(g - g_mid)).astype(BF16)
    k_intra = (k * jnp.exp(g_mid - g)).astype(BF16)
    q_inter = (q * jnp.exp(g)).astype(BF16)
    k_state = (k * jnp.exp(g_last - g)).astype(BF16)
    onw = onw_ref[...]
    heads = range(GLA_HEADS)
    ksl = [slice(h * GLA_DK, (h + 1) * GLA_DK) for h in heads]
    vsl = [slice(h * GLA_DV, (h + 1) * GLA_DV) for h in heads]
    v = [v_ref[:, vsl[h]] for h in heads]
    a = [jnp.where(causal, _dot_nt(q_intra[:, ksl[h]], k_intra[:, ksl[h]]), 0.0).astype(BF16) for h in heads]
    kv = [[_dot_tn(v[h][ci * c:(ci + 1) * c], k_state[ci * c:(ci + 1) * c, ksl[h]]) for ci in range(nc)]
          for h in heads]
    o_intra = [_dot(a[h], v[h]) for h in heads]
    for h in heads:
        st = st_ref[h]
        o_inter = []
        for ci in range(nc):
            o_inter.append(_dot_nt(q_inter[ci * c:(ci + 1) * c, ksl[h]], st.astype(BF16)))
            st = st * jnp.exp(g[ci * c + c - 1:ci * c + c, ksl[h]]) + kv[h][ci]
        st_ref[h] = st
        o = _rms(o_intra[h] + jnp.concatenate(o_inter, axis=0), onw)
        r = r_ref[:, vsl[h]]
        o_ref[:, vsl[h]] = (o * (r * jax.nn.sigmoid(r))).astype(BF16)


def _gla_rec(q, k, v, r, la, onw, bsz, seq):
    t = bsz * seq
    qk = GLA_HEADS * GLA_DK
    vd = GLA_HEADS * GLA_DV
    nb = seq // GLA_BLOCK
    row = lambda n: pl.BlockSpec((GLA_BLOCK, n), lambda b, j: (b * nb + j, 0))
    onw2 = onw.reshape(1, GLA_DV)
    return pl.pallas_call(
        _gla_rec_kernel,
        grid=(bsz, nb),
        in_specs=[row(qk), row(qk), row(vd), row(vd), row(qk), pl.BlockSpec((1, GLA_DV), lambda b, j: (0, 0))],
        out_specs=row(vd),
        out_shape=jax.ShapeDtypeStruct((t, vd), BF16),
        scratch_shapes=[pltpu.VMEM((GLA_HEADS, GLA_DV, GLA_DK), F32)],
        compiler_params=_cparams("arbitrary", "arbitrary"),
        name="gla_rec",
    )(q, k, v, r, la, onw2)


def _store_rows(ref, val):
    n, d = val.shape
    ref[...] = val.reshape(n * d // LANES, LANES)


def _load_rows(ref, start, n, nc, dtype):
    return jnp.concatenate([ref[pl.ds(start * nc + c, n, stride=nc), :].astype(dtype) for c in range(nc)], axis=1)


def _post_kernel(a_ref, wo_ref, x_ref, nw_ref, wrh_ref, wrl_ref,
                 x1_ref, h_ref, ids_ref, gates_ref, cnt_ref, carry_ref):
    tm = a_ref.shape[0]
    ne = MOE_EXPERTS

    @pl.when(pl.program_id(0) == 0)
    def _():
        carry_ref[...] = jnp.zeros_like(carry_ref)

    x1 = x_ref[...] + _dot(a_ref[...], wo_ref[...])
    x1_ref[...] = x1
    h = _rms(x1, nw_ref[...])
    _store_rows(h_ref, h)
    h_hi, h_lo = _split_bf16(h, 2)
    lt = _dot_nt(wrh_ref[...], h_hi) + _dot_nt(wrh_ref[...], h_lo) + _dot_nt(wrl_ref[...], h_hi)
    gl = lt[0:MOE_GROUPS]
    gmax = jnp.max(gl, axis=0, keepdims=True)
    gi = lax.broadcasted_iota(I32, gl.shape, 0)
    g_sel = jnp.min(jnp.where(gl == gmax, gi, MOE_GROUPS), axis=0, keepdims=True)
    g_w = 1.0 / jnp.sum(jnp.exp(gl - gmax), axis=0, keepdims=True)
    el = lt[MOE_GROUPS:MOE_GROUPS + ne]
    ei = lax.broadcasted_iota(I32, el.shape, 0)
    in_group = (ei // MOE_PER_GROUP) == g_sel
    neg = jnp.float32(-jnp.inf)
    el1 = jnp.where(in_group, el, neg)
    l1 = jnp.max(el1, axis=0, keepdims=True)
    i1 = jnp.min(jnp.where(el1 == l1, ei, ne), axis=0, keepdims=True)
    el2 = jnp.where(ei == i1, neg, el1)
    l2 = jnp.max(el2, axis=0, keepdims=True)
    i2 = jnp.min(jnp.where(el2 == l2, ei, ne), axis=0, keepdims=True)
    e2 = jnp.exp(l2 - l1)
    gate1 = g_w / (1.0 + e2)
    gate2 = g_w * e2 / (1.0 + e2)
    oh1 = ei == i1
    oh2 = ei == i2
    both = jnp.where(oh1 | oh2, 1.0, 0.0)
    su = lax.broadcasted_iota(I32, (tm, tm), 0)
    tu = lax.broadcasted_iota(I32, (tm, tm), 1)
    upper = jnp.where(su < tu, 1.0, 0.0).astype(BF16)
    prefix = _dot(both.astype(BF16), upper) + carry_ref[:, 0:1]
    rank1 = jnp.sum(jnp.where(oh1, prefix, 0.0), axis=0, keepdims=True)
    rank2 = jnp.sum(jnp.where(oh2, prefix, 0.0), axis=0, keepdims=True)
    carry = carry_ref[...] + jnp.sum(both, axis=1, keepdims=True)
    carry_ref[...] = carry
    cnt_ref[...] = carry.astype(I32)
    ids_ref[...] = jnp.concatenate([i1, i2, rank1.astype(I32), rank2.astype(I32)], axis=0)
    gates_ref[...] = jnp.concatenate([gate1, gate2], axis=0)


def _post(a, wo, x, nw, w_group, w_expert):
    t, d = x.shape
    tm = TOK_TILE
    wr = jnp.concatenate([w_group.T, w_expert.T], axis=0)
    wr = jnp.pad(wr, ((0, LANES - wr.shape[0]), (0, 0)))
    wrh = wr.astype(BF16)
    wrl = (wr - wrh.astype(F32)).astype(BF16)
    wo = wo.astype(BF16)
    nw2 = nw.reshape(1, d)
    row = lambda n: pl.BlockSpec((tm, n), lambda i: (i, 0))
    full = lambda arr: pl.BlockSpec(arr.shape, lambda i: (0, 0))
    return pl.pallas_call(
        _post_kernel,
        grid=(t // tm,),
        in_specs=[row(a.shape[1]), full(wo), row(d), full(nw2), full(wrh), full(wrl)],
        out_specs=[row(d), pl.BlockSpec((tm * d // LANES, LANES), lambda i: (i, 0)),
                   pl.BlockSpec((4, tm), lambda i: (0, i)), pl.BlockSpec((2, tm), lambda i: (0, i)),
                   pl.BlockSpec((MOE_EXPERTS, LANES), lambda i: (0, 0))],
        out_shape=[jax.ShapeDtypeStruct((t, d), F32), jax.ShapeDtypeStruct((t * d // LANES, LANES), F32),
                   jax.ShapeDtypeStruct((4, t), I32), jax.ShapeDtypeStruct((2, t), F32),
                   jax.ShapeDtypeStruct((MOE_EXPERTS, LANES), I32)],
        scratch_shapes=[pltpu.VMEM((MOE_EXPERTS, LANES), F32)],
        compiler_params=_cparams("arbitrary"),
        name="post_router",
    )(a, wo, x, nw2, wrh, wrl)


def _row(ref, i):
    return ref.at[pl.ds(pl.multiple_of(i * ROW_SUB, ROW_SUB), ROW_SUB)]


def _zero_fill(zinfo_ref, zero_ref, xs_hbm, sem, wait):
    ne = MOE_EXPERTS
    nblk = xs_hbm.shape[0] // (EXP_BLOCK * ROW_SUB)

    def copy(off_rows, n_rows):
        off = pl.multiple_of(off_rows * ROW_SUB, ROW_SUB)
        cp = pltpu.make_async_copy(zero_ref.at[pl.ds(0, n_rows * ROW_SUB)],
                                   xs_hbm.at[pl.ds(off, n_rows * ROW_SUB)], sem)
        cp.wait() if wait else cp.start()

    def expert(e, carry):
        pad = zinfo_ref[ne + e]
        for b in range(EXP_BLOCK.bit_length() - 1):
            @pl.when(((pad >> b) & 1) == 1)
            def _():
                copy(zinfo_ref[e] + (pad & ((1 << b) - 1)), 1 << b)
        return carry

    def tail(kk, carry):
        blk = zinfo_ref[2 * ne] + kk

        @pl.when(blk < nblk)
        def _():
            copy(blk * EXP_BLOCK, EXP_BLOCK)
        return carry

    lax.fori_loop(0, ne, expert, 0)
    lax.fori_loop(0, ne + 1, tail, 0)


RING = 3
ISSUE_UNROLL = 8


def _slot_copies(dest_hbm, i, idx_ref, slot, sem):
    return [pltpu.make_async_copy(dest_hbm.at[a, i], idx_ref.at[slot, a], sem.at[slot]) for a in range(2)]


def _dispatch_kernel(zinfo_ref, dest_hbm, h_hbm, xs_hbm, idx_ref, hbuf_ref, zero_ref, sem_idx, sem_h, sem_rows):
    i = pl.program_id(0)
    nt = pl.num_programs(0)
    tr = ROW_TILE
    rows = tr * ROW_SUB

    def tile_copies(j):
        slot = j % RING
        src = h_hbm.at[pl.ds(pl.multiple_of(j * rows, rows), rows)]
        return [pltpu.make_async_copy(src, hbuf_ref.at[slot], sem_h.at[slot])] + \
            _slot_copies(dest_hbm, j, idx_ref, slot, sem_idx)

    def scatter_rows(slot):
        def body(t, carry):
            src = _row(hbuf_ref.at[slot], t)
            pltpu.make_async_copy(src, _row(xs_hbm, idx_ref[slot, 0, t]), sem_rows.at[slot]).start()
            pltpu.make_async_copy(src, _row(xs_hbm, idx_ref[slot, 1, t]), sem_rows.at[slot]).start()
            return carry
        lax.fori_loop(0, tr, body, 0, unroll=ISSUE_UNROLL)

    def wait_rows(slot):
        def body(t, carry):
            pltpu.make_async_copy(_row(hbuf_ref.at[0], 0), _row(xs_hbm, 0), sem_rows.at[slot]).wait()
            return carry
        lax.fori_loop(0, 2 * tr, body, 0, unroll=ISSUE_UNROLL)

    @pl.when(i == 0)
    def _():
        for cp in tile_copies(0):
            cp.start()
        zero_ref[...] = jnp.zeros_like(zero_ref)
        _zero_fill(zinfo_ref, zero_ref, xs_hbm, sem_rows.at[1], wait=False)
        _zero_fill(zinfo_ref, zero_ref, xs_hbm, sem_rows.at[1], wait=True)

    @pl.when(i + 1 < nt)
    def _():
        for cp in tile_copies(i + 1):
            cp.start()

    for cp in tile_copies(i):
        cp.wait()

    for slot in range(RING):
        @pl.when(i % RING == slot)
        def _():
            scatter_rows(slot)

            @pl.when(i > 0)
            def _():
                wait_rows((slot - 1) % RING)

            @pl.when(i == nt - 1)
            def _():
                wait_rows(slot)


def _dispatch(h, dest, zoff, n_slots):
    t = h.shape[0] // ROW_SUB
    tr = ROW_TILE
    nt = t // tr
    grid_spec = pltpu.PrefetchScalarGridSpec(
        num_scalar_prefetch=1,
        grid=(nt,),
        in_specs=[pl.BlockSpec(memory_space=pl.ANY), pl.BlockSpec(memory_space=pl.ANY)],
        out_specs=pl.BlockSpec(memory_space=pl.ANY),
        scratch_shapes=[pltpu.SMEM((RING, 2, tr), I32), pltpu.VMEM((RING, tr * ROW_SUB, LANES), F32),
                        pltpu.VMEM((EXP_BLOCK * ROW_SUB, LANES), F32),
                        pltpu.SemaphoreType.DMA((RING,)), pltpu.SemaphoreType.DMA((RING,)),
                        pltpu.SemaphoreType.DMA((RING,))],
    )
    return pl.pallas_call(
        _dispatch_kernel,
        grid_spec=grid_spec,
        out_shape=jax.ShapeDtypeStruct((n_slots * ROW_SUB, LANES), F32),
        compiler_params=_cparams("arbitrary"),
        name="moe_dispatch",
    )(zoff, dest.reshape(2, nt, tr), h)


def _expert_kernel(be_ref, nused_ref, xs_ref, w1_ref, w3_ref, w2_ref, ys_ref, w1b_ref, w3b_ref, w2b_ref):
    i = pl.program_id(0)
    used = i < nused_ref[0]

    @pl.when(used & ((i == 0) | (be_ref[i] != be_ref[jnp.maximum(i - 1, 0)])))
    def _():
        w1b_ref[...] = w1_ref[0, 0].astype(BF16)
        w3b_ref[...] = w3_ref[0, 0].astype(BF16)
        w2b_ref[...] = w2_ref[0, 0].astype(BF16)

    @pl.when(used)
    def _():
        xb = _load_rows(xs_ref, 0, EXP_BLOCK, ROW_SUB, BF16)
        a = _dot(xb, w1b_ref[...])
        b = _dot(xb, w3b_ref[...])
        hid = (a * jax.nn.sigmoid(a) * b).astype(BF16)
        _store_rows(ys_ref, _dot(hid, w2b_ref[...]))

    @pl.when(i >= nused_ref[0])
    def _():
        ys_ref[...] = jnp.zeros_like(ys_ref)


def _experts(xs, block_e, n_used, w1, w3, w2, layer):
    bm = EXP_BLOCK
    n_slots = xs.shape[0] // ROW_SUB
    nblk = n_slots // bm
    d, ff = w1.shape[-2:]

    def last_used(i, nu):
        return jnp.maximum(jnp.minimum(i, nu[0] - 1), 0)

    def xmap(i, be, nu):
        return (last_used(i, nu), 0)

    def wmap(i, be, nu):
        return (layer, be[last_used(i, nu)], 0, 0)

    grid_spec = pltpu.PrefetchScalarGridSpec(
        num_scalar_prefetch=2,
        grid=(nblk,),
        in_specs=[pl.BlockSpec((bm * ROW_SUB, LANES), xmap), pl.BlockSpec((1, 1, d, ff), wmap),
                  pl.BlockSpec((1, 1, d, ff), wmap), pl.BlockSpec((1, 1, ff, d), wmap)],
        out_specs=pl.BlockSpec((bm * ROW_SUB, LANES), lambda i, be, nu: (i, 0)),
        scratch_shapes=[pltpu.VMEM((d, ff), BF16), pltpu.VMEM((d, ff), BF16), pltpu.VMEM((ff, d), BF16)],
    )
    return pl.pallas_call(
        _expert_kernel,
        grid_spec=grid_spec,
        out_shape=jax.ShapeDtypeStruct((n_slots * ROW_SUB, LANES), F32),
        compiler_params=_cparams("arbitrary"),
        name="moe_experts",
    )(block_e, n_used, xs, w1, w3, w2)


def _combine_tiles(dest_hbm, ys_hbm, x_ref, g_ref, o_ref, idx_ref, buf_ref, sem_idx, sem_rows, finish):
    i = pl.program_id(0)
    nt = pl.num_programs(0)
    tr = x_ref.shape[0]

    def gather(slot):
        buf = buf_ref.at[slot]

        def body(t, carry):
            pltpu.make_async_copy(_row(ys_hbm, idx_ref[slot, 0, t]), _row(buf, t), sem_rows.at[slot]).start()
            pltpu.make_async_copy(_row(ys_hbm, idx_ref[slot, 1, t]), _row(buf, tr + t), sem_rows.at[slot]).start()
            return carry

        lax.fori_loop(0, tr, body, 0, unroll=ISSUE_UNROLL)

    def wait_rows(slot):
        def body(t, carry):
            pltpu.make_async_copy(_row(ys_hbm, 0), _row(buf_ref.at[0], 0), sem_rows.at[slot]).wait()
            return carry
        lax.fori_loop(0, 2 * tr, body, 0, unroll=ISSUE_UNROLL)

    @pl.when(i == 0)
    def _():
        for cp in _slot_copies(dest_hbm, 0, idx_ref, 0, sem_idx):
            cp.start()
        for cp in _slot_copies(dest_hbm, 0, idx_ref, 0, sem_idx):
            cp.wait()
        gather(0)

        @pl.when(nt > 1)
        def _():
            for cp in _slot_copies(dest_hbm, 1, idx_ref, 1, sem_idx):
                cp.start()

    for slot in range(2):
        @pl.when(i % 2 == slot)
        def _():
            nxt = 1 - slot

            @pl.when(i + 1 < nt)
            def _():
                for cp in _slot_copies(dest_hbm, i + 1, idx_ref, nxt, sem_idx):
                    cp.wait()
                gather(nxt)

            @pl.when(i + 2 < nt)
            def _():
                for cp in _slot_copies(dest_hbm, i + 2, idx_ref, slot, sem_idx):
                    cp.start()

            wait_rows(slot)
            g = g_ref[...]
            y1 = _load_rows(buf_ref.at[slot], 0, tr, ROW_SUB, F32)
            y2 = _load_rows(buf_ref.at[slot], tr, tr, ROW_SUB, F32)
            o_ref[...] = finish(x_ref[...] + g[:, 0:1] * y1 + g[:, 1:2] * y2)


def _combine_scratch(tr):
    return [pltpu.SMEM((2, 2, tr), I32), pltpu.VMEM((2, 2 * tr * ROW_SUB, LANES), F32),
            pltpu.SemaphoreType.DMA((2,)), pltpu.SemaphoreType.DMA((2,))]


def _combine_norm_kernel(dest_hbm, ys_hbm, x_ref, g_ref, nw_ref, o_ref, *scratch):
    _combine_tiles(dest_hbm, ys_hbm, x_ref, g_ref, o_ref, *scratch, finish=lambda v: _rms(v, nw_ref[...]))


def _combine_norm(ys, dest, x, gates, nw):
    t, d = x.shape
    tr = ROW_TILE
    nt = t // tr
    return pl.pallas_call(
        _combine_norm_kernel,
        grid=(nt,),
        in_specs=[pl.BlockSpec(memory_space=pl.ANY), pl.BlockSpec(memory_space=pl.ANY),
                  pl.BlockSpec((tr, d), lambda i: (i, 0)), pl.BlockSpec((tr, 2), lambda i: (i, 0)),
                  pl.BlockSpec((1, d), lambda i: (0, 0))],
        out_specs=pl.BlockSpec((tr, d), lambda i: (i, 0)),
        out_shape=jax.ShapeDtypeStruct((t, d), F32),
        scratch_shapes=_combine_scratch(tr),
        compiler_params=_cparams("arbitrary"),
        name="moe_combine",
    )(dest.reshape(2, nt, tr), ys, x, gates.T, nw.reshape(1, d))


def _slots_kernel(ids_ref, pst_ref, dest_ref):
    ids = ids_ref[...]
    ei = lax.broadcasted_iota(I32, (MOE_EXPERTS, ids.shape[1]), 0)
    pst = pst_ref[...]
    d1 = jnp.sum(jnp.where(ei == ids[0:1], pst, 0), axis=0, keepdims=True) + ids[2:3]
    d2 = jnp.sum(jnp.where(ei == ids[1:2], pst, 0), axis=0, keepdims=True) + ids[3:4]
    dest_ref[...] = jnp.concatenate([d1, d2], axis=0)


def _slots(ids, pstarts):
    t = ids.shape[1]
    tm = TOK_TILE
    return pl.pallas_call(
        _slots_kernel,
        grid=(t // tm,),
        in_specs=[pl.BlockSpec((4, tm), lambda i: (0, i)), pl.BlockSpec((MOE_EXPERTS, 1), lambda i: (0, 0))],
        out_specs=pl.BlockSpec((2, tm), lambda i: (0, i)),
        out_shape=jax.ShapeDtypeStruct((2, t), I32),
        compiler_params=_cparams("arbitrary"),
        name="moe_slots",
    )(ids, pstarts.reshape(MOE_EXPERTS, 1))


def _combine_plain_kernel(dest_hbm, ys_hbm, x_ref, g_ref, o_ref, *scratch):
    _combine_tiles(dest_hbm, ys_hbm, x_ref, g_ref, o_ref, *scratch, finish=lambda v: v)


def _combine_plain(ys, dest, x, gates):
    t, d = x.shape
    tr = ROW_TILE
    nt = t // tr
    return pl.pallas_call(
        _combine_plain_kernel,
        grid=(nt,),
        in_specs=[pl.BlockSpec(memory_space=pl.ANY), pl.BlockSpec(memory_space=pl.ANY),
                  pl.BlockSpec((tr, d), lambda i: (i, 0)), pl.BlockSpec((tr, 2), lambda i: (i, 0))],
        out_specs=pl.BlockSpec((tr, d), lambda i: (i, 0)),
        out_shape=jax.ShapeDtypeStruct((t, d), F32),
        scratch_shapes=_combine_scratch(tr),
        compiler_params=_cparams("arbitrary"),
        name="moe_combine",
    )(dest.reshape(2, nt, tr), ys, x, gates.T)


def _moe(h, t, ids, counts, w1, w3, w2, layer):
    bm = EXP_BLOCK
    n_asg = 2 * t
    n_slots = n_asg + MOE_EXPERTS * bm
    cnt = counts[:, 0]
    padded = (cnt + bm - 1) // bm * bm
    pend = jnp.cumsum(padded)
    pstarts = (pend - padded).astype(I32)
    dest = _slots(ids, pstarts)
    nblk = n_slots // bm
    block_start = jnp.arange(nblk, dtype=I32) * bm
    block_e = jnp.minimum(jnp.sum(pend[None, :] <= block_start[:, None], axis=1), MOE_EXPERTS - 1).astype(I32)
    n_used = (pend[-1:] // bm).astype(I32)
    zinfo = jnp.concatenate([pstarts + cnt, padded - cnt, n_used]).astype(I32)
    xs = _dispatch(h, dest, zinfo, n_slots)
    ys = _experts(xs, block_e, n_used, w1, w3, w2, layer)
    return ys, dest


def _mla_in_combine_kernel(dest_hbm, ys_hbm, g_ref, x_ref, *rest):
    ins, (xc_ref, qt_ref, k_ref, vt_ref), scratch = rest[:12], rest[12:16], rest[16:]
    _combine_tiles(dest_hbm, ys_hbm, x_ref, g_ref, xc_ref, *scratch, finish=lambda v: v)
    _mla_in_kernel(xc_ref, *ins, qt_ref, k_ref, vt_ref)


def _mla_in_kernel(x_ref, pos_ref, nw_ref, wq_ref, wkv_ref, wks_ref, qn_ref, kvn_ref, wqn_ref, wqr_ref, wqs_ref,
                   wuk_ref, freq_ref, qt_ref, k_ref, vt_ref):
    tm = x_ref.shape[0]
    tb = ATT_BLOCK
    hb = _rms(x_ref[...], nw_ref[...]).astype(BF16)
    c_q = _rms(_dot(hb, wq_ref[...]), qn_ref[...]).astype(BF16)
    kv_t = _dot_nt(wkv_ref[...], hb)
    ks_t = _dot_nt(wks_ref[...], hb)
    c_kv = kv_t[:MLA_KV_LORA]
    c_kv = c_kv * lax.rsqrt(jnp.mean(c_kv * c_kv, axis=0, keepdims=True) + EPS) * kvn_ref[...]
    ang = freq_ref[...] * pos_ref[...].astype(F32)
    cos32, sin32 = jnp.cos(ang), jnp.sin(ang)
    cos = jnp.concatenate([cos32] * 4, axis=0)
    sin = jnp.concatenate([-sin32, sin32] * 2, axis=0)
    k_rope = kv_t[MLA_KV_LORA:] * cos + ks_t * sin
    k_ref[...] = jnp.concatenate([c_kv, k_rope], axis=0).T.astype(BF16)
    v_ext = jnp.concatenate([c_kv, jnp.ones((VT_ROWS - MLA_KV_LORA, tm), F32)], axis=0).astype(BF16)
    for c in range(tm // ATT_KV):
        vt_ref[c] = v_ext[:, c * ATT_KV:(c + 1) * ATT_KV]
    q_nope = _dot_nt(wqn_ref[...], c_q).astype(BF16)
    q_r = _dot_nt(wqr_ref[...], c_q)
    q_s = _dot_nt(wqs_ref[...], c_q)
    for h in range(MLA_HEADS):
        sl = slice(h * LANES, (h + 1) * LANES)
        q_lat = (_dot(wuk_ref[h], q_nope[sl]) * Q_SCALE).astype(BF16)
        q_rope = ((q_r[sl] * cos + q_s[sl] * sin) * Q_SCALE).astype(BF16)
        for c in range(tm // tb):
            cs = slice(c * tb, (c + 1) * tb)
            qt_ref[c, 2 * h * LANES:(2 * h + 1) * LANES, :] = q_lat[:, cs]
            qt_ref[c, (2 * h + 1) * LANES:(2 * h + 2) * LANES, :] = q_rope[:, cs]


def _swap_halves(w):
    half = w.shape[-1] // 2
    return jnp.concatenate([w[..., half:], w[..., :half]], axis=-1)


def _mla_in(x, pos, nw, w_in, q_norm, w_uq, kv_norm, w_uk, pending=None):
    t, d = x.shape
    tm = TOK_TILE
    tb = ATT_BLOCK
    nh = MLA_HEADS
    pad_r = LANES - MLA_ROPE
    w_kr = w_in[:, MLA_Q_LORA + MLA_KV_LORA:]
    wq = w_in[:, :MLA_Q_LORA].astype(BF16)
    wkv = jnp.pad(w_in[:, MLA_Q_LORA:], ((0, 0), (0, pad_r))).T.astype(BF16)
    wks = jnp.pad(_swap_halves(w_kr), ((0, 0), (0, pad_r))).T.astype(BF16)
    wqn = w_uq[:, :, :MLA_NOPE].reshape(MLA_Q_LORA, nh * MLA_NOPE).T.astype(BF16)
    w_r = w_uq[:, :, MLA_NOPE:]
    wqr = jnp.pad(w_r, ((0, 0), (0, 0), (0, pad_r))).reshape(MLA_Q_LORA, nh * LANES).T.astype(BF16)
    wqs = jnp.pad(_swap_halves(w_r), ((0, 0), (0, 0), (0, pad_r))).reshape(MLA_Q_LORA, nh * LANES).T.astype(BF16)
    wuk = w_uk.transpose(1, 0, 2).astype(BF16)
    inv_freq = 1.0 / (ROPE_THETA ** (jnp.arange(0, MLA_ROPE, 2, dtype=F32) / MLA_ROPE))
    full = lambda a: pl.BlockSpec(a.shape, lambda i: (0,) * a.ndim)
    args = (x, pos.reshape(1, t), nw.reshape(1, d), wq, wkv, wks, q_norm.reshape(1, -1), kv_norm.reshape(-1, 1),
            wqn, wqr, wqs, wuk, inv_freq.reshape(-1, 1))
    nsub = tm // tb
    nt = t // tm
    in_specs = [pl.BlockSpec((tm, d), lambda i: (i, 0)), pl.BlockSpec((1, tm), lambda i: (0, i))] \
        + [full(a) for a in args[2:]]
    out_specs = [pl.BlockSpec((nsub, 2 * nh * LANES, tb), lambda i: (i, 0, 0)),
                 pl.BlockSpec((tm, 2 * LANES), lambda i: (i, 0)),
                 pl.BlockSpec((tm // ATT_KV, VT_ROWS, ATT_KV), lambda i: (i, 0, 0))]
    out_shape = [jax.ShapeDtypeStruct((t // tb, 2 * nh * LANES, tb), BF16),
                 jax.ShapeDtypeStruct((t, 2 * LANES), BF16),
                 jax.ShapeDtypeStruct((t // ATT_KV, VT_ROWS, ATT_KV), BF16)]
    if pending is None:
        qt, k, vt = pl.pallas_call(
            _mla_in_kernel, grid=(nt,), in_specs=in_specs, out_specs=out_specs, out_shape=out_shape,
            compiler_params=_cparams("arbitrary"), name="mla_in",
        )(*args)
        return x, qt, k, vt
    ys, dest, gates = pending
    any_spec = pl.BlockSpec(memory_space=pl.ANY)
    return pl.pallas_call(
        _mla_in_combine_kernel,
        grid=(nt,),
        in_specs=[any_spec, any_spec, pl.BlockSpec((tm, 2), lambda i: (i, 0))] + in_specs,
        out_specs=[pl.BlockSpec((tm, d), lambda i: (i, 0))] + out_specs,
        out_shape=[jax.ShapeDtypeStruct((t, d), F32)] + out_shape,
        scratch_shapes=_combine_scratch(tm),
        compiler_params=_cparams("arbitrary"),
        name="mla_in_combine",
    )(dest.reshape(2, nt, tm), ys, gates.T, *args)


def _mla_attn_kernel(qt_ref, k_ref, vt_ref, wuv_ref, o_ref, *state):
    i = pl.program_id(1)
    tq = ATT_BLOCK
    tk = ATT_KV
    nh = MLA_HEADS
    dq = 2 * LANES
    s_refs, smax_refs, m_refs, acc_refs = state[0::4], state[1::4], state[2::4], state[3::4]
    last = (i * tq) // tk

    def scores(j, h):
        kb = k_ref[pl.ds(pl.multiple_of(j * tk, tk), tk), :]
        s = _dot(kb, qt_ref[0, h * dq:(h + 1) * dq, :])
        return s, jnp.max(s, axis=0, keepdims=True)

    def consume(j, h, s, s_max):
        m_old = m_refs[h][...]
        m_new = jnp.maximum(m_old, s_max)
        alpha = jnp.exp2(m_old - m_new)
        p = jnp.exp2(s - m_new).astype(BF16)
        acc_refs[h][...] = alpha * acc_refs[h][...] + _dot(vt_ref[j], p)
        m_refs[h][...] = m_new

    for h in range(nh):
        m_refs[h][...] = jnp.full_like(m_refs[h], -jnp.inf)
        acc_refs[h][...] = jnp.zeros_like(acc_refs[h])
        s_refs[h][...], smax_refs[h][...] = scores(0, h)

    def full_block(j, carry):
        for h in range(nh):
            s, s_max = s_refs[h][...], smax_refs[h][...]
            s_new, s_max_new = scores(j + 1, h)
            consume(j, h, s, s_max)
            s_refs[h][...], smax_refs[h][...] = s_new, s_max_new
        return carry

    lax.fori_loop(0, last, full_block, 0)
    kpos = lax.broadcasted_iota(I32, (tk, tq), 0) + last * tk
    qpos = lax.broadcasted_iota(I32, (tk, tq), 1) + i * tq
    for h in range(nh):
        s = jnp.where(kpos <= qpos, s_refs[h][...], -jnp.inf)
        consume(last, h, s, jnp.max(s, axis=0, keepdims=True))
    for h in range(nh):
        acc = acc_refs[h][...]
        o_lat = (acc[:MLA_KV_LORA] / acc[MLA_KV_LORA:MLA_KV_LORA + 1]).astype(BF16)
        o_ref[:, h * MLA_V:(h + 1) * MLA_V] = _dot_tn(o_lat, wuv_ref[h]).astype(BF16)


def _mla_attn(qt, k, vt, w_uv, bsz, seq):
    t = bsz * seq
    tq = ATT_BLOCK
    tk = ATT_KV
    nq = seq // tq
    nh = MLA_HEADS
    wuv = w_uv.transpose(1, 0, 2).astype(BF16)
    return pl.pallas_call(
        _mla_attn_kernel,
        grid=(bsz, nq),
        in_specs=[pl.BlockSpec((1, 2 * nh * LANES, tq), lambda b, i: (b * nq + i, 0, 0)),
                  pl.BlockSpec((seq, 2 * LANES), lambda b, i: (b, 0)),
                  pl.BlockSpec((seq // tk, VT_ROWS, tk), lambda b, i: (b, 0, 0)),
                  pl.BlockSpec(wuv.shape, lambda b, i: (0, 0, 0))],
        out_specs=pl.BlockSpec((tq, nh * MLA_V), lambda b, i: (b * nq + i, 0)),
        out_shape=jax.ShapeDtypeStruct((t, nh * MLA_V), BF16),
        scratch_shapes=[pltpu.VMEM((tk, tq), F32), pltpu.VMEM((1, tq), F32), pltpu.VMEM((1, tq), F32),
                        pltpu.VMEM((VT_ROWS, tq), F32)] * nh,
        compiler_params=_cparams("arbitrary", "arbitrary"),
        name="mla_attn",
    )(qt, k, vt, wuv)


def kernel(x, positions, attn_norm, ffn_norm, final_norm, gla_w_in, gla_w_gate, gla_b_gate, gla_out_norm, gla_w_o,
           mla_w_in, mla_q_norm, mla_w_uq, mla_kv_norm, mla_w_uk, mla_w_uv, mla_w_o,
           moe_w_group, moe_w_expert, moe_w1, moe_w3, moe_w2):
    bsz, seq, d = x.shape
    t = bsz * seq
    depth = attn_norm.shape[0]
    xf = x.reshape(t, d)
    pos = positions.reshape(t)
    pending = None
    for i in range(depth):
        j = i // 2
        if i % 2 == 0:
            if pending is not None:
                xf = _combine_plain(pending[0], pending[1], xf, pending[2])
            q, k, v, r, la = _gla_in(xf, attn_norm[i], gla_w_in[j], gla_w_gate[j], gla_b_gate[j])
            mixed = _gla_rec(q, k, v, r, la, gla_out_norm[j], bsz, seq)
            w_o = gla_w_o[j]
        else:
            xf, qt, kc, vt = _mla_in(xf, pos, attn_norm[i], mla_w_in[j], mla_q_norm[j], mla_w_uq[j],
                                     mla_kv_norm[j], mla_w_uk[j], pending)
            mixed = _mla_attn(qt, kc, vt, mla_w_uv[j], bsz, seq)
            w_o = mla_w_o[j]
        xf, h, ids, gates, counts = _post(mixed, w_o, xf, ffn_norm[i], moe_w_group[i], moe_w_expert[i])
        ys, dest = _moe(h, t, ids, counts, moe_w1, moe_w3, moe_w2, i)
        pending = (ys, dest, gates)
    out = _combine_norm(pending[0], pending[1], xf, pending[2], final_norm)
    return out.reshape(bsz, seq, d)
```

```python
import functools

import jax
import jax.numpy as jnp
from jax import lax
from jax.experimental import pallas as pl
from jax.experimental.pallas import tpu as pltpu

F32 = jnp.float32
BF16 = jnp.bfloat16
I32 = jnp.int32
HIGHEST = lax.Precision.HIGHEST

EPS = 1e-6
GLA_HEADS = 4
GLA_DK = 128
GLA_DV = 256
GLA_GATE_RANK = 16
GLA_GATE_TAU = 16.0
GLA_CHUNK = 64
MLA_HEADS = 8
MLA_NOPE = 128
MLA_ROPE = 64
MLA_V = 128
MLA_Q_LORA = 256
MLA_KV_LORA = 128
MLA_SCALE = (MLA_NOPE + MLA_ROPE) ** -0.5
Q_SCALE = MLA_SCALE * 1.4426950408889634
VT_ROWS = MLA_KV_LORA + 16
ROPE_THETA = 10000.0
MOE_GROUPS = 8
MOE_PER_GROUP = 8
MOE_EXPERTS = MOE_GROUPS * MOE_PER_GROUP
MOE_FF = 256

D_MODEL = 1024
LANES = 128
ROW_SUB = D_MODEL // LANES
VMEM_LIMIT = 48 * 1024 * 1024

TOK_TILE = 512
GLA_BLOCK = 256
ATT_BLOCK = 512
ATT_KV = 512
ROW_TILE = 256
EXP_BLOCK = 512


def _cparams(*sem):
    return pltpu.CompilerParams(dimension_semantics=sem, vmem_limit_bytes=VMEM_LIMIT)


def _rms(x, w):
    return x * lax.rsqrt(jnp.mean(x * x, axis=-1, keepdims=True) + EPS) * w


def _dot(a, b):
    return jnp.dot(a, b, preferred_element_type=F32)


def _dot_nt(a, b):
    return lax.dot_general(a, b, (((1,), (1,)), ((), ())), preferred_element_type=F32)


def _dot_tn(a, b):
    return lax.dot_general(a, b, (((0,), (0,)), ((), ())), preferred_element_type=F32)


def _split_bf16(x, pieces):
    out = []
    for _ in range(pieces - 1):
        hi = x.astype(BF16)
        out.append(hi)
        x = x - hi.astype(F32)
    out.append(x.astype(BF16))
    return out


def _gla_in_kernel(x_ref, nw_ref, wq_ref, wk_ref, wv_ref, wr_ref, wa_ref, wgh_ref, wgl_ref, bg_ref,
                   q_ref, k_ref, v_ref, r_ref, la_ref):
    hb = _rms(x_ref[...], nw_ref[...]).astype(BF16)
    a_hi, a_lo = _split_bf16(_dot(hb, wa_ref[...]), 2)
    z = _dot(a_hi, wgh_ref[...]) + _dot(a_hi, wgl_ref[...]) + _dot(a_lo, wgh_ref[...]) + bg_ref[...]
    log_sig = jnp.minimum(z, 0.0) - jnp.log1p(jnp.exp(-jnp.abs(z)))
    la_ref[...] = log_sig * (1.0 / GLA_GATE_TAU)
    q_ref[...] = _dot(hb, wq_ref[...]) * (GLA_DK ** -0.5)
    k_ref[...] = _dot(hb, wk_ref[...])
    v_ref[...] = _dot(hb, wv_ref[...]).astype(BF16)
    r_ref[...] = _dot(hb, wr_ref[...])


def _gla_in(x, nw, w_in, w_gate, b_gate):
    t, d = x.shape
    qk = GLA_HEADS * GLA_DK
    vd = GLA_HEADS * GLA_DV
    wq = w_in[:, :qk].astype(BF16)
    wk = w_in[:, qk:2 * qk].astype(BF16)
    wv = w_in[:, 2 * qk:2 * qk + vd].astype(BF16)
    wa = jnp.pad(w_in[:, 2 * qk + vd:2 * qk + vd + GLA_GATE_RANK], ((0, 0), (0, LANES - GLA_GATE_RANK))).astype(BF16)
    wr = w_in[:, 2 * qk + vd + GLA_GATE_RANK:].astype(BF16)
    wg = jnp.pad(w_gate, ((0, LANES - GLA_GATE_RANK), (0, 0)))
    wgh = wg.astype(BF16)
    wgl = (wg - wgh.astype(F32)).astype(BF16)
    tm = TOK_TILE
    row = lambda n: pl.BlockSpec((tm, n), lambda i: (i, 0))
    full = lambda a: pl.BlockSpec(a.shape, lambda i: (0, 0))
    nw2, bg2 = nw.reshape(1, d), b_gate.reshape(1, qk)
    return pl.pallas_call(
        _gla_in_kernel,
        grid=(t // tm,),
        in_specs=[row(d), full(nw2), full(wq), full(wk), full(wv), full(wr), full(wa), full(wgh), full(wgl),
                  full(bg2)],
        out_specs=[row(qk), row(qk), row(vd), row(vd), row(qk)],
        out_shape=[jax.ShapeDtypeStruct((t, qk), F32), jax.ShapeDtypeStruct((t, qk), F32),
                   jax.ShapeDtypeStruct((t, vd), BF16), jax.ShapeDtypeStruct((t, vd), F32),
                   jax.ShapeDtypeStruct((t, qk), F32)],
        compiler_params=_cparams("arbitrary"),
        name="gla_in",
    )(x, nw2, wq, wk, wv, wr, wa, wgh, wgl, bg2)


def _gla_rec_kernel(q_ref, k_ref, v_ref, r_ref, la_ref, onw_ref, o_ref, st_ref):
    c = GLA_CHUNK
    blk = GLA_BLOCK
    nc = blk // c

    @pl.when(pl.program_id(1) == 0)
    def _():
        st_ref[...] = jnp.zeros_like(st_ref)

    rows = lax.broadcasted_iota(I32, (blk, blk), 0)
    cols = lax.broadcasted_iota(I32, (blk, blk), 1)
    causal = (cols <= rows) & (cols // c == rows // c)
    tri = jnp.where(causal, 1.0, 0.0).astype(BF16)
    g = sum(_dot(tri, piece) for piece in _split_bf16(la_ref[...], 3))
    g_mid = jnp.concatenate(
        [jnp.broadcast_to(g[ci * c + c // 2:ci * c + c // 2 + 1], (c, g.shape[1])) for ci in range(nc)], axis=0)
    g_last = jnp.concatenate(
        [jnp.broadcast_to(g[ci * c + c - 1:ci * c + c], (c, g.shape[1])) for ci in range(nc)], axis=0)
    q = q_ref[...]
    k = k_ref[...]
    q_intra = (q * jnp.exp(g - g_mid)).astype(BF16)
    k_intra = (k * jnp.exp(g_mid - g)).astype(BF16)
    q_inter = (q * jnp.exp(g)).astype(BF16)
    k_state = (k * jnp.exp(g_last - g)).astype(BF16)
    onw = onw_ref[...]
    heads = range(GLA_HEADS)
    ksl = [slice(h * GLA_DK, (h + 1) * GLA_DK) for h in heads]
    vsl = [slice(h * GLA_DV, (h + 1) * GLA_DV) for h in heads]
    v = [v_ref[:, vsl[h]] for h in heads]
    a = [jnp.where(causal, _dot_nt(q_intra[:, ksl[h]], k_intra[:, ksl[h]]), 0.0).astype(BF16) for h in heads]
    kv = [[_dot_tn(v[h][ci * c:(ci + 1) * c], k_state[ci * c:(ci + 1) * c, ksl[h]]) for ci in range(nc)]
          for h in heads]
    o_intra = [_dot(a[h], v[h]) for h in heads]
    for h in heads:
        st = st_ref[h]
        o_inter = []
        for ci in range(nc):
            o_inter.append(_dot_nt(q_inter[ci * c:(ci + 1) * c, ksl[h]], st.astype(BF16)))
            st = st * jnp.exp(g[ci * c + c - 1:ci * c + c, ksl[h]]) + kv[h][ci]
        st_ref[h] = st
        o = _rms(o_intra[h] + jnp.concatenate(o_inter, axis=0), onw)
        r = r_ref[:, vsl[h]]
        o_ref[:, vsl[h]] = (o * (r * jax.nn.sigmoid(r))).astype(BF16)


def _gla_rec(q, k, v, r, la, onw, bsz, seq):
    t = bsz * seq
    qk = GLA_HEADS * GLA_DK
    vd = GLA_HEADS * GLA_DV
    nb = seq // GLA_BLOCK
    row = lambda n: pl.BlockSpec((GLA_BLOCK, n), lambda b, j: (b * nb + j, 0))
    onw2 = onw.reshape(1, GLA_DV)
    return pl.pallas_call(
        _gla_rec_kernel,
        grid=(bsz, nb),
        in_specs=[row(qk), row(qk), row(vd), row(vd), row(qk), pl.BlockSpec((1, GLA_DV), lambda b, j: (0, 0))],
        out_specs=row(vd),
        out_shape=jax.ShapeDtypeStruct((t, vd), BF16),
        scratch_shapes=[pltpu.VMEM((GLA_HEADS, GLA_DV, GLA_DK), F32)],
        compiler_params=_cparams("arbitrary", "arbitrary"),
        name="gla_rec",
    )(q, k, v, r, la, onw2)


def _store_rows(ref, val):
    n, d = val.shape
    ref[...] = val.reshape(n * d // LANES, LANES)


def _load_rows(ref, start, n, nc, dtype):
    return jnp.concatenate([ref[pl.ds(start * nc + c, n, stride=nc), :].astype(dtype) for c in range(nc)], axis=1)


def _post_kernel(a_ref, wo_ref, x_ref, nw_ref, wrh_ref, wrl_ref,
                 x1_ref, h_ref, ids_ref, gates_ref, cnt_ref, carry_ref):
    tm = a_ref.shape[0]
    ne = MOE_EXPERTS

    @pl.when(pl.program_id(0) == 0)
    def _():
        carry_ref[...] = jnp.zeros_like(carry_ref)

    x1 = x_ref[...] + _dot(a_ref[...], wo_ref[...])
    x1_ref[...] = x1
    h = _rms(x1, nw_ref[...])
    _store_rows(h_ref, h)
    h_hi, h_lo = _split_bf16(h, 2)
    lt = _dot_nt(wrh_ref[...], h_hi) + _dot_nt(wrh_ref[...], h_lo) + _dot_nt(wrl_ref[...], h_hi)
    gl = lt[0:MOE_GROUPS]
    gmax = jnp.max(gl, axis=0, keepdims=True)
    gi = lax.broadcasted_iota(I32, gl.shape, 0)
    g_sel = jnp.min(jnp.where(gl == gmax, gi, MOE_GROUPS), axis=0, keepdims=True)
    g_w = 1.0 / jnp.sum(jnp.exp(gl - gmax), axis=0, keepdims=True)
    el = lt[MOE_GROUPS:MOE_GROUPS + ne]
    ei = lax.broadcasted_iota(I32, el.shape, 0)
    in_group = (ei // MOE_PER_GROUP) == g_sel
    neg = jnp.float32(-jnp.inf)
    el1 = jnp.where(in_group, el, neg)
    l1 = jnp.max(el1, axis=0, keepdims=True)
    i1 = jnp.min(jnp.where(el1 == l1, ei, ne), axis=0, keepdims=True)
    el2 = jnp.where(ei == i1, neg, el1)
    l2 = jnp.max(el2, axis=0, keepdims=True)
    i2 = jnp.min(jnp.where(el2 == l2, ei, ne), axis=0, keepdims=True)
    e2 = jnp.exp(l2 - l1)
    gate1 = g_w / (1.0 + e2)
    gate2 = g_w * e2 / (1.0 + e2)
    oh1 = ei == i1
    oh2 = ei == i2
    both = jnp.where(oh1 | oh2, 1.0, 0.0)
    su = lax.broadcasted_iota(I32, (tm, tm), 0)
    tu = lax.broadcasted_iota(I32, (tm, tm), 1)
    upper = jnp.where(su < tu, 1.0, 0.0).astype(BF16)
    prefix = _dot(both.astype(BF16), upper) + carry_ref[:, 0:1]
    rank1 = jnp.sum(jnp.where(oh1, prefix, 0.0), axis=0, keepdims=True)
    rank2 = jnp.sum(jnp.where(oh2, prefix, 0.0), axis=0, keepdims=True)
    carry = carry_ref[...] + jnp.sum(both, axis=1, keepdims=True)
    carry_ref[...] = carry
    cnt_ref[...] = carry.astype(I32)
    ids_ref[...] = jnp.concatenate([i1, i2, rank1.astype(I32), rank2.astype(I32)], axis=0)
    gates_ref[...] = jnp.concatenate([gate1, gate2], axis=0)


def _post(a, wo, x, nw, w_group, w_expert):
    t, d = x.shape
    tm = TOK_TILE
    wr = jnp.concatenate([w_group.T, w_expert.T], axis=0)
    wr = jnp.pad(wr, ((0, LANES - wr.shape[0]), (0, 0)))
    wrh = wr.astype(BF16)
    wrl = (wr - wrh.astype(F32)).astype(BF16)
    wo = wo.astype(BF16)
    nw2 = nw.reshape(1, d)
    row = lambda n: pl.BlockSpec((tm, n), lambda i: (i, 0))
    full = lambda arr: pl.BlockSpec(arr.shape, lambda i: (0, 0))
    return pl.pallas_call(
        _post_kernel,
        grid=(t // tm,),
        in_specs=[row(a.shape[1]), full(wo), row(d), full(nw2), full(wrh), full(wrl)],
        out_specs=[row(d), pl.BlockSpec((tm * d // LANES, LANES), lambda i: (i, 0)),
                   pl.BlockSpec((4, tm), lambda i: (0, i)), pl.BlockSpec((2, tm), lambda i: (0, i)),
                   pl.BlockSpec((MOE_EXPERTS, LANES), lambda i: (0, 0))],
        out_shape=[jax.ShapeDtypeStruct((t, d), F32), jax.ShapeDtypeStruct((t * d // LANES, LANES), F32),
                   jax.ShapeDtypeStruct((4, t), I32), jax.ShapeDtypeStruct((2, t), F32),
                   jax.ShapeDtypeStruct((MOE_EXPERTS, LANES), I32)],
        scratch_shapes=[pltpu.VMEM((MOE_EXPERTS, LANES), F32)],
        compiler_params=_cparams("arbitrary"),
        name="post_router",
    )(a, wo, x, nw2, wrh, wrl)


def _row(ref, i):
    return ref.at[pl.ds(pl.multiple_of(i * ROW_SUB, ROW_SUB), ROW_SUB)]


def _zero_fill(zinfo_ref, zero_ref, xs_hbm, sem, wait):
    ne = MOE_EXPERTS
    nblk = xs_hbm.shape[0] // (EXP_BLOCK * ROW_SUB)

    def copy(off_rows, n_rows):
        off = pl.multiple_of(off_rows * ROW_SUB, ROW_SUB)
        cp = pltpu.make_async_copy(zero_ref.at[pl.ds(0, n_rows * ROW_SUB)],
                                   xs_hbm.at[pl.ds(off, n_rows * ROW_SUB)], sem)
        cp.wait() if wait else cp.start()

    def expert(e, carry):
        pad = zinfo_ref[ne + e]
        for b in range(EXP_BLOCK.bit_length() - 1):
            @pl.when(((pad >> b) & 1) == 1)
            def _():
                copy(zinfo_ref[e] + (pad & ((1 << b) - 1)), 1 << b)
        return carry

    def tail(kk, carry):
        blk = zinfo_ref[2 * ne] + kk

        @pl.when(blk < nblk)
        def _():
            copy(blk * EXP_BLOCK, EXP_BLOCK)
        return carry

    lax.fori_loop(0, ne, expert, 0)
    lax.fori_loop(0, ne + 1, tail, 0)


RING = 3
ISSUE_UNROLL = 8


def _slot_copies(dest_hbm, i, idx_ref, slot, sem):
    return [pltpu.make_async_copy(dest_hbm.at[a, i], idx_ref.at[slot, a], sem.at[slot]) for a in range(2)]


def _dispatch_kernel(zinfo_ref, dest_hbm, h_hbm, xs_hbm, idx_ref, hbuf_ref, zero_ref, sem_idx, sem_h, sem_rows):
    i = pl.program_id(0)
    nt = pl.num_programs(0)
    tr = ROW_TILE
    rows = tr * ROW_SUB

    def tile_copies(j):
        slot = j % RING
        src = h_hbm.at[pl.ds(pl.multiple_of(j * rows, rows), rows)]
        return [pltpu.make_async_copy(src, hbuf_ref.at[slot], sem_h.at[slot])] + \
            _slot_copies(dest_hbm, j, idx_ref, slot, sem_idx)

    def scatter_rows(slot):
        def body(t, carry):
            src = _row(hbuf_ref.at[slot], t)
            pltpu.make_async_copy(src, _row(xs_hbm, idx_ref[slot, 0, t]), sem_rows.at[slot]).start()
            pltpu.make_async_copy(src, _row(xs_hbm, idx_ref[slot, 1, t]), sem_rows.at[slot]).start()
            return carry
        lax.fori_loop(0, tr, body, 0, unroll=ISSUE_UNROLL)

    def wait_rows(slot):
        def body(t, carry):
            pltpu.make_async_copy(_row(hbuf_ref.at[0], 0), _row(xs_hbm, 0), sem_rows.at[slot]).wait()
            return carry
        lax.fori_loop(0, 2 * tr, body, 0, unroll=ISSUE_UNROLL)

    @pl.when(i == 0)
    def _():
        for cp in tile_copies(0):
            cp.start()
        zero_ref[...] = jnp.zeros_like(zero_ref)
        _zero_fill(zinfo_ref, zero_ref, xs_hbm, sem_rows.at[1], wait=False)
        _zero_fill(zinfo_ref, zero_ref, xs_hbm, sem_rows.at[1], wait=True)

    @pl.when(i + 1 < nt)
    def _():
        for cp in tile_copies(i + 1):
            cp.start()

    for cp in tile_copies(i):
        cp.wait()

    for slot in range(RING):
        @pl.when(i % RING == slot)
        def _():
            scatter_rows(slot)

            @pl.when(i > 0)
            def _():
                wait_rows((slot - 1) % RING)

            @pl.when(i == nt - 1)
            def _():
                wait_rows(slot)


def _dispatch(h, dest, zoff, n_slots):
    t = h.shape[0] // ROW_SUB
    tr = ROW_TILE
    nt = t // tr
    grid_spec = pltpu.PrefetchScalarGridSpec(
        num_scalar_prefetch=1,
        grid=(nt,),
        in_specs=[pl.BlockSpec(memory_space=pl.ANY), pl.BlockSpec(memory_space=pl.ANY)],
        out_specs=pl.BlockSpec(memory_space=pl.ANY),
        scratch_shapes=[pltpu.SMEM((RING, 2, tr), I32), pltpu.VMEM((RING, tr * ROW_SUB, LANES), F32),
                        pltpu.VMEM((EXP_BLOCK * ROW_SUB, LANES), F32),
                        pltpu.SemaphoreType.DMA((RING,)), pltpu.SemaphoreType.DMA((RING,)),
                        pltpu.SemaphoreType.DMA((RING,))],
    )
    return pl.pallas_call(
        _dispatch_kernel,
        grid_spec=grid_spec,
        out_shape=jax.ShapeDtypeStruct((n_slots * ROW_SUB, LANES), F32),
        compiler_params=_cparams("arbitrary"),
        name="moe_dispatch",
    )(zoff, dest.reshape(2, nt, tr), h)


def _expert_kernel(be_ref, nused_ref, xs_ref, w1_ref, w3_ref, w2_ref, ys_ref, w1b_ref, w3b_ref, w2b_ref):
    i = pl.program_id(0)
    used = i < nused_ref[0]

    @pl.when(used & ((i == 0) | (be_ref[i] != be_ref[jnp.maximum(i - 1, 0)])))
    def _():
        w1b_ref[...] = w1_ref[0, 0].astype(BF16)
        w3b_ref[...] = w3_ref[0, 0].astype(BF16)
        w2b_ref[...] = w2_ref[0, 0].astype(BF16)

    @pl.when(used)
    def _():
        xb = _load_rows(xs_ref, 0, EXP_BLOCK, ROW_SUB, BF16)
        a = _dot(xb, w1b_ref[...])
        b = _dot(xb, w3b_ref[...])
        hid = (a * jax.nn.sigmoid(a) * b).astype(BF16)
        _store_rows(ys_ref, _dot(hid, w2b_ref[...]))

    @pl.when(i >= nused_ref[0])
    def _():
        ys_ref[...] = jnp.zeros_like(ys_ref)


def _experts(xs, block_e, n_used, w1, w3, w2, layer):
    bm = EXP_BLOCK
    n_slots = xs.shape[0] // ROW_SUB
    nblk = n_slots // bm
    d, ff = w1.shape[-2:]

    def last_used(i, nu):
        return jnp.maximum(jnp.minimum(i, nu[0] - 1), 0)

    def xmap(i, be, nu):
        return (last_used(i, nu), 0)

    def wmap(i, be, nu):
        return (layer, be[last_used(i, nu)], 0, 0)

    grid_spec = pltpu.PrefetchScalarGridSpec(
        num_scalar_prefetch=2,
        grid=(nblk,),
        in_specs=[pl.BlockSpec((bm * ROW_SUB, LANES), xmap), pl.BlockSpec((1, 1, d, ff), wmap),
                  pl.BlockSpec((1, 1, d, ff), wmap), pl.BlockSpec((1, 1, ff, d), wmap)],
        out_specs=pl.BlockSpec((bm * ROW_SUB, LANES), lambda i, be, nu: (i, 0)),
        scratch_shapes=[pltpu.VMEM((d, ff), BF16), pltpu.VMEM((d, ff), BF16), pltpu.VMEM((ff, d), BF16)],
    )
    return pl.pallas_call(
        _expert_kernel,
        grid_spec=grid_spec,
        out_shape=jax.ShapeDtypeStruct((n_slots * ROW_SUB, LANES), F32),
        compiler_params=_cparams("arbitrary"),
        name="moe_experts",
    )(block_e, n_used, xs, w1, w3, w2)


def _combine_tiles(dest_hbm, ys_hbm, x_ref, g_ref, o_ref, idx_ref, buf_ref, sem_idx, sem_rows, finish,
                   make_stages=None):
    i = pl.program_id(0)
    nt = pl.num_programs(0)
    tr = x_ref.shape[0]

    def gather(slot, lo=0, hi=None):
        buf = buf_ref.at[slot]

        def body(t, carry):
            pltpu.make_async_copy(_row(ys_hbm, idx_ref[slot, 0, t]), _row(buf, t), sem_rows.at[slot]).start()
            pltpu.make_async_copy(_row(ys_hbm, idx_ref[slot, 1, t]), _row(buf, tr + t), sem_rows.at[slot]).start()
            return carry

        lax.fori_loop(lo, tr if hi is None else hi, body, 0, unroll=ISSUE_UNROLL)

    def wait_rows(slot):
        def body(t, carry):
            pltpu.make_async_copy(_row(ys_hbm, 0), _row(buf_ref.at[0], 0), sem_rows.at[slot]).wait()
            return carry
        lax.fori_loop(0, 2 * tr, body, 0, unroll=ISSUE_UNROLL)

    @pl.when(i == 0)
    def _():
        for cp in _slot_copies(dest_hbm, 0, idx_ref, 0, sem_idx):
            cp.start()
        for cp in _slot_copies(dest_hbm, 0, idx_ref, 0, sem_idx):
            cp.wait()
        gather(0)

        @pl.when(nt > 1)
        def _():
            for cp in _slot_copies(dest_hbm, 1, idx_ref, 1, sem_idx):
                cp.start()

    def consume(slot):
        wait_rows(slot)
        g = g_ref[...]
        y1 = _load_rows(buf_ref.at[slot], 0, tr, ROW_SUB, F32)
        y2 = _load_rows(buf_ref.at[slot], tr, tr, ROW_SUB, F32)
        o_ref[...] = finish(x_ref[...] + g[:, 0:1] * y1 + g[:, 1:2] * y2)

    def next_slots_ready(nxt):
        @pl.when(i + 1 < nt)
        def _():
            for cp in _slot_copies(dest_hbm, i + 1, idx_ref, nxt, sem_idx):
                cp.wait()

    def prefetch_slots(slot):
        @pl.when(i + 2 < nt)
        def _():
            for cp in _slot_copies(dest_hbm, i + 2, idx_ref, slot, sem_idx):
                cp.start()

    for slot in range(2):
        @pl.when(i % 2 == slot)
        def _():
            nxt = 1 - slot
            next_slots_ready(nxt)
            if make_stages is None:
                @pl.when(i + 1 < nt)
                def _():
                    gather(nxt)
                prefetch_slots(slot)
                consume(slot)
                return
            consume(slot)
            stages = make_stages()
            per = tr // len(stages)
            for n, stage in enumerate(stages):
                @pl.when(i + 1 < nt)
                def _():
                    gather(nxt, n * per, (n + 1) * per)
                stage()
            prefetch_slots(slot)


def _combine_scratch(tr):
    return [pltpu.SMEM((2, 2, tr), I32), pltpu.VMEM((2, 2 * tr * ROW_SUB, LANES), F32),
            pltpu.SemaphoreType.DMA((2,)), pltpu.SemaphoreType.DMA((2,))]


def _combine_norm_kernel(dest_hbm, ys_hbm, x_ref, g_ref, nw_ref, o_ref, *scratch):
    _combine_tiles(dest_hbm, ys_hbm, x_ref, g_ref, o_ref, *scratch, finish=lambda v: _rms(v, nw_ref[...]))


def _combine_norm(ys, dest, x, gates, nw):
    t, d = x.shape
    tr = ROW_TILE
    nt = t // tr
    return pl.pallas_call(
        _combine_norm_kernel,
        grid=(nt,),
        in_specs=[pl.BlockSpec(memory_space=pl.ANY), pl.BlockSpec(memory_space=pl.ANY),
                  pl.BlockSpec((tr, d), lambda i: (i, 0)), pl.BlockSpec((tr, 2), lambda i: (i, 0)),
                  pl.BlockSpec((1, d), lambda i: (0, 0))],
        out_specs=pl.BlockSpec((tr, d), lambda i: (i, 0)),
        out_shape=jax.ShapeDtypeStruct((t, d), F32),
        scratch_shapes=_combine_scratch(tr),
        compiler_params=_cparams("arbitrary"),
        name="moe_combine",
    )(dest.reshape(2, nt, tr), ys, x, gates.T, nw.reshape(1, d))


def _slots_kernel(ids_ref, pst_ref, dest_ref):
    ids = ids_ref[...]
    ei = lax.broadcasted_iota(I32, (MOE_EXPERTS, ids.shape[1]), 0)
    pst = pst_ref[...]
    d1 = jnp.sum(jnp.where(ei == ids[0:1], pst, 0), axis=0, keepdims=True) + ids[2:3]
    d2 = jnp.sum(jnp.where(ei == ids[1:2], pst, 0), axis=0, keepdims=True) + ids[3:4]
    dest_ref[...] = jnp.concatenate([d1, d2], axis=0)


def _slots(ids, pstarts):
    t = ids.shape[1]
    tm = TOK_TILE
    return pl.pallas_call(
        _slots_kernel,
        grid=(t // tm,),
        in_specs=[pl.BlockSpec((4, tm), lambda i: (0, i)), pl.BlockSpec((MOE_EXPERTS, 1), lambda i: (0, 0))],
        out_specs=pl.BlockSpec((2, tm), lambda i: (0, i)),
        out_shape=jax.ShapeDtypeStruct((2, t), I32),
        compiler_params=_cparams("arbitrary"),
        name="moe_slots",
    )(ids, pstarts.reshape(MOE_EXPERTS, 1))


def _combine_plain_kernel(dest_hbm, ys_hbm, x_ref, g_ref, o_ref, *scratch):
    _combine_tiles(dest_hbm, ys_hbm, x_ref, g_ref, o_ref, *scratch, finish=lambda v: v)


def _combine_plain(ys, dest, x, gates):
    t, d = x.shape
    tr = ROW_TILE
    nt = t // tr
    return pl.pallas_call(
        _combine_plain_kernel,
        grid=(nt,),
        in_specs=[pl.BlockSpec(memory_space=pl.ANY), pl.BlockSpec(memory_space=pl.ANY),
                  pl.BlockSpec((tr, d), lambda i: (i, 0)), pl.BlockSpec((tr, 2), lambda i: (i, 0))],
        out_specs=pl.BlockSpec((tr, d), lambda i: (i, 0)),
        out_shape=jax.ShapeDtypeStruct((t, d), F32),
        scratch_shapes=_combine_scratch(tr),
        compiler_params=_cparams("arbitrary"),
        name="moe_combine",
    )(dest.reshape(2, nt, tr), ys, x, gates.T)


def _moe(h, t, ids, counts, w1, w3, w2, layer):
    bm = EXP_BLOCK
    n_asg = 2 * t
    n_slots = n_asg + MOE_EXPERTS * bm
    cnt = counts[:, 0]
    padded = (cnt + bm - 1) // bm * bm
    pend = jnp.cumsum(padded)
    pstarts = (pend - padded).astype(I32)
    dest = _slots(ids, pstarts)
    nblk = n_slots // bm
    block_start = jnp.arange(nblk, dtype=I32) * bm
    block_e = jnp.minimum(jnp.sum(pend[None, :] <= block_start[:, None], axis=1), MOE_EXPERTS - 1).astype(I32)
    n_used = (pend[-1:] // bm).astype(I32)
    zinfo = jnp.concatenate([pstarts + cnt, padded - cnt, n_used]).astype(I32)
    xs = _dispatch(h, dest, zinfo, n_slots)
    ys = _experts(xs, block_e, n_used, w1, w3, w2, layer)
    return ys, dest


def _mla_in_combine_kernel(dest_hbm, ys_hbm, g_ref, x_ref, *rest):
    ins, (xc_ref, qt_ref, k_ref, vt_ref), scratch = rest[:12], rest[12:16], rest[16:]
    _combine_tiles(dest_hbm, ys_hbm, x_ref, g_ref, xc_ref, *scratch, finish=lambda v: v,
                   make_stages=lambda: _mla_in_stages(xc_ref, *ins, qt_ref, k_ref, vt_ref))


def _mla_in_kernel(*refs):
    for stage in _mla_in_stages(*refs):
        stage()


def _mla_in_stages(x_ref, pos_ref, nw_ref, wq_ref, wkv_ref, wks_ref, qn_ref, kvn_ref, wqn_ref, wqr_ref, wqs_ref,
                   wuk_ref, freq_ref, qt_ref, k_ref, vt_ref):
    tm = x_ref.shape[0]
    tb = ATT_BLOCK
    v = {}

    def latents():
        hb = _rms(x_ref[...], nw_ref[...]).astype(BF16)
        v["c_q"] = _rms(_dot(hb, wq_ref[...]), qn_ref[...]).astype(BF16)
        v["kv_t"] = _dot_nt(wkv_ref[...], hb)
        v["ks_t"] = _dot_nt(wks_ref[...], hb)

    def rotary_tables():
        ang = freq_ref[...] * pos_ref[...].astype(F32)
        cos32, sin32 = jnp.cos(ang), jnp.sin(ang)
        v["cos"] = jnp.concatenate([cos32] * 4, axis=0)
        v["sin"] = jnp.concatenate([-sin32, sin32] * 2, axis=0)

    def keys_values():
        c_kv = v["kv_t"][:MLA_KV_LORA]
        c_kv = c_kv * lax.rsqrt(jnp.mean(c_kv * c_kv, axis=0, keepdims=True) + EPS) * kvn_ref[...]
        k_rope = v["kv_t"][MLA_KV_LORA:] * v["cos"] + v["ks_t"] * v["sin"]
        k_ref[...] = jnp.concatenate([c_kv, k_rope], axis=0).T.astype(BF16)
        v_ext = jnp.concatenate([c_kv, jnp.ones((VT_ROWS - MLA_KV_LORA, tm), F32)], axis=0).astype(BF16)
        for c in range(tm // ATT_KV):
            vt_ref[c] = v_ext[:, c * ATT_KV:(c + 1) * ATT_KV]

    def queries_up():
        v["q_nope"] = _dot_nt(wqn_ref[...], v["c_q"]).astype(BF16)
        v["q_r"] = _dot_nt(wqr_ref[...], v["c_q"])
        v["q_s"] = _dot_nt(wqs_ref[...], v["c_q"])

    def head(h):
        def run():
            sl = slice(h * LANES, (h + 1) * LANES)
            q_lat = (_dot(wuk_ref[h], v["q_nope"][sl]) * Q_SCALE).astype(BF16)
            q_rope = ((v["q_r"][sl] * v["cos"] + v["q_s"][sl] * v["sin"]) * Q_SCALE).astype(BF16)
            for c in range(tm // tb):
                cs = slice(c * tb, (c + 1) * tb)
                qt_ref[c, 2 * h * LANES:(2 * h + 1) * LANES, :] = q_lat[:, cs]
                qt_ref[c, (2 * h + 1) * LANES:(2 * h + 2) * LANES, :] = q_rope[:, cs]
        return run

    def pair(a, b):
        def run():
            a()
            b()
        return run

    return [latents, rotary_tables, keys_values, queries_up] + [pair(head(h), head(h + 1))
                                                                for h in range(0, MLA_HEADS, 2)]


def _swap_halves(w):
    half = w.shape[-1] // 2
    return jnp.concatenate([w[..., half:], w[..., :half]], axis=-1)


def _mla_in(x, pos, nw, w_in, q_norm, w_uq, kv_norm, w_uk, pending=None):
    t, d = x.shape
    tm = TOK_TILE
    tb = ATT_BLOCK
    nh = MLA_HEADS
    pad_r = LANES - MLA_ROPE
    w_kr = w_in[:, MLA_Q_LORA + MLA_KV_LORA:]
    wq = w_in[:, :MLA_Q_LORA].astype(BF16)
    wkv = jnp.pad(w_in[:, MLA_Q_LORA:], ((0, 0), (0, pad_r))).T.astype(BF16)
    wks = jnp.pad(_swap_halves(w_kr), ((0, 0), (0, pad_r))).T.astype(BF16)
    wqn = w_uq[:, :, :MLA_NOPE].reshape(MLA_Q_LORA, nh * MLA_NOPE).T.astype(BF16)
    w_r = w_uq[:, :, MLA_NOPE:]
    wqr = jnp.pad(w_r, ((0, 0), (0, 0), (0, pad_r))).reshape(MLA_Q_LORA, nh * LANES).T.astype(BF16)
    wqs = jnp.pad(_swap_halves(w_r), ((0, 0), (0, 0), (0, pad_r))).reshape(MLA_Q_LORA, nh * LANES).T.astype(BF16)
    wuk = w_uk.transpose(1, 0, 2).astype(BF16)
    inv_freq = 1.0 / (ROPE_THETA ** (jnp.arange(0, MLA_ROPE, 2, dtype=F32) / MLA_ROPE))
    full = lambda a: pl.BlockSpec(a.shape, lambda i: (0,) * a.ndim)
    args = (x, pos.reshape(1, t), nw.reshape(1, d), wq, wkv, wks, q_norm.reshape(1, -1), kv_norm.reshape(-1, 1),
            wqn, wqr, wqs, wuk, inv_freq.reshape(-1, 1))
    nsub = tm // tb
    nt = t // tm
    in_specs = [pl.BlockSpec((tm, d), lambda i: (i, 0)), pl.BlockSpec((1, tm), lambda i: (0, i))] \
        + [full(a) for a in args[2:]]
    out_specs = [pl.BlockSpec((nsub, 2 * nh * LANES, tb), lambda i: (i, 0, 0)),
                 pl.BlockSpec((tm, 2 * LANES), lambda i: (i, 0)),
                 pl.BlockSpec((tm // ATT_KV, VT_ROWS, ATT_KV), lambda i: (i, 0, 0))]
    out_shape = [jax.ShapeDtypeStruct((t // tb, 2 * nh * LANES, tb), BF16),
                 jax.ShapeDtypeStruct((t, 2 * LANES), BF16),
                 jax.ShapeDtypeStruct((t // ATT_KV, VT_ROWS, ATT_KV), BF16)]
    if pending is None:
        qt, k, vt = pl.pallas_call(
            _mla_in_kernel, grid=(nt,), in_specs=in_specs, out_specs=out_specs, out_shape=out_shape,
            compiler_params=_cparams("arbitrary"), name="mla_in",
        )(*args)
        return x, qt, k, vt
    ys, dest, gates = pending
    any_spec = pl.BlockSpec(memory_space=pl.ANY)
    return pl.pallas_call(
        _mla_in_combine_kernel,
        grid=(nt,),
        in_specs=[any_spec, any_spec, pl.BlockSpec((tm, 2), lambda i: (i, 0))] + in_specs,
        out_specs=[pl.BlockSpec((tm, d), lambda i: (i, 0))] + out_specs,
        out_shape=[jax.ShapeDtypeStruct((t, d), F32)] + out_shape,
        scratch_shapes=_combine_scratch(tm),
        compiler_params=_cparams("arbitrary"),
        name="mla_in_combine",
    )(dest.reshape(2, nt, tm), ys, gates.T, *args)


def _mla_attn_kernel(qt_ref, k_ref, vt_ref, wuv_ref, o_ref, *state):
    i = pl.program_id(1)
    tq = ATT_BLOCK
    tk = ATT_KV
    nh = MLA_HEADS
    dq = 2 * LANES
    s_refs, smax_refs, m_refs, acc_refs = state[0::4], state[1::4], state[2::4], state[3::4]
    last = (i * tq) // tk

    def scores(j, h):
        kb = k_ref[pl.ds(pl.multiple_of(j * tk, tk), tk), :]
        s = _dot(kb, qt_ref[0, h * dq:(h + 1) * dq, :])
        return s, jnp.max(s, axis=0, keepdims=True)

    def consume(j, h, s, s_max):
        m_old = m_refs[h][...]
        m_new = jnp.maximum(m_old, s_max)
        alpha = jnp.exp2(m_old - m_new)
        p = jnp.exp2(s - m_new).astype(BF16)
        acc_refs[h][...] = alpha * acc_refs[h][...] + _dot(vt_ref[j], p)
        m_refs[h][...] = m_new

    for h in range(nh):
        m_refs[h][...] = jnp.full_like(m_refs[h], -jnp.inf)
        acc_refs[h][...] = jnp.zeros_like(acc_refs[h])
        s_refs[h][...], smax_refs[h][...] = scores(0, h)

    def full_block(j, carry):
        for h in range(nh):
            s, s_max = s_refs[h][...], smax_refs[h][...]
            s_new, s_max_new = scores(j + 1, h)
            consume(j, h, s, s_max)
            s_refs[h][...], smax_refs[h][...] = s_new, s_max_new
        return carry

    lax.fori_loop(0, last, full_block, 0)
    kpos = lax.broadcasted_iota(I32, (tk, tq), 0) + last * tk
    qpos = lax.broadcasted_iota(I32, (tk, tq), 1) + i * tq
    for h in range(nh):
        s = jnp.where(kpos <= qpos, s_refs[h][...], -jnp.inf)
        consume(last, h, s, jnp.max(s, axis=0, keepdims=True))
    for h in range(nh):
        acc = acc_refs[h][...]
        o_lat = (acc[:MLA_KV_LORA] / acc[MLA_KV_LORA:MLA_KV_LORA + 1]).astype(BF16)
        o_ref[:, h * MLA_V:(h + 1) * MLA_V] = _dot_tn(o_lat, wuv_ref[h]).astype(BF16)


def _mla_attn(qt, k, vt, w_uv, bsz, seq):
    t = bsz * seq
    tq = ATT_BLOCK
    tk = ATT_KV
    nq = seq // tq
    nh = MLA_HEADS
    wuv = w_uv.transpose(1, 0, 2).astype(BF16)
    return pl.pallas_call(
        _mla_attn_kernel,
        grid=(bsz, nq),
        in_specs=[pl.BlockSpec((1, 2 * nh * LANES, tq), lambda b, i: (b * nq + i, 0, 0)),
                  pl.BlockSpec((seq, 2 * LANES), lambda b, i: (b, 0)),
                  pl.BlockSpec((seq // tk, VT_ROWS, tk), lambda b, i: (b, 0, 0)),
                  pl.BlockSpec(wuv.shape, lambda b, i: (0, 0, 0))],
        out_specs=pl.BlockSpec((tq, nh * MLA_V), lambda b, i: (b * nq + i, 0)),
        out_shape=jax.ShapeDtypeStruct((t, nh * MLA_V), BF16),
        scratch_shapes=[pltpu.VMEM((tk, tq), F32), pltpu.VMEM((1, tq), F32), pltpu.VMEM((1, tq), F32),
                        pltpu.VMEM((VT_ROWS, tq), F32)] * nh,
        compiler_params=_cparams("arbitrary", "arbitrary"),
        name="mla_attn",
    )(qt, k, vt, wuv)


def kernel(x, positions, attn_norm, ffn_norm, final_norm, gla_w_in, gla_w_gate, gla_b_gate, gla_out_norm, gla_w_o,
           mla_w_in, mla_q_norm, mla_w_uq, mla_kv_norm, mla_w_uk, mla_w_uv, mla_w_o,
           moe_w_group, moe_w_expert, moe_w1, moe_w3, moe_w2):
    bsz, seq, d = x.shape
    t = bsz * seq
    depth = attn_norm.shape[0]
    xf = x.reshape(t, d)
    pos = positions.reshape(t)
    pending = None
    for i in range(depth):
        j = i // 2
        if i % 2 == 0:
            if pending is not None:
                xf = _combine_plain(pending[0], pending[1], xf, pending[2])
            q, k, v, r, la = _gla_in(xf, attn_norm[i], gla_w_in[j], gla_w_gate[j], gla_b_gate[j])
            mixed = _gla_rec(q, k, v, r, la, gla_out_norm[j], bsz, seq)
            w_o = gla_w_o[j]
        else:
            xf, qt, kc, vt = _mla_in(xf, pos, attn_norm[i], mla_w_in[j], mla_q_norm[j], mla_w_uq[j],
                                     mla_kv_norm[j], mla_w_uk[j], pending)
            mixed = _mla_attn(qt, kc, vt, mla_w_uv[j], bsz, seq)
            w_o = mla_w_o[j]
        xf, h, ids, gates, counts = _post(mixed, w_o, xf, ffn_norm[i], moe_w_group[i], moe_w_expert[i])
        ys, dest = _moe(h, t, ids, counts, moe_w1, moe_w3, moe_w2, i)
        pending = (ys, dest, gates)
    out = _combine_norm(pending[0], pending[1], xf, pending[2], final_norm)
    return out.reshape(bsz, seq, d)
```

```python
import functools

import jax
import jax.numpy as jnp
from jax import lax
from jax.experimental import pallas as pl
from jax.experimental.pallas import tpu as pltpu

F32 = jnp.float32
BF16 = jnp.bfloat16
I32 = jnp.int32
HIGHEST = lax.Precision.HIGHEST

EPS = 1e-6
GLA_HEADS = 4
GLA_DK = 128
GLA_DV = 256
GLA_GATE_RANK = 16
GLA_GATE_TAU = 16.0
GLA_CHUNK = 64
MLA_HEADS = 8
MLA_NOPE = 128
MLA_ROPE = 64
MLA_V = 128
MLA_Q_LORA = 256
MLA_KV_LORA = 128
MLA_SCALE = (MLA_NOPE + MLA_ROPE) ** -0.5
Q_SCALE = MLA_SCALE * 1.4426950408889634
VT_ROWS = MLA_KV_LORA + 16
ROPE_THETA = 10000.0
MOE_GROUPS = 8
MOE_PER_GROUP = 8
MOE_EXPERTS = MOE_GROUPS * MOE_PER_GROUP
MOE_FF = 256

D_MODEL = 1024
LANES = 128
ROW_SUB = D_MODEL // LANES
VMEM_LIMIT = 48 * 1024 * 1024

TOK_TILE = 512
GLA_BLOCK = 256
ATT_BLOCK = 512
ATT_KV = 512
ROW_TILE = 256
EXP_BLOCK = 512


def _cparams(*sem):
    return pltpu.CompilerParams(dimension_semantics=sem, vmem_limit_bytes=VMEM_LIMIT)


def _rms(x, w):
    return x * lax.rsqrt(jnp.mean(x * x, axis=-1, keepdims=True) + EPS) * w


def _dot(a, b):
    return jnp.dot(a, b, preferred_element_type=F32)


def _dot_nt(a, b):
    return lax.dot_general(a, b, (((1,), (1,)), ((), ())), preferred_element_type=F32)


def _dot_tn(a, b):
    return lax.dot_general(a, b, (((0,), (0,)), ((), ())), preferred_element_type=F32)


def _split_bf16(x, pieces):
    out = []
    for _ in range(pieces - 1):
        hi = x.astype(BF16)
        out.append(hi)
        x = x - hi.astype(F32)
    out.append(x.astype(BF16))
    return out


def _gla_in_kernel(x_ref, nw_ref, wq_ref, wk_ref, wv_ref, wr_ref, wa_ref, wgh_ref, wgl_ref, bg_ref,
                   q_ref, k_ref, v_ref, r_ref, la_ref):
    hb = _rms(x_ref[...], nw_ref[...]).astype(BF16)
    a_hi, a_lo = _split_bf16(_dot(hb, wa_ref[...]), 2)
    z = _dot(a_hi, wgh_ref[...]) + _dot(a_hi, wgl_ref[...]) + _dot(a_lo, wgh_ref[...]) + bg_ref[...]
    log_sig = jnp.minimum(z, 0.0) - jnp.log1p(jnp.exp(-jnp.abs(z)))
    la_ref[...] = log_sig * (1.0 / GLA_GATE_TAU)
    q_ref[...] = _dot(hb, wq_ref[...]) * (GLA_DK ** -0.5)
    k_ref[...] = _dot(hb, wk_ref[...])
    v_ref[...] = _dot(hb, wv_ref[...]).astype(BF16)
    r_ref[...] = _dot(hb, wr_ref[...])


def _gla_in(x, nw, w_in, w_gate, b_gate):
    t, d = x.shape
    qk = GLA_HEADS * GLA_DK
    vd = GLA_HEADS * GLA_DV
    wq = w_in[:, :qk].astype(BF16)
    wk = w_in[:, qk:2 * qk].astype(BF16)
    wv = w_in[:, 2 * qk:2 * qk + vd].astype(BF16)
    wa = jnp.pad(w_in[:, 2 * qk + vd:2 * qk + vd + GLA_GATE_RANK], ((0, 0), (0, LANES - GLA_GATE_RANK))).astype(BF16)
    wr = w_in[:, 2 * qk + vd + GLA_GATE_RANK:].astype(BF16)
    wg = jnp.pad(w_gate, ((0, LANES - GLA_GATE_RANK), (0, 0)))
    wgh = wg.astype(BF16)
    wgl = (wg - wgh.astype(F32)).astype(BF16)
    tm = TOK_TILE
    row = lambda n: pl.BlockSpec((tm, n), lambda i: (i, 0))
    full = lambda a: pl.BlockSpec(a.shape, lambda i: (0, 0))
    nw2, bg2 = nw.reshape(1, d), b_gate.reshape(1, qk)
    return pl.pallas_call(
        _gla_in_kernel,
        grid=(t // tm,),
        in_specs=[row(d), full(nw2), full(wq), full(wk), full(wv), full(wr), full(wa), full(wgh), full(wgl),
                  full(bg2)],
        out_specs=[row(qk), row(qk), row(vd), row(vd), row(qk)],
        out_shape=[jax.ShapeDtypeStruct((t, qk), F32), jax.ShapeDtypeStruct((t, qk), F32),
                   jax.ShapeDtypeStruct((t, vd), BF16), jax.ShapeDtypeStruct((t, vd), F32),
                   jax.ShapeDtypeStruct((t, qk), F32)],
        compiler_params=_cparams("arbitrary"),
        name="gla_in",
    )(x, nw2, wq, wk, wv, wr, wa, wgh, wgl, bg2)


def _gla_rec_kernel(q_ref, k_ref, v_ref, r_ref, la_ref, onw_ref, o_ref, st_ref):
    c = GLA_CHUNK
    blk = GLA_BLOCK
    nc = blk // c

    @pl.when(pl.program_id(1) == 0)
    def _():
        st_ref[...] = jnp.zeros_like(st_ref)

    rows = lax.broadcasted_iota(I32, (blk, blk), 0)
    cols = lax.broadcasted_iota(I32, (blk, blk), 1)
    causal = (cols <= rows) & (cols // c == rows // c)
    tri = jnp.where(causal, 1.0, 0.0).astype(BF16)
    g = sum(_dot(tri, piece) for piece in _split_bf16(la_ref[...], 3))
    g_mid = jnp.concatenate(
        [jnp.broadcast_to(g[ci * c + c // 2:ci * c + c // 2 + 1], (c, g.shape[1])) for ci in range(nc)], axis=0)
    g_last = jnp.concatenate(
        [jnp.broadcast_to(g[ci * c + c - 1:ci * c + c], (c, g.shape[1])) for ci in range(nc)], axis=0)
    q = q_ref[...]
    k = k_ref[...]
    q_intra = (q * jnp.exp(g - g_mid)).astype(BF16)
    k_intra = (k * jnp.exp(g_mid - g)).astype(BF16)
    q_inter = (q * jnp.exp(g)).astype(BF16)
    k_state = (k * jnp.exp(g_last - g)).astype(BF16)
    onw = onw_ref[...]
    heads = range(GLA_HEADS)
    ksl = [slice(h * GLA_DK, (h + 1) * GLA_DK) for h in heads]
    vsl = [slice(h * GLA_DV, (h + 1) * GLA_DV) for h in heads]
    v = [v_ref[:, vsl[h]] for h in heads]
    a = [jnp.where(causal, _dot_nt(q_intra[:, ksl[h]], k_intra[:, ksl[h]]), 0.0).astype(BF16) for h in heads]
    kv = [[_dot_tn(v[h][ci * c:(ci + 1) * c], k_state[ci * c:(ci + 1) * c, ksl[h]]) for ci in range(nc)]
          for h in heads]
    o_intra = [_dot(a[h], v[h]) for h in heads]
    for h in heads:
        st = st_ref[h]
        o_inter = []
        for ci in range(nc):
            o_inter.append(_dot_nt(q_inter[ci * c:(ci + 1) * c, ksl[h]], st.astype(BF16)))
            st = st * jnp.exp(g[ci * c + c - 1:ci * c + c, ksl[h]]) + kv[h][ci]
        st_ref[h] = st
        o = _rms(o_intra[h] + jnp.concatenate(o_inter, axis=0), onw)
        r = r_ref[:, vsl[h]]
        o_ref[:, vsl[h]] = (o * (r * jax.nn.sigmoid(r))).astype(BF16)


def _gla_rec(q, k, v, r, la, onw, bsz, seq):
    t = bsz * seq
    qk = GLA_HEADS * GLA_DK
    vd = GLA_HEADS * GLA_DV
    nb = seq // GLA_BLOCK
    row = lambda n: pl.BlockSpec((GLA_BLOCK, n), lambda b, j: (b * nb + j, 0))
    onw2 = onw.reshape(1, GLA_DV)
    return pl.pallas_call(
        _gla_rec_kernel,
        grid=(bsz, nb),
        in_specs=[row(qk), row(qk), row(vd), row(vd), row(qk), pl.BlockSpec((1, GLA_DV), lambda b, j: (0, 0))],
        out_specs=row(vd),
        out_shape=jax.ShapeDtypeStruct((t, vd), BF16),
        scratch_shapes=[pltpu.VMEM((GLA_HEADS, GLA_DV, GLA_DK), F32)],
        compiler_params=_cparams("arbitrary", "arbitrary"),
        name="gla_rec",
    )(q, k, v, r, la, onw2)


def _store_rows(ref, val):
    n, d = val.shape
    ref[...] = val.reshape(n * d // LANES, LANES)


def _load_rows(ref, start, n, nc, dtype):
    return jnp.concatenate([ref[pl.ds(start * nc + c, n, stride=nc), :].astype(dtype) for c in range(nc)], axis=1)


def _post_kernel(a_ref, wo_ref, x_ref, nw_ref, wrh_ref, wrl_ref,
                 x1_ref, h_ref, ids_ref, gates_ref, cnt_ref, carry_ref):
    tm = a_ref.shape[0]
    ne = MOE_EXPERTS

    @pl.when(pl.program_id(0) == 0)
    def _():
        carry_ref[...] = jnp.zeros_like(carry_ref)

    x1 = x_ref[...] + _dot(a_ref[...], wo_ref[...])
    x1_ref[...] = x1
    h = _rms(x1, nw_ref[...])
    _store_rows(h_ref, h)
    h_hi, h_lo = _split_bf16(h, 2)
    lt = _dot_nt(wrh_ref[...], h_hi) + _dot_nt(wrh_ref[...], h_lo) + _dot_nt(wrl_ref[...], h_hi)
    gl = lt[0:MOE_GROUPS]
    gmax = jnp.max(gl, axis=0, keepdims=True)
    gi = lax.broadcasted_iota(I32, gl.shape, 0)
    g_sel = jnp.min(jnp.where(gl == gmax, gi, MOE_GROUPS), axis=0, keepdims=True)
    g_w = 1.0 / jnp.sum(jnp.exp(gl - gmax), axis=0, keepdims=True)
    el = lt[MOE_GROUPS:MOE_GROUPS + ne]
    ei = lax.broadcasted_iota(I32, el.shape, 0)
    in_group = (ei // MOE_PER_GROUP) == g_sel
    neg = jnp.float32(-jnp.inf)
    el1 = jnp.where(in_group, el, neg)
    l1 = jnp.max(el1, axis=0, keepdims=True)
    i1 = jnp.min(jnp.where(el1 == l1, ei, ne), axis=0, keepdims=True)
    el2 = jnp.where(ei == i1, neg, el1)
    l2 = jnp.max(el2, axis=0, keepdims=True)
    i2 = jnp.min(jnp.where(el2 == l2, ei, ne), axis=0, keepdims=True)
    e2 = jnp.exp(l2 - l1)
    gate1 = g_w / (1.0 + e2)
    gate2 = g_w * e2 / (1.0 + e2)
    oh1 = ei == i1
    oh2 = ei == i2
    both = jnp.where(oh1 | oh2, 1.0, 0.0)
    su = lax.broadcasted_iota(I32, (tm, tm), 0)
    tu = lax.broadcasted_iota(I32, (tm, tm), 1)
    upper = jnp.where(su < tu, 1.0, 0.0).astype(BF16)
    prefix = _dot(both.astype(BF16), upper) + carry_ref[:, 0:1]
    rank1 = jnp.sum(jnp.where(oh1, prefix, 0.0), axis=0, keepdims=True)
    rank2 = jnp.sum(jnp.where(oh2, prefix, 0.0), axis=0, keepdims=True)
    carry = carry_ref[...] + jnp.sum(both, axis=1, keepdims=True)
    carry_ref[...] = carry
    cnt_ref[...] = carry.astype(I32)
    ids_ref[...] = jnp.concatenate([i1, i2, rank1.astype(I32), rank2.astype(I32)], axis=0)
    gates_ref[...] = jnp.concatenate([gate1, gate2], axis=0)


def _post(a, wo, x, nw, w_group, w_expert):
    t, d = x.shape
    tm = TOK_TILE
    wr = jnp.concatenate([w_group.T, w_expert.T], axis=0)
    wr = jnp.pad(wr, ((0, LANES - wr.shape[0]), (0, 0)))
    wrh = wr.astype(BF16)
    wrl = (wr - wrh.astype(F32)).astype(BF16)
    wo = wo.astype(BF16)
    nw2 = nw.reshape(1, d)
    row = lambda n: pl.BlockSpec((tm, n), lambda i: (i, 0))
    full = lambda arr: pl.BlockSpec(arr.shape, lambda i: (0, 0))
    return pl.pallas_call(
        _post_kernel,
        grid=(t // tm,),
        in_specs=[row(a.shape[1]), full(wo), row(d), full(nw2), full(wrh), full(wrl)],
        out_specs=[row(d), pl.BlockSpec((tm * d // LANES, LANES), lambda i: (i, 0)),
                   pl.BlockSpec((4, tm), lambda i: (0, i)), pl.BlockSpec((2, tm), lambda i: (0, i)),
                   pl.BlockSpec((MOE_EXPERTS, LANES), lambda i: (0, 0))],
        out_shape=[jax.ShapeDtypeStruct((t, d), F32), jax.ShapeDtypeStruct((t * d // LANES, LANES), F32),
                   jax.ShapeDtypeStruct((4, t), I32), jax.ShapeDtypeStruct((2, t), F32),
                   jax.ShapeDtypeStruct((MOE_EXPERTS, LANES), I32)],
        scratch_shapes=[pltpu.VMEM((MOE_EXPERTS, LANES), F32)],
        compiler_params=_cparams("arbitrary"),
        name="post_router",
    )(a, wo, x, nw2, wrh, wrl)


def _row(ref, i):
    return ref.at[pl.ds(pl.multiple_of(i * ROW_SUB, ROW_SUB), ROW_SUB)]


def _zero_fill(zinfo_ref, zero_ref, xs_hbm, sem, wait):
    ne = MOE_EXPERTS
    nblk = xs_hbm.shape[0] // (EXP_BLOCK * ROW_SUB)

    def copy(off_rows, n_rows):
        off = pl.multiple_of(off_rows * ROW_SUB, ROW_SUB)
        cp = pltpu.make_async_copy(zero_ref.at[pl.ds(0, n_rows * ROW_SUB)],
                                   xs_hbm.at[pl.ds(off, n_rows * ROW_SUB)], sem)
        cp.wait() if wait else cp.start()

    def expert(e, carry):
        pad = zinfo_ref[ne + e]
        for b in range(EXP_BLOCK.bit_length() - 1):
            @pl.when(((pad >> b) & 1) == 1)
            def _():
                copy(zinfo_ref[e] + (pad & ((1 << b) - 1)), 1 << b)
        return carry

    def tail(kk, carry):
        blk = zinfo_ref[2 * ne] + kk

        @pl.when(blk < nblk)
        def _():
            copy(blk * EXP_BLOCK, EXP_BLOCK)
        return carry

    lax.fori_loop(0, ne, expert, 0)
    lax.fori_loop(0, ne + 1, tail, 0)


RING = 3
ISSUE_UNROLL = 8


def _slot_copies(dest_hbm, i, idx_ref, slot, sem):
    return [pltpu.make_async_copy(dest_hbm.at[a, i], idx_ref.at[slot, a], sem.at[slot]) for a in range(2)]


def _dispatch_kernel(zinfo_ref, dest_hbm, h_hbm, xs_hbm, idx_ref, hbuf_ref, zero_ref, sem_idx, sem_h, sem_rows):
    i = pl.program_id(0)
    nt = pl.num_programs(0)
    tr = ROW_TILE
    rows = tr * ROW_SUB

    def tile_copies(j):
        slot = j % RING
        src = h_hbm.at[pl.ds(pl.multiple_of(j * rows, rows), rows)]
        return [pltpu.make_async_copy(src, hbuf_ref.at[slot], sem_h.at[slot])] + \
            _slot_copies(dest_hbm, j, idx_ref, slot, sem_idx)

    def scatter_rows(slot):
        def body(t, carry):
            src = _row(hbuf_ref.at[slot], t)
            pltpu.make_async_copy(src, _row(xs_hbm, idx_ref[slot, 0, t]), sem_rows.at[slot]).start()
            pltpu.make_async_copy(src, _row(xs_hbm, idx_ref[slot, 1, t]), sem_rows.at[slot]).start()
            return carry
        lax.fori_loop(0, tr, body, 0, unroll=ISSUE_UNROLL)

    def wait_rows(slot):
        def body(t, carry):
            pltpu.make_async_copy(_row(hbuf_ref.at[0], 0), _row(xs_hbm, 0), sem_rows.at[slot]).wait()
            return carry
        lax.fori_loop(0, 2 * tr, body, 0, unroll=ISSUE_UNROLL)

    @pl.when(i == 0)
    def _():
        for cp in tile_copies(0):
            cp.start()
        zero_ref[...] = jnp.zeros_like(zero_ref)
        _zero_fill(zinfo_ref, zero_ref, xs_hbm, sem_rows.at[1], wait=False)
        _zero_fill(zinfo_ref, zero_ref, xs_hbm, sem_rows.at[1], wait=True)

    @pl.when(i + 1 < nt)
    def _():
        for cp in tile_copies(i + 1):
            cp.start()

    for cp in tile_copies(i):
        cp.wait()

    for slot in range(RING):
        @pl.when(i % RING == slot)
        def _():
            scatter_rows(slot)

            @pl.when(i > 0)
            def _():
                wait_rows((slot - 1) % RING)

            @pl.when(i == nt - 1)
            def _():
                wait_rows(slot)


def _dispatch(h, dest, zoff, n_slots):
    t = h.shape[0] // ROW_SUB
    tr = ROW_TILE
    nt = t // tr
    grid_spec = pltpu.PrefetchScalarGridSpec(
        num_scalar_prefetch=1,
        grid=(nt,),
        in_specs=[pl.BlockSpec(memory_space=pl.ANY), pl.BlockSpec(memory_space=pl.ANY)],
        out_specs=pl.BlockSpec(memory_space=pl.ANY),
        scratch_shapes=[pltpu.SMEM((RING, 2, tr), I32), pltpu.VMEM((RING, tr * ROW_SUB, LANES), F32),
                        pltpu.VMEM((EXP_BLOCK * ROW_SUB, LANES), F32),
                        pltpu.SemaphoreType.DMA((RING,)), pltpu.SemaphoreType.DMA((RING,)),
                        pltpu.SemaphoreType.DMA((RING,))],
    )
    return pl.pallas_call(
        _dispatch_kernel,
        grid_spec=grid_spec,
        out_shape=jax.ShapeDtypeStruct((n_slots * ROW_SUB, LANES), F32),
        compiler_params=_cparams("arbitrary"),
        name="moe_dispatch",
    )(zoff, dest.reshape(2, nt, tr), h)


def _expert_kernel(be_ref, nused_ref, xs_ref, w1_ref, w3_ref, w2_ref, ys_ref, w1b_ref, w3b_ref, w2b_ref):
    i = pl.program_id(0)
    used = i < nused_ref[0]

    @pl.when(used & ((i == 0) | (be_ref[i] != be_ref[jnp.maximum(i - 1, 0)])))
    def _():
        w1b_ref[...] = w1_ref[0, 0].astype(BF16)
        w3b_ref[...] = w3_ref[0, 0].astype(BF16)
        w2b_ref[...] = w2_ref[0, 0].astype(BF16)

    @pl.when(used)
    def _():
        xb = _load_rows(xs_ref, 0, EXP_BLOCK, ROW_SUB, BF16)
        a = _dot(xb, w1b_ref[...])
        b = _dot(xb, w3b_ref[...])
        hid = (a * jax.nn.sigmoid(a) * b).astype(BF16)
        _store_rows(ys_ref, _dot(hid, w2b_ref[...]))

    @pl.when(i >= nused_ref[0])
    def _():
        ys_ref[...] = jnp.zeros_like(ys_ref)


def _experts(xs, block_e, n_used, w1, w3, w2, layer):
    bm = EXP_BLOCK
    n_slots = xs.shape[0] // ROW_SUB
    nblk = n_slots // bm
    d, ff = w1.shape[-2:]

    def last_used(i, nu):
        return jnp.maximum(jnp.minimum(i, nu[0] - 1), 0)

    def xmap(i, be, nu):
        return (last_used(i, nu), 0)

    def wmap(i, be, nu):
        return (layer, be[last_used(i, nu)], 0, 0)

    grid_spec = pltpu.PrefetchScalarGridSpec(
        num_scalar_prefetch=2,
        grid=(nblk,),
        in_specs=[pl.BlockSpec((bm * ROW_SUB, LANES), xmap), pl.BlockSpec((1, 1, d, ff), wmap),
                  pl.BlockSpec((1, 1, d, ff), wmap), pl.BlockSpec((1, 1, ff, d), wmap)],
        out_specs=pl.BlockSpec((bm * ROW_SUB, LANES), lambda i, be, nu: (i, 0)),
        scratch_shapes=[pltpu.VMEM((d, ff), BF16), pltpu.VMEM((d, ff), BF16), pltpu.VMEM((ff, d), BF16)],
    )
    return pl.pallas_call(
        _expert_kernel,
        grid_spec=grid_spec,
        out_shape=jax.ShapeDtypeStruct((n_slots * ROW_SUB, LANES), F32),
        compiler_params=_cparams("arbitrary"),
        name="moe_experts",
    )(block_e, n_used, xs, w1, w3, w2)


def _combine_tiles(dest_hbm, ys_hbm, x_ref, g_ref, o_ref, idx_ref, buf_ref, sem_idx, sem_rows, finish):
    i = pl.program_id(0)
    nt = pl.num_programs(0)
    tr = x_ref.shape[0]

    def gather(slot):
        buf = buf_ref.at[slot]

        def body(t, carry):
            pltpu.make_async_copy(_row(ys_hbm, idx_ref[slot, 0, t]), _row(buf, t), sem_rows.at[slot]).start()
            pltpu.make_async_copy(_row(ys_hbm, idx_ref[slot, 1, t]), _row(buf, tr + t), sem_rows.at[slot]).start()
            return carry

        lax.fori_loop(0, tr, body, 0, unroll=ISSUE_UNROLL)

    def wait_rows(slot):
        def body(t, carry):
            pltpu.make_async_copy(_row(ys_hbm, 0), _row(buf_ref.at[0], 0), sem_rows.at[slot]).wait()
            return carry
        lax.fori_loop(0, 2 * tr, body, 0, unroll=ISSUE_UNROLL)

    @pl.when(i == 0)
    def _():
        for cp in _slot_copies(dest_hbm, 0, idx_ref, 0, sem_idx):
            cp.start()
        for cp in _slot_copies(dest_hbm, 0, idx_ref, 0, sem_idx):
            cp.wait()
        gather(0)

        @pl.when(nt > 1)
        def _():
            for cp in _slot_copies(dest_hbm, 1, idx_ref, 1, sem_idx):
                cp.start()

    for slot in range(2):
        @pl.when(i % 2 == slot)
        def _():
            nxt = 1 - slot

            @pl.when(i + 1 < nt)
            def _():
                for cp in _slot_copies(dest_hbm, i + 1, idx_ref, nxt, sem_idx):
                    cp.wait()
                gather(nxt)

            @pl.when(i + 2 < nt)
            def _():
                for cp in _slot_copies(dest_hbm, i + 2, idx_ref, slot, sem_idx):
                    cp.start()

            wait_rows(slot)
            g = g_ref[...]
            y1 = _load_rows(buf_ref.at[slot], 0, tr, ROW_SUB, F32)
            y2 = _load_rows(buf_ref.at[slot], tr, tr, ROW_SUB, F32)
            o_ref[...] = finish(x_ref[...] + g[:, 0:1] * y1 + g[:, 1:2] * y2)


def _combine_scratch(tr):
    return [pltpu.SMEM((2, 2, tr), I32), pltpu.VMEM((2, 2 * tr * ROW_SUB, LANES), F32),
            pltpu.SemaphoreType.DMA((2,)), pltpu.SemaphoreType.DMA((2,))]


def _combine_norm_kernel(dest_hbm, ys_hbm, x_ref, g_ref, nw_ref, o_ref, *scratch):
    _combine_tiles(dest_hbm, ys_hbm, x_ref, g_ref, o_ref, *scratch, finish=lambda v: _rms(v, nw_ref[...]))


def _combine_norm(ys, dest, x, gates, nw):
    t, d = x.shape
    tr = ROW_TILE
    nt = t // tr
    return pl.pallas_call(
        _combine_norm_kernel,
        grid=(nt,),
        in_specs=[pl.BlockSpec(memory_space=pl.ANY), pl.BlockSpec(memory_space=pl.ANY),
                  pl.BlockSpec((tr, d), lambda i: (i, 0)), pl.BlockSpec((tr, 2), lambda i: (i, 0)),
                  pl.BlockSpec((1, d), lambda i: (0, 0))],
        out_specs=pl.BlockSpec((tr, d), lambda i: (i, 0)),
        out_shape=jax.ShapeDtypeStruct((t, d), F32),
        scratch_shapes=_combine_scratch(tr),
        compiler_params=_cparams("arbitrary"),
        name="moe_combine",
    )(dest.reshape(2, nt, tr), ys, x, gates.T, nw.reshape(1, d))


def _slots_kernel(ids_ref, pst_ref, dest_ref):
    ids = ids_ref[...]
    ei = lax.broadcasted_iota(I32, (MOE_EXPERTS, ids.shape[1]), 0)
    pst = pst_ref[...]
    d1 = jnp.sum(jnp.where(ei == ids[0:1], pst, 0), axis=0, keepdims=True) + ids[2:3]
    d2 = jnp.sum(jnp.where(ei == ids[1:2], pst, 0), axis=0, keepdims=True) + ids[3:4]
    dest_ref[...] = jnp.concatenate([d1, d2], axis=0)


def _slots(ids, pstarts):
    t = ids.shape[1]
    tm = TOK_TILE
    return pl.pallas_call(
        _slots_kernel,
        grid=(t // tm,),
        in_specs=[pl.BlockSpec((4, tm), lambda i: (0, i)), pl.BlockSpec((MOE_EXPERTS, 1), lambda i: (0, 0))],
        out_specs=pl.BlockSpec((2, tm), lambda i: (0, i)),
        out_shape=jax.ShapeDtypeStruct((2, t), I32),
        compiler_params=_cparams("arbitrary"),
        name="moe_slots",
    )(ids, pstarts.reshape(MOE_EXPERTS, 1))


def _combine_plain_kernel(dest_hbm, ys_hbm, x_ref, g_ref, o_ref, *scratch):
    _combine_tiles(dest_hbm, ys_hbm, x_ref, g_ref, o_ref, *scratch, finish=lambda v: v)


def _combine_plain(ys, dest, x, gates):
    t, d = x.shape
    tr = ROW_TILE
    nt = t // tr
    return pl.pallas_call(
        _combine_plain_kernel,
        grid=(nt,),
        in_specs=[pl.BlockSpec(memory_space=pl.ANY), pl.BlockSpec(memory_space=pl.ANY),
                  pl.BlockSpec((tr, d), lambda i: (i, 0)), pl.BlockSpec((tr, 2), lambda i: (i, 0))],
        out_specs=pl.BlockSpec((tr, d), lambda i: (i, 0)),
        out_shape=jax.ShapeDtypeStruct((t, d), F32),
        scratch_shapes=_combine_scratch(tr),
        compiler_params=_cparams("arbitrary"),
        name="moe_combine",
    )(dest.reshape(2, nt, tr), ys, x, gates.T)


def _moe(h, t, ids, counts, w1, w3, w2, layer):
    bm = EXP_BLOCK
    n_asg = 2 * t
    n_slots = n_asg + MOE_EXPERTS * bm
    cnt = counts[:, 0]
    padded = (cnt + bm - 1) // bm * bm
    pend = jnp.cumsum(padded)
    pstarts = (pend - padded).astype(I32)
    dest = _slots(ids, pstarts)
    nblk = n_slots // bm
    block_start = jnp.arange(nblk, dtype=I32) * bm
    block_e = jnp.minimum(jnp.sum(pend[None, :] <= block_start[:, None], axis=1), MOE_EXPERTS - 1).astype(I32)
    n_used = (pend[-1:] // bm).astype(I32)
    zinfo = jnp.concatenate([pstarts + cnt, padded - cnt, n_used]).astype(I32)
    xs = _dispatch(h, dest, zinfo, n_slots)
    ys = _experts(xs, block_e, n_used, w1, w3, w2, layer)
    return ys, dest


def _mla_in_combine_kernel(dest_hbm, ys_hbm, g_ref, x_ref, *rest):
    ins, (xc_ref, qt_ref, k_ref, vt_ref), scratch = rest[:12], rest[12:16], rest[16:]
    _combine_tiles(dest_hbm, ys_hbm, x_ref, g_ref, xc_ref, *scratch, finish=lambda v: v)
    _mla_in_kernel(xc_ref, *ins, qt_ref, k_ref, vt_ref)


def _mla_in_kernel(x_ref, pos_ref, nw_ref, wq_ref, wkv_ref, wks_ref, qn_ref, kvn_ref, wqn_ref, wqr_ref, wqs_ref,
                   wuk_ref, freq_ref, qt_ref, k_ref, vt_ref):
    tm = x_ref.shape[0]
    tb = ATT_BLOCK
    hb = _rms(x_ref[...], nw_ref[...]).astype(BF16)
    c_q = _rms(_dot(hb, wq_ref[...]), qn_ref[...]).astype(BF16)
    kv_t = _dot_nt(wkv_ref[...], hb)
    ks_t = _dot_nt(wks_ref[...], hb)
    c_kv = kv_t[:MLA_KV_LORA]
    c_kv = c_kv * lax.rsqrt(jnp.mean(c_kv * c_kv, axis=0, keepdims=True) + EPS) * kvn_ref[...]
    ang = freq_ref[...] * pos_ref[...].astype(F32)
    cos32, sin32 = jnp.cos(ang), jnp.sin(ang)
    cos = jnp.concatenate([cos32] * 4, axis=0)
    sin = jnp.concatenate([-sin32, sin32] * 2, axis=0)
    k_rope = kv_t[MLA_KV_LORA:] * cos + ks_t * sin
    k_ref[...] = jnp.concatenate([c_kv, k_rope], axis=0).T.astype(BF16)
    v_ext = jnp.concatenate([c_kv, jnp.ones((VT_ROWS - MLA_KV_LORA, tm), F32)], axis=0).astype(BF16)
    for c in range(tm // ATT_KV):
        vt_ref[c] = v_ext[:, c * ATT_KV:(c + 1) * ATT_KV]
    q_nope = _dot_nt(wqn_ref[...], c_q).astype(BF16)
    q_r = _dot_nt(wqr_ref[...], c_q)
    q_s = _dot_nt(wqs_ref[...], c_q)
    for h in range(MLA_HEADS):
        sl = slice(h * LANES, (h + 1) * LANES)
        q_lat = (_dot(wuk_ref[h], q_nope[sl]) * Q_SCALE).astype(BF16)
        q_rope = ((q_r[sl] * cos + q_s[sl] * sin) * Q_SCALE).astype(BF16)
        for c in range(tm // tb):
            cs = slice(c * tb, (c + 1) * tb)
            qt_ref[c, 2 * h * LANES:(2 * h + 1) * LANES, :] = q_lat[:, cs]
            qt_ref[c, (2 * h + 1) * LANES:(2 * h + 2) * LANES, :] = q_rope[:, cs]


def _swap_halves(w):
    half = w.shape[-1] // 2
    return jnp.concatenate([w[..., half:], w[..., :half]], axis=-1)


def _mla_in(x, pos, nw, w_in, q_norm, w_uq, kv_norm, w_uk, pending=None):
    t, d = x.shape
    tm = TOK_TILE
    tb = ATT_BLOCK
    nh = MLA_HEADS
    pad_r = LANES - MLA_ROPE
    w_kr = w_in[:, MLA_Q_LORA + MLA_KV_LORA:]
    wq = w_in[:, :MLA_Q_LORA].astype(BF16)
    wkv = jnp.pad(w_in[:, MLA_Q_LORA:], ((0, 0), (0, pad_r))).T.astype(BF16)
    wks = jnp.pad(_swap_halves(w_kr), ((0, 0), (0, pad_r))).T.astype(BF16)
    wqn = w_uq[:, :, :MLA_NOPE].reshape(MLA_Q_LORA, nh * MLA_NOPE).T.astype(BF16)
    w_r = w_uq[:, :, MLA_NOPE:]
    wqr = jnp.pad(w_r, ((0, 0), (0, 0), (0, pad_r))).reshape(MLA_Q_LORA, nh * LANES).T.astype(BF16)
    wqs = jnp.pad(_swap_halves(w_r), ((0, 0), (0, 0), (0, pad_r))).reshape(MLA_Q_LORA, nh * LANES).T.astype(BF16)
    wuk = w_uk.transpose(1, 0, 2).astype(BF16)
    inv_freq = 1.0 / (ROPE_THETA ** (jnp.arange(0, MLA_ROPE, 2, dtype=F32) / MLA_ROPE))
    full = lambda a: pl.BlockSpec(a.shape, lambda i: (0,) * a.ndim)
    args = (x, pos.reshape(1, t), nw.reshape(1, d), wq, wkv, wks, q_norm.reshape(1, -1), kv_norm.reshape(-1, 1),
            wqn, wqr, wqs, wuk, inv_freq.reshape(-1, 1))
    nsub = tm // tb
    nt = t // tm
    in_specs = [pl.BlockSpec((tm, d), lambda i: (i, 0)), pl.BlockSpec((1, tm), lambda i: (0, i))] \
        + [full(a) for a in args[2:]]
    out_specs = [pl.BlockSpec((nsub, 2 * nh * LANES, tb), lambda i: (i, 0, 0)),
                 pl.BlockSpec((tm, 2 * LANES), lambda i: (i, 0)),
                 pl.BlockSpec((tm // ATT_KV, VT_ROWS, ATT_KV), lambda i: (i, 0, 0))]
    out_shape = [jax.ShapeDtypeStruct((t // tb, 2 * nh * LANES, tb), BF16),
                 jax.ShapeDtypeStruct((t, 2 * LANES), BF16),
                 jax.ShapeDtypeStruct((t // ATT_KV, VT_ROWS, ATT_KV), BF16)]
    if pending is None:
        qt, k, vt = pl.pallas_call(
            _mla_in_kernel, grid=(nt,), in_specs=in_specs, out_specs=out_specs, out_shape=out_shape,
            compiler_params=_cparams("arbitrary"), name="mla_in",
        )(*args)
        return x, qt, k, vt
    ys, dest, gates = pending
    any_spec = pl.BlockSpec(memory_space=pl.ANY)
    return pl.pallas_call(
        _mla_in_combine_kernel,
        grid=(nt,),
        in_specs=[any_spec, any_spec, pl.BlockSpec((tm, 2), lambda i: (i, 0))] + in_specs,
        out_specs=[pl.BlockSpec((tm, d), lambda i: (i, 0))] + out_specs,
        out_shape=[jax.ShapeDtypeStruct((t, d), F32)] + out_shape,
        scratch_shapes=_combine_scratch(tm),
        compiler_params=_cparams("arbitrary"),
        name="mla_in_combine",
    )(dest.reshape(2, nt, tm), ys, gates.T, *args)


def _mla_attn_kernel(qt_ref, k_ref, vt_ref, wuv_ref, o_ref, *state):
    i = pl.program_id(1)
    tq = ATT_BLOCK
    tk = ATT_KV
    nh = MLA_HEADS
    dq = 2 * LANES
    s_refs, smax_refs, m_refs, acc_refs = state[0::4], state[1::4], state[2::4], state[3::4]
    last = (i * tq) // tk

    def scores(j, h):
        kb = k_ref[pl.ds(pl.multiple_of(j * tk, tk), tk), :]
        s = _dot(kb, qt_ref[0, h * dq:(h + 1) * dq, :])
        return s, jnp.max(s, axis=0, keepdims=True)

    def consume(j, h, s, s_max):
        m_old = m_refs[h][...]
        m_new = jnp.maximum(m_old, s_max)
        alpha = jnp.exp2(m_old - m_new)
        p = jnp.exp2(s - m_new).astype(BF16)
        acc_refs[h][...] = alpha * acc_refs[h][...] + _dot(vt_ref[j], p)
        m_refs[h][...] = m_new

    for h in range(nh):
        m_refs[h][...] = jnp.full_like(m_refs[h], -jnp.inf)
        acc_refs[h][...] = jnp.zeros_like(acc_refs[h])
        s_refs[h][...], smax_refs[h][...] = scores(0, h)

    def full_block(j, carry):
        for h in range(nh):
            s, s_max = s_refs[h][...], smax_refs[h][...]
            s_new, s_max_new = scores(j + 1, h)
            consume(j, h, s, s_max)
            s_refs[h][...], smax_refs[h][...] = s_new, s_max_new
        return carry

    lax.fori_loop(0, last, full_block, 0)
    kpos = lax.broadcasted_iota(I32, (tk, tq), 0) + last * tk
    qpos = lax.broadcasted_iota(I32, (tk, tq), 1) + i * tq
    for h in range(nh):
        s = jnp.where(kpos <= qpos, s_refs[h][...], -jnp.inf)
        consume(last, h, s, jnp.max(s, axis=0, keepdims=True))
    for h in range(nh):
        acc = acc_refs[h][...]
        o_lat = (acc[:MLA_KV_LORA] / acc[MLA_KV_LORA:MLA_KV_LORA + 1]).astype(BF16)
        o_ref[:, h * MLA_V:(h + 1) * MLA_V] = _dot_tn(o_lat, wuv_ref[h]).astype(BF16)


def _mla_attn(qt, k, vt, w_uv, bsz, seq):
    t = bsz * seq
    tq = ATT_BLOCK
    tk = ATT_KV
    nq = seq // tq
    nh = MLA_HEADS
    wuv = w_uv.transpose(1, 0, 2).astype(BF16)
    return pl.pallas_call(
        _mla_attn_kernel,
        grid=(bsz, nq),
        in_specs=[pl.BlockSpec((1, 2 * nh * LANES, tq), lambda b, i: (b * nq + i, 0, 0)),
                  pl.BlockSpec((seq, 2 * LANES), lambda b, i: (b, 0)),
                  pl.BlockSpec((seq // tk, VT_ROWS, tk), lambda b, i: (b, 0, 0)),
                  pl.BlockSpec(wuv.shape, lambda b, i: (0, 0, 0))],
        out_specs=pl.BlockSpec((tq, nh * MLA_V), lambda b, i: (b * nq + i, 0)),
        out_shape=jax.ShapeDtypeStruct((t, nh * MLA_V), BF16),
        scratch_shapes=[pltpu.VMEM((tk, tq), F32), pltpu.VMEM((1, tq), F32), pltpu.VMEM((1, tq), F32),
                        pltpu.VMEM((VT_ROWS, tq), F32)] * nh,
        compiler_params=_cparams("arbitrary", "arbitrary"),
        name="mla_attn",
    )(qt, k, vt, wuv)


def kernel(x, positions, attn_norm, ffn_norm, final_norm, gla_w_in, gla_w_gate, gla_b_gate, gla_out_norm, gla_w_o,
           mla_w_in, mla_q_norm, mla_w_uq, mla_kv_norm, mla_w_uk, mla_w_uv, mla_w_o,
           moe_w_group, moe_w_expert, moe_w1, moe_w3, moe_w2):
    bsz, seq, d = x.shape
    t = bsz * seq
    depth = attn_norm.shape[0]
    xf = x.reshape(t, d)
    pos = positions.reshape(t)
    pending = None
    for i in range(depth):
        j = i // 2
        if i % 2 == 0:
            if pending is not None:
                xf = _combine_plain(pending[0], pending[1], xf, pending[2])
            q, k, v, r, la = _gla_in(xf, attn_norm[i], gla_w_in[j], gla_w_gate[j], gla_b_gate[j])
            mixed = _gla_rec(q, k, v, r, la, gla_out_norm[j], bsz, seq)
            w_o = gla_w_o[j]
        else:
            xf, qt, kc, vt = _mla_in(xf, pos, attn_norm[i], mla_w_in[j], mla_q_norm[j], mla_w_uq[j],
                                     mla_kv_norm[j], mla_w_uk[j], pending)
            mixed = _mla_attn(qt, kc, vt, mla_w_uv[j], bsz, seq)
            w_o = mla_w_o[j]
        xf, h, ids, gates, counts = _post(mixed, w_o, xf, ffn_norm[i], moe_w_group[i], moe_w_expert[i])
        ys, dest = _moe(h, t, ids, counts, moe_w1, moe_w3, moe_w2, i)
        pending = (ys, dest, gates)
    out = _combine_norm(pending[0], pending[1], xf, pending[2], final_norm)
    return out.reshape(bsz, seq, d)
```

```python
import jax
import jax.numpy as jnp
from jax import lax
from jax.experimental import pallas as pl
from jax.experimental.pallas import tpu as pltpu

F32 = jnp.float32
BF16 = jnp.bfloat16
I32 = jnp.int32

EPS = 1e-6
GLA_HEADS = 4
GLA_DK = 128
GLA_DV = 256
GLA_GATE_RANK = 16
GLA_GATE_TAU = 16.0
GLA_CHUNK = 64
MLA_HEADS = 8
MLA_NOPE = 128
MLA_ROPE = 64
MLA_V = 128
MLA_Q_LORA = 256
MLA_KV_LORA = 128
MLA_SCALE = (MLA_NOPE + MLA_ROPE) ** -0.5
Q_SCALE = MLA_SCALE * 1.4426950408889634
VT_ROWS = MLA_KV_LORA + 16
ROPE_THETA = 10000.0
MOE_GROUPS = 8
MOE_PER_GROUP = 8
MOE_EXPERTS = MOE_GROUPS * MOE_PER_GROUP

D_MODEL = 1024
LANES = 128
ROW_SUB = D_MODEL // LANES
VMEM_LIMIT = 48 * 1024 * 1024

TOK_TILE = 512
GLA_BLOCK = 256
ATT_BLOCK = 512
ATT_KV = 512
ROW_TILE = 512
EXP_BLOCK = 512


def _cparams(*sem):
    return pltpu.CompilerParams(dimension_semantics=sem, vmem_limit_bytes=VMEM_LIMIT)


def _rms(x, w):
    return x * lax.rsqrt(jnp.mean(x * x, axis=-1, keepdims=True) + EPS) * w


def _dot(a, b):
    return jnp.dot(a, b, preferred_element_type=F32)


def _dot_nt(a, b):
    return lax.dot_general(a, b, (((1,), (1,)), ((), ())), preferred_element_type=F32)


def _dot_tn(a, b):
    return lax.dot_general(a, b, (((0,), (0,)), ((), ())), preferred_element_type=F32)


def _split_bf16(x, pieces):
    out = []
    for _ in range(pieces - 1):
        hi = x.astype(BF16)
        out.append(hi)
        x = x - hi.astype(F32)
    out.append(x.astype(BF16))
    return out


def _gla_in_kernel(x_ref, nw_ref, wq_ref, wk_ref, wv_ref, wr_ref, wa_ref, wgh_ref, wgl_ref, bg_ref,
                   q_ref, k_ref, v_ref, r_ref, la_ref):
    hb = _rms(x_ref[...], nw_ref[...]).astype(BF16)
    a_hi, a_lo = _split_bf16(_dot(hb, wa_ref[...]), 2)
    z = _dot(a_hi, wgh_ref[...]) + _dot(a_hi, wgl_ref[...]) + _dot(a_lo, wgh_ref[...]) + bg_ref[...]
    log_sig = jnp.minimum(z, 0.0) - jnp.log1p(jnp.exp(-jnp.abs(z)))
    la_ref[...] = log_sig * (1.0 / GLA_GATE_TAU)
    q_ref[...] = _dot(hb, wq_ref[...]) * (GLA_DK ** -0.5)
    k_ref[...] = _dot(hb, wk_ref[...])
    v_ref[...] = _dot(hb, wv_ref[...]).astype(BF16)
    r_ref[...] = _dot(hb, wr_ref[...])


def _gla_in(x, nw, w_in, w_gate, b_gate):
    t, d = x.shape
    qk = GLA_HEADS * GLA_DK
    vd = GLA_HEADS * GLA_DV
    wq = w_in[:, :qk].astype(BF16)
    wk = w_in[:, qk:2 * qk].astype(BF16)
    wv = w_in[:, 2 * qk:2 * qk + vd].astype(BF16)
    wa = jnp.pad(w_in[:, 2 * qk + vd:2 * qk + vd + GLA_GATE_RANK], ((0, 0), (0, LANES - GLA_GATE_RANK))).astype(BF16)
    wr = w_in[:, 2 * qk + vd + GLA_GATE_RANK:].astype(BF16)
    wg = jnp.pad(w_gate, ((0, LANES - GLA_GATE_RANK), (0, 0)))
    wgh = wg.astype(BF16)
    wgl = (wg - wgh.astype(F32)).astype(BF16)
    tm = TOK_TILE
    row = lambda n: pl.BlockSpec((tm, n), lambda i: (i, 0))
    full = lambda a: pl.BlockSpec(a.shape, lambda i: (0, 0))
    nw2, bg2 = nw.reshape(1, d), b_gate.reshape(1, qk)
    return pl.pallas_call(
        _gla_in_kernel,
        grid=(t // tm,),
        in_specs=[row(d), full(nw2), full(wq), full(wk), full(wv), full(wr), full(wa), full(wgh), full(wgl),
                  full(bg2)],
        out_specs=[row(qk), row(qk), row(vd), row(vd), row(qk)],
        out_shape=[jax.ShapeDtypeStruct((t, qk), F32), jax.ShapeDtypeStruct((t, qk), F32),
                   jax.ShapeDtypeStruct((t, vd), BF16), jax.ShapeDtypeStruct((t, vd), F32),
                   jax.ShapeDtypeStruct((t, qk), F32)],
        compiler_params=_cparams("arbitrary"),
        name="gla_in",
    )(x, nw2, wq, wk, wv, wr, wa, wgh, wgl, bg2)


def _gla_rec_kernel(q_ref, k_ref, v_ref, r_ref, la_ref, onw_ref, o_ref, st_ref):
    c = GLA_CHUNK
    blk = GLA_BLOCK
    nc = blk // c

    @pl.when(pl.program_id(1) == 0)
    def _():
        st_ref[...] = jnp.zeros_like(st_ref)

    rows = lax.broadcasted_iota(I32, (blk, blk), 0)
    cols = lax.broadcasted_iota(I32, (blk, blk), 1)
    causal = (cols <= rows) & (cols // c == rows // c)
    tri = jnp.where(causal, 1.0, 0.0).astype(BF16)
    g = sum(_dot(tri, piece) for piece in _split_bf16(la_ref[...], 3))
    g_mid = jnp.concatenate(
        [jnp.broadcast_to(g[ci * c + c // 2:ci * c + c // 2 + 1], (c, g.shape[1])) for ci in range(nc)], axis=0)
    g_last = jnp.concatenate(
        [jnp.broadcast_to(g[ci * c + c - 1:ci * c + c], (c, g.shape[1])) for ci in range(nc)], axis=0)
    q = q_ref[...]
    k = k_ref[...]
    q_intra = (q * jnp.exp(g - g_mid)).astype(BF16)
    k_intra = (k * jnp.exp(g_mid - g)).astype(BF16)
    q_inter = (q * jnp.exp(g)).astype(BF16)
    k_state = (k * jnp.exp(g_last - g)).astype(BF16)
    onw = onw_ref[...]
    heads = range(GLA_HEADS)
    ksl = [slice(h * GLA_DK, (h + 1) * GLA_DK) for h in heads]
    vsl = [slice(h * GLA_DV, (h + 1) * GLA_DV) for h in heads]
    v = [v_ref[:, vsl[h]] for h in heads]
    a = [jnp.where(causal, _dot_nt(q_intra[:, ksl[h]], k_intra[:, ksl[h]]), 0.0).astype(BF16) for h in heads]
    kv = [[_dot_tn(v[h][ci * c:(ci + 1) * c], k_state[ci * c:(ci + 1) * c, ksl[h]]) for ci in range(nc)]
          for h in heads]
    o_intra = [_dot(a[h], v[h]) for h in heads]
    for h in heads:
        st = st_ref[h]
        o_inter = []
        for ci in range(nc):
            o_inter.append(_dot_nt(q_inter[ci * c:(ci + 1) * c, ksl[h]], st.astype(BF16)))
            st = st * jnp.exp(g[ci * c + c - 1:ci * c + c, ksl[h]]) + kv[h][ci]
        st_ref[h] = st
        o = _rms(o_intra[h] + jnp.concatenate(o_inter, axis=0), onw)
        r = r_ref[:, vsl[h]]
        o_ref[:, vsl[h]] = (o * (r * jax.nn.sigmoid(r))).astype(BF16)


def _gla_rec(q, k, v, r, la, onw, bsz, seq):
    t = bsz * seq
    qk = GLA_HEADS * GLA_DK
    vd = GLA_HEADS * GLA_DV
    nb = seq // GLA_BLOCK
    row = lambda n: pl.BlockSpec((GLA_BLOCK, n), lambda b, j: (b * nb + j, 0))
    onw2 = onw.reshape(1, GLA_DV)
    return pl.pallas_call(
        _gla_rec_kernel,
        grid=(bsz, nb),
        in_specs=[row(qk), row(qk), row(vd), row(vd), row(qk), pl.BlockSpec((1, GLA_DV), lambda b, j: (0, 0))],
        out_specs=row(vd),
        out_shape=jax.ShapeDtypeStruct((t, vd), BF16),
        scratch_shapes=[pltpu.VMEM((GLA_HEADS, GLA_DV, GLA_DK), F32)],
        compiler_params=_cparams("arbitrary", "arbitrary"),
        name="gla_rec",
    )(q, k, v, r, la, onw2)


def _store_rows(ref, val):
    n, d = val.shape
    ref[...] = val.reshape(n * d // LANES, LANES)


def _load_rows(ref, start, n, nc, dtype):
    return jnp.concatenate([ref[pl.ds(start * nc + c, n, stride=nc), :].astype(dtype) for c in range(nc)], axis=1)


def _post_kernel(a_ref, wo_ref, x_ref, nw_ref, wrh_ref, wrl_ref,
                 x1_ref, h_ref, ids_ref, gates_ref, cnt_ref, carry_ref):
    tm = a_ref.shape[0]
    ne = MOE_EXPERTS

    @pl.when(pl.program_id(0) == 0)
    def _():
        carry_ref[...] = jnp.zeros_like(carry_ref)

    x1 = x_ref[...] + _dot(a_ref[...], wo_ref[...])
    x1_ref[...] = x1
    h = _rms(x1, nw_ref[...])
    _store_rows(h_ref, h)
    h_hi, h_lo = _split_bf16(h, 2)
    lt = _dot_nt(wrh_ref[...], h_hi) + _dot_nt(wrh_ref[...], h_lo) + _dot_nt(wrl_ref[...], h_hi)
    gl = lt[0:MOE_GROUPS]
    gmax = jnp.max(gl, axis=0, keepdims=True)
    gi = lax.broadcasted_iota(I32, gl.shape, 0)
    g_sel = jnp.min(jnp.where(gl == gmax, gi, MOE_GROUPS), axis=0, keepdims=True)
    g_w = 1.0 / jnp.sum(jnp.exp(gl - gmax), axis=0, keepdims=True)
    el = lt[MOE_GROUPS:MOE_GROUPS + ne]
    ei = lax.broadcasted_iota(I32, el.shape, 0)
    in_group = (ei // MOE_PER_GROUP) == g_sel
    neg = jnp.float32(-jnp.inf)
    el1 = jnp.where(in_group, el, neg)
    l1 = jnp.max(el1, axis=0, keepdims=True)
    i1 = jnp.min(jnp.where(el1 == l1, ei, ne), axis=0, keepdims=True)
    el2 = jnp.where(ei == i1, neg, el1)
    l2 = jnp.max(el2, axis=0, keepdims=True)
    i2 = jnp.min(jnp.where(el2 == l2, ei, ne), axis=0, keepdims=True)
    e2 = jnp.exp(l2 - l1)
    gate1 = g_w / (1.0 + e2)
    gate2 = g_w * e2 / (1.0 + e2)
    oh1 = ei == i1
    oh2 = ei == i2
    both = jnp.where(oh1 | oh2, 1.0, 0.0)
    su = lax.broadcasted_iota(I32, (tm, tm), 0)
    tu = lax.broadcasted_iota(I32, (tm, tm), 1)
    upper = jnp.where(su < tu, 1.0, 0.0).astype(BF16)
    prefix = _dot(both.astype(BF16), upper) + carry_ref[:, 0:1]
    rank1 = jnp.sum(jnp.where(oh1, prefix, 0.0), axis=0, keepdims=True)
    rank2 = jnp.sum(jnp.where(oh2, prefix, 0.0), axis=0, keepdims=True)
    carry = carry_ref[...] + jnp.sum(both, axis=1, keepdims=True)
    carry_ref[...] = carry
    cnt_ref[...] = carry.astype(I32)
    ids_ref[...] = jnp.concatenate([i1, i2, rank1.astype(I32), rank2.astype(I32)], axis=0)
    gates_ref[...] = jnp.concatenate([gate1, gate2], axis=0)


def _post(a, wo, x, nw, w_group, w_expert):
    t, d = x.shape
    tm = TOK_TILE
    wr = jnp.concatenate([w_group.T, w_expert.T], axis=0)
    wr = jnp.pad(wr, ((0, LANES - wr.shape[0]), (0, 0)))
    wrh = wr.astype(BF16)
    wrl = (wr - wrh.astype(F32)).astype(BF16)
    wo = wo.astype(BF16)
    nw2 = nw.reshape(1, d)
    row = lambda n: pl.BlockSpec((tm, n), lambda i: (i, 0))
    full = lambda arr: pl.BlockSpec(arr.shape, lambda i: (0, 0))
    return pl.pallas_call(
        _post_kernel,
        grid=(t // tm,),
        in_specs=[row(a.shape[1]), full(wo), row(d), full(nw2), full(wrh), full(wrl)],
        out_specs=[row(d), pl.BlockSpec((tm * d // LANES, LANES), lambda i: (i, 0)),
                   pl.BlockSpec((4, tm), lambda i: (0, i)), pl.BlockSpec((2, tm), lambda i: (0, i)),
                   pl.BlockSpec((MOE_EXPERTS, LANES), lambda i: (0, 0))],
        out_shape=[jax.ShapeDtypeStruct((t, d), F32), jax.ShapeDtypeStruct((t * d // LANES, LANES), F32),
                   jax.ShapeDtypeStruct((4, t), I32), jax.ShapeDtypeStruct((2, t), F32),
                   jax.ShapeDtypeStruct((MOE_EXPERTS, LANES), I32)],
        scratch_shapes=[pltpu.VMEM((MOE_EXPERTS, LANES), F32)],
        compiler_params=_cparams("arbitrary"),
        name="post_router",
    )(a, wo, x, nw2, wrh, wrl)


def _row(ref, i):
    return ref.at[pl.ds(pl.multiple_of(i * ROW_SUB, ROW_SUB), ROW_SUB)]


def _zero_fill(zinfo_ref, zero_ref, xs_hbm, sem, wait):
    ne = MOE_EXPERTS
    nblk = xs_hbm.shape[0] // (EXP_BLOCK * ROW_SUB)

    def copy(off_rows, n_rows):
        off = pl.multiple_of(off_rows * ROW_SUB, ROW_SUB)
        cp = pltpu.make_async_copy(zero_ref.at[pl.ds(0, n_rows * ROW_SUB)],
                                   xs_hbm.at[pl.ds(off, n_rows * ROW_SUB)], sem)
        cp.wait() if wait else cp.start()

    def expert(e, carry):
        pad = zinfo_ref[ne + e]
        for b in range(EXP_BLOCK.bit_length() - 1):
            @pl.when(((pad >> b) & 1) == 1)
            def _():
                copy(zinfo_ref[e] + (pad & ((1 << b) - 1)), 1 << b)
        return carry

    def tail(kk, carry):
        blk = zinfo_ref[2 * ne] + kk

        @pl.when(blk < nblk)
        def _():
            copy(blk * EXP_BLOCK, EXP_BLOCK)
        return carry

    lax.fori_loop(0, ne, expert, 0)
    lax.fori_loop(0, ne + 1, tail, 0)


RING = 3
ISSUE_UNROLL = 8


def _slot_copies(dest_hbm, i, idx_ref, slot, sem):
    return [pltpu.make_async_copy(dest_hbm.at[a, i], idx_ref.at[slot, a], sem.at[slot]) for a in range(2)]


def _dispatch_kernel(zinfo_ref, dest_hbm, h_hbm, xs_hbm, idx_ref, hbuf_ref, zero_ref, sem_idx, sem_h, sem_rows):
    i = pl.program_id(0)
    nt = pl.num_programs(0)
    tr = ROW_TILE
    rows = tr * ROW_SUB

    def tile_copies(j):
        slot = j % RING
        src = h_hbm.at[pl.ds(pl.multiple_of(j * rows, rows), rows)]
        return [pltpu.make_async_copy(src, hbuf_ref.at[slot], sem_h.at[slot])] + \
            _slot_copies(dest_hbm, j, idx_ref, slot, sem_idx)

    def scatter_rows(slot):
        def body(t, carry):
            src = _row(hbuf_ref.at[slot], t)
            pltpu.make_async_copy(src, _row(xs_hbm, idx_ref[slot, 0, t]), sem_rows.at[slot]).start()
            pltpu.make_async_copy(src, _row(xs_hbm, idx_ref[slot, 1, t]), sem_rows.at[slot]).start()
            return carry
        lax.fori_loop(0, tr, body, 0, unroll=ISSUE_UNROLL)

    def wait_rows(slot):
        def body(t, carry):
            pltpu.make_async_copy(_row(hbuf_ref.at[0], 0), _row(xs_hbm, 0), sem_rows.at[slot]).wait()
            return carry
        lax.fori_loop(0, 2 * tr, body, 0, unroll=ISSUE_UNROLL)

    @pl.when(i == 0)
    def _():
        for cp in tile_copies(0):
            cp.start()
        zero_ref[...] = jnp.zeros_like(zero_ref)
        _zero_fill(zinfo_ref, zero_ref, xs_hbm, sem_rows.at[1], wait=False)
        _zero_fill(zinfo_ref, zero_ref, xs_hbm, sem_rows.at[1], wait=True)

    @pl.when(i + 1 < nt)
    def _():
        for cp in tile_copies(i + 1):
            cp.start()

    for cp in tile_copies(i):
        cp.wait()

    for slot in range(RING):
        @pl.when(i % RING == slot)
        def _():
            scatter_rows(slot)

            @pl.when(i > 0)
            def _():
                wait_rows((slot - 1) % RING)

            @pl.when(i == nt - 1)
            def _():
                wait_rows(slot)


def _dispatch(h, dest, zoff, n_slots):
    t = h.shape[0] // ROW_SUB
    tr = ROW_TILE
    nt = t // tr
    grid_spec = pltpu.PrefetchScalarGridSpec(
        num_scalar_prefetch=1,
        grid=(nt,),
        in_specs=[pl.BlockSpec(memory_space=pl.ANY), pl.BlockSpec(memory_space=pl.ANY)],
        out_specs=pl.BlockSpec(memory_space=pl.ANY),
        scratch_shapes=[pltpu.SMEM((RING, 2, tr), I32), pltpu.VMEM((RING, tr * ROW_SUB, LANES), F32),
                        pltpu.VMEM((EXP_BLOCK * ROW_SUB, LANES), F32),
                        pltpu.SemaphoreType.DMA((RING,)), pltpu.SemaphoreType.DMA((RING,)),
                        pltpu.SemaphoreType.DMA((RING,))],
    )
    return pl.pallas_call(
        _dispatch_kernel,
        grid_spec=grid_spec,
        out_shape=jax.ShapeDtypeStruct((n_slots * ROW_SUB, LANES), F32),
        compiler_params=_cparams("arbitrary"),
        name="moe_dispatch",
    )(zoff, dest.reshape(2, nt, tr), h)


def _expert_kernel(be_ref, nused_ref, xs_ref, w1_ref, w3_ref, w2_ref, ys_ref, w1b_ref, w3b_ref, w2b_ref):
    i = pl.program_id(0)
    used = i < nused_ref[0]

    @pl.when(used & ((i == 0) | (be_ref[i] != be_ref[jnp.maximum(i - 1, 0)])))
    def _():
        w1b_ref[...] = w1_ref[0, 0].astype(BF16)
        w3b_ref[...] = w3_ref[0, 0].astype(BF16)
        w2b_ref[...] = w2_ref[0, 0].astype(BF16)

    @pl.when(used)
    def _():
        xb = _load_rows(xs_ref, 0, EXP_BLOCK, ROW_SUB, BF16)
        a = _dot(xb, w1b_ref[...])
        b = _dot(xb, w3b_ref[...])
        hid = (a * jax.nn.sigmoid(a) * b).astype(BF16)
        _store_rows(ys_ref, _dot(hid, w2b_ref[...]))

    @pl.when(i >= nused_ref[0])
    def _():
        ys_ref[...] = jnp.zeros_like(ys_ref)


def _experts(xs, block_e, n_used, w1, w3, w2, layer):
    bm = EXP_BLOCK
    n_slots = xs.shape[0] // ROW_SUB
    nblk = n_slots // bm
    d, ff = w1.shape[-2:]

    def last_used(i, nu):
        return jnp.maximum(jnp.minimum(i, nu[0] - 1), 0)

    def xmap(i, be, nu):
        return (last_used(i, nu), 0)

    def wmap(i, be, nu):
        return (layer, be[last_used(i, nu)], 0, 0)

    grid_spec = pltpu.PrefetchScalarGridSpec(
        num_scalar_prefetch=2,
        grid=(nblk,),
        in_specs=[pl.BlockSpec((bm * ROW_SUB, LANES), xmap), pl.BlockSpec((1, 1, d, ff), wmap),
                  pl.BlockSpec((1, 1, d, ff), wmap), pl.BlockSpec((1, 1, ff, d), wmap)],
        out_specs=pl.BlockSpec((bm * ROW_SUB, LANES), lambda i, be, nu: (i, 0)),
        scratch_shapes=[pltpu.VMEM((d, ff), BF16), pltpu.VMEM((d, ff), BF16), pltpu.VMEM((ff, d), BF16)],
    )
    return pl.pallas_call(
        _expert_kernel,
        grid_spec=grid_spec,
        out_shape=jax.ShapeDtypeStruct((n_slots * ROW_SUB, LANES), F32),
        compiler_params=_cparams("arbitrary"),
        name="moe_experts",
    )(block_e, n_used, xs, w1, w3, w2)


def _combine_tiles(dest_hbm, ys_hbm, x_ref, g_ref, o_ref, idx_ref, buf_ref, sem_idx, sem_rows, finish):
    i = pl.program_id(0)
    nt = pl.num_programs(0)
    tr = x_ref.shape[0]

    def gather(slot):
        buf = buf_ref.at[slot]

        def body(t, carry):
            pltpu.make_async_copy(_row(ys_hbm, idx_ref[slot, 0, t]), _row(buf, t), sem_rows.at[slot]).start()
            pltpu.make_async_copy(_row(ys_hbm, idx_ref[slot, 1, t]), _row(buf, tr + t), sem_rows.at[slot]).start()
            return carry

        lax.fori_loop(0, tr, body, 0, unroll=ISSUE_UNROLL)

    def wait_rows(slot):
        def body(t, carry):
            pltpu.make_async_copy(_row(ys_hbm, 0), _row(buf_ref.at[0], 0), sem_rows.at[slot]).wait()
            return carry
        lax.fori_loop(0, 2 * tr, body, 0, unroll=ISSUE_UNROLL)

    @pl.when(i == 0)
    def _():
        for cp in _slot_copies(dest_hbm, 0, idx_ref, 0, sem_idx):
            cp.start()
        for cp in _slot_copies(dest_hbm, 0, idx_ref, 0, sem_idx):
            cp.wait()
        gather(0)

        @pl.when(nt > 1)
        def _():
            for cp in _slot_copies(dest_hbm, 1, idx_ref, 1, sem_idx):
                cp.start()

    for slot in range(2):
        @pl.when(i % 2 == slot)
        def _():
            nxt = 1 - slot

            @pl.when(i + 1 < nt)
            def _():
                for cp in _slot_copies(dest_hbm, i + 1, idx_ref, nxt, sem_idx):
                    cp.wait()
                gather(nxt)

            @pl.when(i + 2 < nt)
            def _():
                for cp in _slot_copies(dest_hbm, i + 2, idx_ref, slot, sem_idx):
                    cp.start()

            wait_rows(slot)
            g = g_ref[...]
            y1 = _load_rows(buf_ref.at[slot], 0, tr, ROW_SUB, F32)
            y2 = _load_rows(buf_ref.at[slot], tr, tr, ROW_SUB, F32)
            o_ref[...] = finish(x_ref[...] + g[:, 0:1] * y1 + g[:, 1:2] * y2)


def _combine_scratch(tr):
    return [pltpu.SMEM((2, 2, tr), I32), pltpu.VMEM((2, 2 * tr * ROW_SUB, LANES), F32),
            pltpu.SemaphoreType.DMA((2,)), pltpu.SemaphoreType.DMA((2,))]


def _combine_norm_kernel(dest_hbm, ys_hbm, x_ref, g_ref, nw_ref, o_ref, *scratch):
    _combine_tiles(dest_hbm, ys_hbm, x_ref, g_ref, o_ref, *scratch, finish=lambda v: _rms(v, nw_ref[...]))


def _combine_norm(ys, dest, x, gates, nw):
    t, d = x.shape
    tr = ROW_TILE
    nt = t // tr
    return pl.pallas_call(
        _combine_norm_kernel,
        grid=(nt,),
        in_specs=[pl.BlockSpec(memory_space=pl.ANY), pl.BlockSpec(memory_space=pl.ANY),
                  pl.BlockSpec((tr, d), lambda i: (i, 0)), pl.BlockSpec((tr, 2), lambda i: (i, 0)),
                  pl.BlockSpec((1, d), lambda i: (0, 0))],
        out_specs=pl.BlockSpec((tr, d), lambda i: (i, 0)),
        out_shape=jax.ShapeDtypeStruct((t, d), F32),
        scratch_shapes=_combine_scratch(tr),
        compiler_params=_cparams("arbitrary"),
        name="moe_combine",
    )(dest.reshape(2, nt, tr), ys, x, gates.T, nw.reshape(1, d))


def _slots_kernel(ids_ref, pst_ref, dest_ref):
    ids = ids_ref[...]
    ei = lax.broadcasted_iota(I32, (MOE_EXPERTS, ids.shape[1]), 0)
    pst = pst_ref[...]
    d1 = jnp.sum(jnp.where(ei == ids[0:1], pst, 0), axis=0, keepdims=True) + ids[2:3]
    d2 = jnp.sum(jnp.where(ei == ids[1:2], pst, 0), axis=0, keepdims=True) + ids[3:4]
    dest_ref[...] = jnp.concatenate([d1, d2], axis=0)


def _slots(ids, pstarts):
    t = ids.shape[1]
    tm = TOK_TILE
    return pl.pallas_call(
        _slots_kernel,
        grid=(t // tm,),
        in_specs=[pl.BlockSpec((4, tm), lambda i: (0, i)), pl.BlockSpec((MOE_EXPERTS, 1), lambda i: (0, 0))],
        out_specs=pl.BlockSpec((2, tm), lambda i: (0, i)),
        out_shape=jax.ShapeDtypeStruct((2, t), I32),
        compiler_params=_cparams("arbitrary"),
        name="moe_slots",
    )(ids, pstarts.reshape(MOE_EXPERTS, 1))


def _combine_plain_kernel(dest_hbm, ys_hbm, x_ref, g_ref, o_ref, *scratch):
    _combine_tiles(dest_hbm, ys_hbm, x_ref, g_ref, o_ref, *scratch, finish=lambda v: v)


def _combine_plain(ys, dest, x, gates):
    t, d = x.shape
    tr = ROW_TILE
    nt = t // tr
    return pl.pallas_call(
        _combine_plain_kernel,
        grid=(nt,),
        in_specs=[pl.BlockSpec(memory_space=pl.ANY), pl.BlockSpec(memory_space=pl.ANY),
                  pl.BlockSpec((tr, d), lambda i: (i, 0)), pl.BlockSpec((tr, 2), lambda i: (i, 0))],
        out_specs=pl.BlockSpec((tr, d), lambda i: (i, 0)),
        out_shape=jax.ShapeDtypeStruct((t, d), F32),
        scratch_shapes=_combine_scratch(tr),
        compiler_params=_cparams("arbitrary"),
        name="moe_combine",
    )(dest.reshape(2, nt, tr), ys, x, gates.T)


def _moe(h, t, ids, counts, w1, w3, w2, layer):
    bm = EXP_BLOCK
    n_asg = 2 * t
    n_slots = n_asg + MOE_EXPERTS * bm
    cnt = counts[:, 0]
    padded = (cnt + bm - 1) // bm * bm
    pend = jnp.cumsum(padded)
    pstarts = (pend - padded).astype(I32)
    dest = _slots(ids, pstarts)
    nblk = n_slots // bm
    block_start = jnp.arange(nblk, dtype=I32) * bm
    block_e = jnp.minimum(jnp.sum(pend[None, :] <= block_start[:, None], axis=1), MOE_EXPERTS - 1).astype(I32)
    n_used = (pend[-1:] // bm).astype(I32)
    zinfo = jnp.concatenate([pstarts + cnt, padded - cnt, n_used]).astype(I32)
    xs = _dispatch(h, dest, zinfo, n_slots)
    ys = _experts(xs, block_e, n_used, w1, w3, w2, layer)
    return ys, dest


def _mla_in_combine_kernel(dest_hbm, ys_hbm, g_ref, x_ref, *rest):
    ins, (xc_ref, qt_ref, k_ref, vt_ref), scratch = rest[:12], rest[12:16], rest[16:]
    _combine_tiles(dest_hbm, ys_hbm, x_ref, g_ref, xc_ref, *scratch, finish=lambda v: v)
    _mla_in_kernel(xc_ref, *ins, qt_ref, k_ref, vt_ref)


def _mla_in_kernel(x_ref, pos_ref, nw_ref, wq_ref, wkv_ref, wks_ref, qn_ref, kvn_ref, wqn_ref, wqr_ref, wqs_ref,
                   wuk_ref, freq_ref, qt_ref, k_ref, vt_ref):
    tm = x_ref.shape[0]
    tb = ATT_BLOCK
    hb = _rms(x_ref[...], nw_ref[...]).astype(BF16)
    c_q = _rms(_dot(hb, wq_ref[...]), qn_ref[...]).astype(BF16)
    kv_t = _dot_nt(wkv_ref[...], hb)
    ks_t = _dot_nt(wks_ref[...], hb)
    c_kv = kv_t[:MLA_KV_LORA]
    c_kv = c_kv * lax.rsqrt(jnp.mean(c_kv * c_kv, axis=0, keepdims=True) + EPS) * kvn_ref[...]
    ang = freq_ref[...] * pos_ref[...].astype(F32)
    cos32, sin32 = jnp.cos(ang), jnp.sin(ang)
    cos = jnp.concatenate([cos32] * 4, axis=0)
    sin = jnp.concatenate([-sin32, sin32] * 2, axis=0)
    k_rope = kv_t[MLA_KV_LORA:] * cos + ks_t * sin
    k_ref[...] = jnp.concatenate([c_kv, k_rope], axis=0).T.astype(BF16)
    v_ext = jnp.concatenate([c_kv, jnp.ones((VT_ROWS - MLA_KV_LORA, tm), F32)], axis=0).astype(BF16)
    for c in range(tm // ATT_KV):
        vt_ref[c] = v_ext[:, c * ATT_KV:(c + 1) * ATT_KV]
    q_nope = _dot_nt(wqn_ref[...], c_q).astype(BF16)
    q_r = _dot_nt(wqr_ref[...], c_q)
    q_s = _dot_nt(wqs_ref[...], c_q)
    for h in range(MLA_HEADS):
        sl = slice(h * LANES, (h + 1) * LANES)
        q_lat = (_dot(wuk_ref[h], q_nope[sl]) * Q_SCALE).astype(BF16)
        q_rope = ((q_r[sl] * cos + q_s[sl] * sin) * Q_SCALE).astype(BF16)
        for c in range(tm // tb):
            cs = slice(c * tb, (c + 1) * tb)
            qt_ref[c, 2 * h * LANES:(2 * h + 1) * LANES, :] = q_lat[:, cs]
            qt_ref[c, (2 * h + 1) * LANES:(2 * h + 2) * LANES, :] = q_rope[:, cs]


def _swap_halves(w):
    half = w.shape[-1] // 2
    return jnp.concatenate([w[..., half:], w[..., :half]], axis=-1)


def _mla_in(x, pos, nw, w_in, q_norm, w_uq, kv_norm, w_uk, pending=None):
    t, d = x.shape
    tm = TOK_TILE
    tb = ATT_BLOCK
    nh = MLA_HEADS
    pad_r = LANES - MLA_ROPE
    w_kr = w_in[:, MLA_Q_LORA + MLA_KV_LORA:]
    wq = w_in[:, :MLA_Q_LORA].astype(BF16)
    wkv = jnp.pad(w_in[:, MLA_Q_LORA:], ((0, 0), (0, pad_r))).T.astype(BF16)
    wks = jnp.pad(_swap_halves(w_kr), ((0, 0), (0, pad_r))).T.astype(BF16)
    wqn = w_uq[:, :, :MLA_NOPE].reshape(MLA_Q_LORA, nh * MLA_NOPE).T.astype(BF16)
    w_r = w_uq[:, :, MLA_NOPE:]
    wqr = jnp.pad(w_r, ((0, 0), (0, 0), (0, pad_r))).reshape(MLA_Q_LORA, nh * LANES).T.astype(BF16)
    wqs = jnp.pad(_swap_halves(w_r), ((0, 0), (0, 0), (0, pad_r))).reshape(MLA_Q_LORA, nh * LANES).T.astype(BF16)
    wuk = w_uk.transpose(1, 0, 2).astype(BF16)
    inv_freq = 1.0 / (ROPE_THETA ** (jnp.arange(0, MLA_ROPE, 2, dtype=F32) / MLA_ROPE))
    full = lambda a: pl.BlockSpec(a.shape, lambda i: (0,) * a.ndim)
    args = (x, pos.reshape(1, t), nw.reshape(1, d), wq, wkv, wks, q_norm.reshape(1, -1), kv_norm.reshape(-1, 1),
            wqn, wqr, wqs, wuk, inv_freq.reshape(-1, 1))
    nsub = tm // tb
    nt = t // tm
    in_specs = [pl.BlockSpec((tm, d), lambda i: (i, 0)), pl.BlockSpec((1, tm), lambda i: (0, i))] \
        + [full(a) for a in args[2:]]
    out_specs = [pl.BlockSpec((nsub, 2 * nh * LANES, tb), lambda i: (i, 0, 0)),
                 pl.BlockSpec((tm, 2 * LANES), lambda i: (i, 0)),
                 pl.BlockSpec((tm // ATT_KV, VT_ROWS, ATT_KV), lambda i: (i, 0, 0))]
    out_shape = [jax.ShapeDtypeStruct((t // tb, 2 * nh * LANES, tb), BF16),
                 jax.ShapeDtypeStruct((t, 2 * LANES), BF16),
                 jax.ShapeDtypeStruct((t // ATT_KV, VT_ROWS, ATT_KV), BF16)]
    if pending is None:
        qt, k, vt = pl.pallas_call(
            _mla_in_kernel, grid=(nt,), in_specs=in_specs, out_specs=out_specs, out_shape=out_shape,
            compiler_params=_cparams("arbitrary"), name="mla_in",
        )(*args)
        return x, qt, k, vt
    ys, dest, gates = pending
    any_spec = pl.BlockSpec(memory_space=pl.ANY)
    return pl.pallas_call(
        _mla_in_combine_kernel,
        grid=(nt,),
        in_specs=[any_spec, any_spec, pl.BlockSpec((tm, 2), lambda i: (i, 0))] + in_specs,
        out_specs=[pl.BlockSpec((tm, d), lambda i: (i, 0))] + out_specs,
        out_shape=[jax.ShapeDtypeStruct((t, d), F32)] + out_shape,
        scratch_shapes=_combine_scratch(tm),
        compiler_params=_cparams("arbitrary"),
        name="mla_in_combine",
    )(dest.reshape(2, nt, tm), ys, gates.T, *args)


def _mla_attn_kernel(qt_ref, k_ref, vt_ref, wuv_ref, o_ref, *state):
    i = pl.program_id(1)
    tq = ATT_BLOCK
    tk = ATT_KV
    nh = MLA_HEADS
    dq = 2 * LANES
    s_refs, smax_refs, m_refs, acc_refs = state[0::4], state[1::4], state[2::4], state[3::4]
    last = (i * tq) // tk

    def scores(j, h):
        kb = k_ref[pl.ds(pl.multiple_of(j * tk, tk), tk), :]
        s = _dot(kb, qt_ref[0, h * dq:(h + 1) * dq, :])
        return s, jnp.max(s, axis=0, keepdims=True)

    def consume(j, h, s, s_max):
        m_old = m_refs[h][...]
        m_new = jnp.maximum(m_old, s_max)
        alpha = jnp.exp2(m_old - m_new)
        p = jnp.exp2(s - m_new).astype(BF16)
        acc_refs[h][...] = alpha * acc_refs[h][...] + _dot(vt_ref[j], p)
        m_refs[h][...] = m_new

    for h in range(nh):
        m_refs[h][...] = jnp.full_like(m_refs[h], -jnp.inf)
        acc_refs[h][...] = jnp.zeros_like(acc_refs[h])
        s_refs[h][...], smax_refs[h][...] = scores(0, h)

    def full_block(j, carry):
        for h in range(nh):
            s, s_max = s_refs[h][...], smax_refs[h][...]
            s_new, s_max_new = scores(j + 1, h)
            consume(j, h, s, s_max)
            s_refs[h][...], smax_refs[h][...] = s_new, s_max_new
        return carry

    lax.fori_loop(0, last, full_block, 0)
    kpos = lax.broadcasted_iota(I32, (tk, tq), 0) + last * tk
    qpos = lax.broadcasted_iota(I32, (tk, tq), 1) + i * tq
    for h in range(nh):
        s = jnp.where(kpos <= qpos, s_refs[h][...], -jnp.inf)
        consume(last, h, s, jnp.max(s, axis=0, keepdims=True))
    for h in range(nh):
        acc = acc_refs[h][...]
        o_lat = (acc[:MLA_KV_LORA] / acc[MLA_KV_LORA:MLA_KV_LORA + 1]).astype(BF16)
        o_ref[:, h * MLA_V:(h + 1) * MLA_V] = _dot_tn(o_lat, wuv_ref[h]).astype(BF16)


def _mla_attn(qt, k, vt, w_uv, bsz, seq):
    t = bsz * seq
    tq = ATT_BLOCK
    tk = ATT_KV
    nq = seq // tq
    nh = MLA_HEADS
    wuv = w_uv.transpose(1, 0, 2).astype(BF16)
    return pl.pallas_call(
        _mla_attn_kernel,
        grid=(bsz, nq),
        in_specs=[pl.BlockSpec((1, 2 * nh * LANES, tq), lambda b, i: (b * nq + i, 0, 0)),
                  pl.BlockSpec((seq, 2 * LANES), lambda b, i: (b, 0)),
                  pl.BlockSpec((seq // tk, VT_ROWS, tk), lambda b, i: (b, 0, 0)),
                  pl.BlockSpec(wuv.shape, lambda b, i: (0, 0, 0))],
        out_specs=pl.BlockSpec((tq, nh * MLA_V), lambda b, i: (b * nq + i, 0)),
        out_shape=jax.ShapeDtypeStruct((t, nh * MLA_V), BF16),
        scratch_shapes=[pltpu.VMEM((tk, tq), F32), pltpu.VMEM((1, tq), F32), pltpu.VMEM((1, tq), F32),
                        pltpu.VMEM((VT_ROWS, tq), F32)] * nh,
        compiler_params=_cparams("arbitrary", "arbitrary"),
        name="mla_attn",
    )(qt, k, vt, wuv)


def kernel(x, positions, attn_norm, ffn_norm, final_norm, gla_w_in, gla_w_gate, gla_b_gate, gla_out_norm, gla_w_o,
           mla_w_in, mla_q_norm, mla_w_uq, mla_kv_norm, mla_w_uk, mla_w_uv, mla_w_o,
           moe_w_group, moe_w_expert, moe_w1, moe_w3, moe_w2):
    bsz, seq, d = x.shape
    t = bsz * seq
    depth = attn_norm.shape[0]
    xf = x.reshape(t, d)
    pos = positions.reshape(t)
    pending = None
    for i in range(depth):
        j = i // 2
        if i % 2 == 0:
            if pending is not None:
                xf = _combine_plain(pending[0], pending[1], xf, pending[2])
            q, k, v, r, la = _gla_in(xf, attn_norm[i], gla_w_in[j], gla_w_gate[j], gla_b_gate[j])
            mixed = _gla_rec(q, k, v, r, la, gla_out_norm[j], bsz, seq)
            w_o = gla_w_o[j]
        else:
            xf, qt, kc, vt = _mla_in(xf, pos, attn_norm[i], mla_w_in[j], mla_q_norm[j], mla_w_uq[j],
                                     mla_kv_norm[j], mla_w_uk[j], pending)
            mixed = _mla_attn(qt, kc, vt, mla_w_uv[j], bsz, seq)
            w_o = mla_w_o[j]
        xf, h, ids, gates, counts = _post(mixed, w_o, xf, ffn_norm[i], moe_w_group[i], moe_w_expert[i])
        ys, dest = _moe(h, t, ids, counts, moe_w1, moe_w3, moe_w2, i)
        pending = (ys, dest, gates)
    out = _combine_norm(pending[0], pending[1], xf, pending[2], final_norm)
    return out.reshape(bsz, seq, d)
```

```python
import jax
import jax.numpy as jnp
from jax import lax
from jax.experimental import pallas as pl
from jax.experimental.pallas import tpu as pltpu

F32 = jnp.float32
BF16 = jnp.bfloat16
I32 = jnp.int32

EPS = 1e-6
GLA_HEADS = 4
GLA_DK = 128
GLA_DV = 256
GLA_GATE_RANK = 16
GLA_GATE_TAU = 16.0
GLA_CHUNK = 64
MLA_HEADS = 8
MLA_NOPE = 128
MLA_ROPE = 64
MLA_V = 128
MLA_Q_LORA = 256
MLA_KV_LORA = 128
MLA_SCALE = (MLA_NOPE + MLA_ROPE) ** -0.5
Q_SCALE = MLA_SCALE * 1.4426950408889634
VT_ROWS = MLA_KV_LORA + 16
ROPE_THETA = 10000.0
MOE_GROUPS = 8
MOE_PER_GROUP = 8
MOE_EXPERTS = MOE_GROUPS * MOE_PER_GROUP

D_MODEL = 1024
LANES = 128
ROW_SUB = D_MODEL // LANES
VMEM_LIMIT = 48 * 1024 * 1024

TOK_TILE = 512
POST_TILE = 1024
GLA_BLOCK = 256
ATT_BLOCK = 512
ATT_KV = 512
ROW_TILE = 512
EXP_BLOCK = 512


def _cparams(*sem):
    return pltpu.CompilerParams(dimension_semantics=sem, vmem_limit_bytes=VMEM_LIMIT)


def _rms(x, w):
    return x * lax.rsqrt(jnp.mean(x * x, axis=-1, keepdims=True) + EPS) * w


def _dot(a, b):
    return jnp.dot(a, b, preferred_element_type=F32)


def _dot_nt(a, b):
    return lax.dot_general(a, b, (((1,), (1,)), ((), ())), preferred_element_type=F32)


def _dot_tn(a, b):
    return lax.dot_general(a, b, (((0,), (0,)), ((), ())), preferred_element_type=F32)


def _split_bf16(x, pieces):
    out = []
    for _ in range(pieces - 1):
        hi = x.astype(BF16)
        out.append(hi)
        x = x - hi.astype(F32)
    out.append(x.astype(BF16))
    return out


def _gla_in_kernel(x_ref, nw_ref, wq_ref, wk_ref, wv_ref, wr_ref, wa_ref, wgh_ref, wgl_ref, bg_ref,
                   q_ref, k_ref, v_ref, r_ref, la_ref):
    hb = _rms(x_ref[...], nw_ref[...]).astype(BF16)
    a_hi, a_lo = _split_bf16(_dot(hb, wa_ref[...]), 2)
    z = _dot(a_hi, wgh_ref[...]) + _dot(a_hi, wgl_ref[...]) + _dot(a_lo, wgh_ref[...]) + bg_ref[...]
    log_sig = jnp.minimum(z, 0.0) - jnp.log1p(jnp.exp(-jnp.abs(z)))
    la_ref[...] = log_sig * (1.0 / GLA_GATE_TAU)
    q_ref[...] = _dot(hb, wq_ref[...]) * (GLA_DK ** -0.5)
    k_ref[...] = _dot(hb, wk_ref[...])
    v_ref[...] = _dot(hb, wv_ref[...]).astype(BF16)
    r_ref[...] = _dot(hb, wr_ref[...])


def _gla_in(x, nw, w_in, w_gate, b_gate):
    t, d = x.shape
    qk = GLA_HEADS * GLA_DK
    vd = GLA_HEADS * GLA_DV
    wq = w_in[:, :qk].astype(BF16)
    wk = w_in[:, qk:2 * qk].astype(BF16)
    wv = w_in[:, 2 * qk:2 * qk + vd].astype(BF16)
    wa = jnp.pad(w_in[:, 2 * qk + vd:2 * qk + vd + GLA_GATE_RANK], ((0, 0), (0, LANES - GLA_GATE_RANK))).astype(BF16)
    wr = w_in[:, 2 * qk + vd + GLA_GATE_RANK:].astype(BF16)
    wg = jnp.pad(w_gate, ((0, LANES - GLA_GATE_RANK), (0, 0)))
    wgh = wg.astype(BF16)
    wgl = (wg - wgh.astype(F32)).astype(BF16)
    tm = TOK_TILE
    row = lambda n: pl.BlockSpec((tm, n), lambda i: (i, 0))
    full = lambda a: pl.BlockSpec(a.shape, lambda i: (0, 0))
    nw2, bg2 = nw.reshape(1, d), b_gate.reshape(1, qk)
    return pl.pallas_call(
        _gla_in_kernel,
        grid=(t // tm,),
        in_specs=[row(d), full(nw2), full(wq), full(wk), full(wv), full(wr), full(wa), full(wgh), full(wgl),
                  full(bg2)],
        out_specs=[row(qk), row(qk), row(vd), row(vd), row(qk)],
        out_shape=[jax.ShapeDtypeStruct((t, qk), F32), jax.ShapeDtypeStruct((t, qk), F32),
                   jax.ShapeDtypeStruct((t, vd), BF16), jax.ShapeDtypeStruct((t, vd), F32),
                   jax.ShapeDtypeStruct((t, qk), F32)],
        compiler_params=_cparams("arbitrary"),
        name="gla_in",
    )(x, nw2, wq, wk, wv, wr, wa, wgh, wgl, bg2)


def _gla_rec_kernel(q_ref, k_ref, v_ref, r_ref, la_ref, onw_ref, o_ref, st_ref):
    c = GLA_CHUNK
    blk = GLA_BLOCK
    nc = blk // c

    @pl.when(pl.program_id(1) == 0)
    def _():
        st_ref[...] = jnp.zeros_like(st_ref)

    rows = lax.broadcasted_iota(I32, (blk, blk), 0)
    cols = lax.broadcasted_iota(I32, (blk, blk), 1)
    causal = (cols <= rows) & (cols // c == rows // c)
    tri = jnp.where(causal, 1.0, 0.0).astype(BF16)
    g = sum(_dot(tri, piece) for piece in _split_bf16(la_ref[...], 3))
    g_mid = jnp.concatenate(
        [jnp.broadcast_to(g[ci * c + c // 2:ci * c + c // 2 + 1], (c, g.shape[1])) for ci in range(nc)], axis=0)
    g_last = jnp.concatenate(
        [jnp.broadcast_to(g[ci * c + c - 1:ci * c + c], (c, g.shape[1])) for ci in range(nc)], axis=0)
    q = q_ref[...]
    k = k_ref[...]
    q_intra = (q * jnp.exp(g - g_mid)).astype(BF16)
    k_intra = (k * jnp.exp(g_mid - g)).astype(BF16)
    q_inter = (q * jnp.exp(g)).astype(BF16)
    k_state = (k * jnp.exp(g_last - g)).astype(BF16)
    onw = onw_ref[...]
    heads = range(GLA_HEADS)
    ksl = [slice(h * GLA_DK, (h + 1) * GLA_DK) for h in heads]
    vsl = [slice(h * GLA_DV, (h + 1) * GLA_DV) for h in heads]
    v = [v_ref[:, vsl[h]] for h in heads]
    a = [jnp.where(causal, _dot_nt(q_intra[:, ksl[h]], k_intra[:, ksl[h]]), 0.0).astype(BF16) for h in heads]
    kv = [[_dot_tn(v[h][ci * c:(ci + 1) * c], k_state[ci * c:(ci + 1) * c, ksl[h]]) for ci in range(nc)]
          for h in heads]
    o_intra = [_dot(a[h], v[h]) for h in heads]
    for h in heads:
        st = st_ref[h]
        o_inter = []
        for ci in range(nc):
            o_inter.append(_dot_nt(q_inter[ci * c:(ci + 1) * c, ksl[h]], st.astype(BF16)))
            st = st * jnp.exp(g[ci * c + c - 1:ci * c + c, ksl[h]]) + kv[h][ci]
        st_ref[h] = st
        o = _rms(o_intra[h] + jnp.concatenate(o_inter, axis=0), onw)
        r = r_ref[:, vsl[h]]
        o_ref[:, vsl[h]] = (o * (r * jax.nn.sigmoid(r))).astype(BF16)


def _gla_rec(q, k, v, r, la, onw, bsz, seq):
    t = bsz * seq
    qk = GLA_HEADS * GLA_DK
    vd = GLA_HEADS * GLA_DV
    nb = seq // GLA_BLOCK
    row = lambda n: pl.BlockSpec((GLA_BLOCK, n), lambda b, j: (b * nb + j, 0))
    onw2 = onw.reshape(1, GLA_DV)
    return pl.pallas_call(
        _gla_rec_kernel,
        grid=(bsz, nb),
        in_specs=[row(qk), row(qk), row(vd), row(vd), row(qk), pl.BlockSpec((1, GLA_DV), lambda b, j: (0, 0))],
        out_specs=row(vd),
        out_shape=jax.ShapeDtypeStruct((t, vd), BF16),
        scratch_shapes=[pltpu.VMEM((GLA_HEADS, GLA_DV, GLA_DK), F32)],
        compiler_params=_cparams("arbitrary", "arbitrary"),
        name="gla_rec",
    )(q, k, v, r, la, onw2)


def _store_rows(ref, val):
    n, d = val.shape
    ref[...] = val.reshape(n * d // LANES, LANES)


def _load_rows(ref, start, n, nc, dtype):
    return jnp.concatenate([ref[pl.ds(start * nc + c, n, stride=nc), :].astype(dtype) for c in range(nc)], axis=1)


def _post_kernel(a_ref, wo_ref, x_ref, nw_ref, wrh_ref, wrl_ref,
                 x1_ref, h_ref, ids_ref, gates_ref, cnt_ref, carry_ref):
    tm = a_ref.shape[0]
    ne = MOE_EXPERTS

    @pl.when(pl.program_id(0) == 0)
    def _():
        carry_ref[...] = jnp.zeros_like(carry_ref)

    x1 = x_ref[...] + _dot(a_ref[...], wo_ref[...])
    x1_ref[...] = x1
    h = _rms(x1, nw_ref[...])
    _store_rows(h_ref, h)
    h_hi, h_lo = _split_bf16(h, 2)
    lt = _dot_nt(wrh_ref[...], h_hi) + _dot_nt(wrh_ref[...], h_lo) + _dot_nt(wrl_ref[...], h_hi)
    gl = lt[0:MOE_GROUPS]
    gmax = jnp.max(gl, axis=0, keepdims=True)
    gi = lax.broadcasted_iota(I32, gl.shape, 0)
    g_sel = jnp.min(jnp.where(gl == gmax, gi, MOE_GROUPS), axis=0, keepdims=True)
    g_w = 1.0 / jnp.sum(jnp.exp(gl - gmax), axis=0, keepdims=True)
    el = lt[MOE_GROUPS:MOE_GROUPS + ne]
    ei = lax.broadcasted_iota(I32, el.shape, 0)
    in_group = (ei // MOE_PER_GROUP) == g_sel
    neg = jnp.float32(-jnp.inf)
    el1 = jnp.where(in_group, el, neg)
    l1 = jnp.max(el1, axis=0, keepdims=True)
    i1 = jnp.min(jnp.where(el1 == l1, ei, ne), axis=0, keepdims=True)
    el2 = jnp.where(ei == i1, neg, el1)
    l2 = jnp.max(el2, axis=0, keepdims=True)
    i2 = jnp.min(jnp.where(el2 == l2, ei, ne), axis=0, keepdims=True)
    e2 = jnp.exp(l2 - l1)
    gate1 = g_w / (1.0 + e2)
    gate2 = g_w * e2 / (1.0 + e2)
    oh1 = ei == i1
    oh2 = ei == i2
    both = jnp.where(oh1 | oh2, 1.0, 0.0)
    su = lax.broadcasted_iota(I32, (tm, tm), 0)
    tu = lax.broadcasted_iota(I32, (tm, tm), 1)
    upper = jnp.where(su < tu, 1.0, 0.0).astype(BF16)
    prefix = _dot(both.astype(BF16), upper) + carry_ref[:, 0:1]
    rank1 = jnp.sum(jnp.where(oh1, prefix, 0.0), axis=0, keepdims=True)
    rank2 = jnp.sum(jnp.where(oh2, prefix, 0.0), axis=0, keepdims=True)
    carry = carry_ref[...] + jnp.sum(both, axis=1, keepdims=True)
    carry_ref[...] = carry
    cnt_ref[...] = carry.astype(I32)
    ids_ref[...] = jnp.concatenate([i1, i2, rank1.astype(I32), rank2.astype(I32)], axis=0)
    gates_ref[...] = jnp.concatenate([gate1, gate2], axis=0)


def _post(a, wo, x, nw, w_group, w_expert):
    t, d = x.shape
    tm = POST_TILE
    wr = jnp.concatenate([w_group.T, w_expert.T], axis=0)
    wr = jnp.pad(wr, ((0, LANES - wr.shape[0]), (0, 0)))
    wrh = wr.astype(BF16)
    wrl = (wr - wrh.astype(F32)).astype(BF16)
    wo = wo.astype(BF16)
    nw2 = nw.reshape(1, d)
    row = lambda n: pl.BlockSpec((tm, n), lambda i: (i, 0))
    full = lambda arr: pl.BlockSpec(arr.shape, lambda i: (0, 0))
    return pl.pallas_call(
        _post_kernel,
        grid=(t // tm,),
        in_specs=[row(a.shape[1]), full(wo), row(d), full(nw2), full(wrh), full(wrl)],
        out_specs=[row(d), pl.BlockSpec((tm * d // LANES, LANES), lambda i: (i, 0)),
                   pl.BlockSpec((4, tm), lambda i: (0, i)), pl.BlockSpec((2, tm), lambda i: (0, i)),
                   pl.BlockSpec((MOE_EXPERTS, LANES), lambda i: (0, 0))],
        out_shape=[jax.ShapeDtypeStruct((t, d), F32), jax.ShapeDtypeStruct((t * d // LANES, LANES), F32),
                   jax.ShapeDtypeStruct((4, t), I32), jax.ShapeDtypeStruct((2, t), F32),
                   jax.ShapeDtypeStruct((MOE_EXPERTS, LANES), I32)],
        scratch_shapes=[pltpu.VMEM((MOE_EXPERTS, LANES), F32)],
        compiler_params=_cparams("arbitrary"),
        name="post_router",
    )(a, wo, x, nw2, wrh, wrl)


def _row(ref, i):
    return ref.at[pl.ds(pl.multiple_of(i * ROW_SUB, ROW_SUB), ROW_SUB)]


def _zero_fill(zinfo_ref, zero_ref, xs_hbm, sem, wait):
    ne = MOE_EXPERTS
    nblk = xs_hbm.shape[0] // (EXP_BLOCK * ROW_SUB)

    def copy(off_rows, n_rows):
        off = pl.multiple_of(off_rows * ROW_SUB, ROW_SUB)
        cp = pltpu.make_async_copy(zero_ref.at[pl.ds(0, n_rows * ROW_SUB)],
                                   xs_hbm.at[pl.ds(off, n_rows * ROW_SUB)], sem)
        cp.wait() if wait else cp.start()

    def expert(e, carry):
        pad = zinfo_ref[ne + e]
        for b in range(EXP_BLOCK.bit_length() - 1):
            @pl.when(((pad >> b) & 1) == 1)
            def _():
                copy(zinfo_ref[e] + (pad & ((1 << b) - 1)), 1 << b)
        return carry

    def tail(kk, carry):
        blk = zinfo_ref[2 * ne] + kk

        @pl.when(blk < nblk)
        def _():
            copy(blk * EXP_BLOCK, EXP_BLOCK)
        return carry

    lax.fori_loop(0, ne, expert, 0)
    lax.fori_loop(0, ne + 1, tail, 0)


RING = 3
ISSUE_UNROLL = 8


def _slot_copies(dest_hbm, i, idx_ref, slot, sem):
    return [pltpu.make_async_copy(dest_hbm.at[a, i], idx_ref.at[slot, a], sem.at[slot]) for a in range(2)]


def _dispatch_kernel(zinfo_ref, dest_hbm, h_hbm, xs_hbm, idx_ref, hbuf_ref, zero_ref, sem_idx, sem_h, sem_rows):
    i = pl.program_id(0)
    nt = pl.num_programs(0)
    tr = ROW_TILE
    rows = tr * ROW_SUB

    def tile_copies(j):
        slot = j % RING
        src = h_hbm.at[pl.ds(pl.multiple_of(j * rows, rows), rows)]
        return [pltpu.make_async_copy(src, hbuf_ref.at[slot], sem_h.at[slot])] + \
            _slot_copies(dest_hbm, j, idx_ref, slot, sem_idx)

    def scatter_rows(slot):
        def body(t, carry):
            src = _row(hbuf_ref.at[slot], t)
            pltpu.make_async_copy(src, _row(xs_hbm, idx_ref[slot, 0, t]), sem_rows.at[slot]).start()
            pltpu.make_async_copy(src, _row(xs_hbm, idx_ref[slot, 1, t]), sem_rows.at[slot]).start()
            return carry
        lax.fori_loop(0, tr, body, 0, unroll=ISSUE_UNROLL)

    def wait_rows(slot):
        def body(t, carry):
            pltpu.make_async_copy(_row(hbuf_ref.at[0], 0), _row(xs_hbm, 0), sem_rows.at[slot]).wait()
            return carry
        lax.fori_loop(0, 2 * tr, body, 0, unroll=ISSUE_UNROLL)

    @pl.when(i == 0)
    def _():
        for cp in tile_copies(0):
            cp.start()
        zero_ref[...] = jnp.zeros_like(zero_ref)
        _zero_fill(zinfo_ref, zero_ref, xs_hbm, sem_rows.at[1], wait=False)
        _zero_fill(zinfo_ref, zero_ref, xs_hbm, sem_rows.at[1], wait=True)

    @pl.when(i + 1 < nt)
    def _():
        for cp in tile_copies(i + 1):
            cp.start()

    for cp in tile_copies(i):
        cp.wait()

    for slot in range(RING):
        @pl.when(i % RING == slot)
        def _():
            scatter_rows(slot)

            @pl.when(i > 0)
            def _():
                wait_rows((slot - 1) % RING)

            @pl.when(i == nt - 1)
            def _():
                wait_rows(slot)


def _dispatch(h, dest, zoff, n_slots):
    t = h.shape[0] // ROW_SUB
    tr = ROW_TILE
    nt = t // tr
    grid_spec = pltpu.PrefetchScalarGridSpec(
        num_scalar_prefetch=1,
        grid=(nt,),
        in_specs=[pl.BlockSpec(memory_space=pl.ANY), pl.BlockSpec(memory_space=pl.ANY)],
        out_specs=pl.BlockSpec(memory_space=pl.ANY),
        scratch_shapes=[pltpu.SMEM((RING, 2, tr), I32), pltpu.VMEM((RING, tr * ROW_SUB, LANES), F32),
                        pltpu.VMEM((EXP_BLOCK * ROW_SUB, LANES), F32),
                        pltpu.SemaphoreType.DMA((RING,)), pltpu.SemaphoreType.DMA((RING,)),
                        pltpu.SemaphoreType.DMA((RING,))],
    )
    return pl.pallas_call(
        _dispatch_kernel,
        grid_spec=grid_spec,
        out_shape=jax.ShapeDtypeStruct((n_slots * ROW_SUB, LANES), F32),
        compiler_params=_cparams("arbitrary"),
        name="moe_dispatch",
    )(zoff, dest.reshape(2, nt, tr), h)


def _expert_kernel(be_ref, nused_ref, xs_ref, w1_ref, w3_ref, w2_ref, ys_ref, w1b_ref, w3b_ref, w2b_ref):
    i = pl.program_id(0)
    used = i < nused_ref[0]

    @pl.when(used & ((i == 0) | (be_ref[i] != be_ref[jnp.maximum(i - 1, 0)])))
    def _():
        w1b_ref[...] = w1_ref[0, 0].astype(BF16)
        w3b_ref[...] = w3_ref[0, 0].astype(BF16)
        w2b_ref[...] = w2_ref[0, 0].astype(BF16)

    @pl.when(used)
    def _():
        xb = _load_rows(xs_ref, 0, EXP_BLOCK, ROW_SUB, BF16)
        a = _dot(xb, w1b_ref[...])
        b = _dot(xb, w3b_ref[...])
        hid = (a * jax.nn.sigmoid(a) * b).astype(BF16)
        _store_rows(ys_ref, _dot(hid, w2b_ref[...]))

    @pl.when(i >= nused_ref[0])
    def _():
        ys_ref[...] = jnp.zeros_like(ys_ref)


def _experts(xs, block_e, n_used, w1, w3, w2, layer):
    bm = EXP_BLOCK
    n_slots = xs.shape[0] // ROW_SUB
    nblk = n_slots // bm
    d, ff = w1.shape[-2:]

    def last_used(i, nu):
        return jnp.maximum(jnp.minimum(i, nu[0] - 1), 0)

    def xmap(i, be, nu):
        return (last_used(i, nu), 0)

    def wmap(i, be, nu):
        return (layer, be[last_used(i, nu)], 0, 0)

    grid_spec = pltpu.PrefetchScalarGridSpec(
        num_scalar_prefetch=2,
        grid=(nblk,),
        in_specs=[pl.BlockSpec((bm * ROW_SUB, LANES), xmap), pl.BlockSpec((1, 1, d, ff), wmap),
                  pl.BlockSpec((1, 1, d, ff), wmap), pl.BlockSpec((1, 1, ff, d), wmap)],
        out_specs=pl.BlockSpec((bm * ROW_SUB, LANES), lambda i, be, nu: (i, 0)),
        scratch_shapes=[pltpu.VMEM((d, ff), BF16), pltpu.VMEM((d, ff), BF16), pltpu.VMEM((ff, d), BF16)],
    )
    return pl.pallas_call(
        _expert_kernel,
        grid_spec=grid_spec,
        out_shape=jax.ShapeDtypeStruct((n_slots * ROW_SUB, LANES), F32),
        compiler_params=_cparams("arbitrary"),
        name="moe_experts",
    )(block_e, n_used, xs, w1, w3, w2)


def _combine_tiles(dest_hbm, ys_hbm, x_ref, g_ref, o_ref, idx_ref, buf_ref, sem_idx, sem_rows, finish):
    i = pl.program_id(0)
    nt = pl.num_programs(0)
    tr = x_ref.shape[0]

    def gather(slot):
        buf = buf_ref.at[slot]

        def body(t, carry):
            pltpu.make_async_copy(_row(ys_hbm, idx_ref[slot, 0, t]), _row(buf, t), sem_rows.at[slot]).start()
            pltpu.make_async_copy(_row(ys_hbm, idx_ref[slot, 1, t]), _row(buf, tr + t), sem_rows.at[slot]).start()
            return carry

        lax.fori_loop(0, tr, body, 0, unroll=ISSUE_UNROLL)

    def wait_rows(slot):
        def body(t, carry):
            pltpu.make_async_copy(_row(ys_hbm, 0), _row(buf_ref.at[0], 0), sem_rows.at[slot]).wait()
            return carry
        lax.fori_loop(0, 2 * tr, body, 0, unroll=ISSUE_UNROLL)

    @pl.when(i == 0)
    def _():
        for cp in _slot_copies(dest_hbm, 0, idx_ref, 0, sem_idx):
            cp.start()
        for cp in _slot_copies(dest_hbm, 0, idx_ref, 0, sem_idx):
            cp.wait()
        gather(0)

        @pl.when(nt > 1)
        def _():
            for cp in _slot_copies(dest_hbm, 1, idx_ref, 1, sem_idx):
                cp.start()

    for slot in range(2):
        @pl.when(i % 2 == slot)
        def _():
            nxt = 1 - slot

            @pl.when(i + 1 < nt)
            def _():
                for cp in _slot_copies(dest_hbm, i + 1, idx_ref, nxt, sem_idx):
                    cp.wait()
                gather(nxt)

            @pl.when(i + 2 < nt)
            def _():
                for cp in _slot_copies(dest_hbm, i + 2, idx_ref, slot, sem_idx):
                    cp.start()

            wait_rows(slot)
            g = g_ref[...]
            y1 = _load_rows(buf_ref.at[slot], 0, tr, ROW_SUB, F32)
            y2 = _load_rows(buf_ref.at[slot], tr, tr, ROW_SUB, F32)
            o_ref[...] = finish(x_ref[...] + g[:, 0:1] * y1 + g[:, 1:2] * y2)


def _combine_scratch(tr):
    return [pltpu.SMEM((2, 2, tr), I32), pltpu.VMEM((2, 2 * tr * ROW_SUB, LANES), F32),
            pltpu.SemaphoreType.DMA((2,)), pltpu.SemaphoreType.DMA((2,))]


def _combine_norm_kernel(dest_hbm, ys_hbm, x_ref, g_ref, nw_ref, o_ref, *scratch):
    _combine_tiles(dest_hbm, ys_hbm, x_ref, g_ref, o_ref, *scratch, finish=lambda v: _rms(v, nw_ref[...]))


def _combine_norm(ys, dest, x, gates, nw):
    t, d = x.shape
    tr = ROW_TILE
    nt = t // tr
    return pl.pallas_call(
        _combine_norm_kernel,
        grid=(nt,),
        in_specs=[pl.BlockSpec(memory_space=pl.ANY), pl.BlockSpec(memory_space=pl.ANY),
                  pl.BlockSpec((tr, d), lambda i: (i, 0)), pl.BlockSpec((tr, 2), lambda i: (i, 0)),
                  pl.BlockSpec((1, d), lambda i: (0, 0))],
        out_specs=pl.BlockSpec((tr, d), lambda i: (i, 0)),
        out_shape=jax.ShapeDtypeStruct((t, d), F32),
        scratch_shapes=_combine_scratch(tr),
        compiler_params=_cparams("arbitrary"),
        name="moe_combine",
    )(dest.reshape(2, nt, tr), ys, x, gates.T, nw.reshape(1, d))


def _slots_kernel(ids_ref, pst_ref, dest_ref):
    ids = ids_ref[...]
    ei = lax.broadcasted_iota(I32, (MOE_EXPERTS, ids.shape[1]), 0)
    pst = pst_ref[...]
    d1 = jnp.sum(jnp.where(ei == ids[0:1], pst, 0), axis=0, keepdims=True) + ids[2:3]
    d2 = jnp.sum(jnp.where(ei == ids[1:2], pst, 0), axis=0, keepdims=True) + ids[3:4]
    dest_ref[...] = jnp.concatenate([d1, d2], axis=0)


def _slots(ids, pstarts):
    t = ids.shape[1]
    tm = TOK_TILE
    return pl.pallas_call(
        _slots_kernel,
        grid=(t // tm,),
        in_specs=[pl.BlockSpec((4, tm), lambda i: (0, i)), pl.BlockSpec((MOE_EXPERTS, 1), lambda i: (0, 0))],
        out_specs=pl.BlockSpec((2, tm), lambda i: (0, i)),
        out_shape=jax.ShapeDtypeStruct((2, t), I32),
        compiler_params=_cparams("arbitrary"),
        name="moe_slots",
    )(ids, pstarts.reshape(MOE_EXPERTS, 1))


def _combine_plain_kernel(dest_hbm, ys_hbm, x_ref, g_ref, o_ref, *scratch):
    _combine_tiles(dest_hbm, ys_hbm, x_ref, g_ref, o_ref, *scratch, finish=lambda v: v)


def _combine_plain(ys, dest, x, gates):
    t, d = x.shape
    tr = ROW_TILE
    nt = t // tr
    return pl.pallas_call(
        _combine_plain_kernel,
        grid=(nt,),
        in_specs=[pl.BlockSpec(memory_space=pl.ANY), pl.BlockSpec(memory_space=pl.ANY),
                  pl.BlockSpec((tr, d), lambda i: (i, 0)), pl.BlockSpec((tr, 2), lambda i: (i, 0))],
        out_specs=pl.BlockSpec((tr, d), lambda i: (i, 0)),
        out_shape=jax.ShapeDtypeStruct((t, d), F32),
        scratch_shapes=_combine_scratch(tr),
        compiler_params=_cparams("arbitrary"),
        name="moe_combine",
    )(dest.reshape(2, nt, tr), ys, x, gates.T)


def _moe(h, t, ids, counts, w1, w3, w2, layer):
    bm = EXP_BLOCK
    n_asg = 2 * t
    n_slots = n_asg + MOE_EXPERTS * bm
    cnt = counts[:, 0]
    padded = (cnt + bm - 1) // bm * bm
    pend = jnp.cumsum(padded)
    pstarts = (pend - padded).astype(I32)
    dest = _slots(ids, pstarts)
    nblk = n_slots // bm
    block_start = jnp.arange(nblk, dtype=I32) * bm
    block_e = jnp.minimum(jnp.sum(pend[None, :] <= block_start[:, None], axis=1), MOE_EXPERTS - 1).astype(I32)
    n_used = (pend[-1:] // bm).astype(I32)
    zinfo = jnp.concatenate([pstarts + cnt, padded - cnt, n_used]).astype(I32)
    xs = _dispatch(h, dest, zinfo, n_slots)
    ys = _experts(xs, block_e, n_used, w1, w3, w2, layer)
    return ys, dest


def _mla_in_combine_kernel(dest_hbm, ys_hbm, g_ref, x_ref, *rest):
    ins, (xc_ref, qt_ref, k_ref, vt_ref), scratch = rest[:12], rest[12:16], rest[16:]
    _combine_tiles(dest_hbm, ys_hbm, x_ref, g_ref, xc_ref, *scratch, finish=lambda v: v)
    _mla_in_kernel(xc_ref, *ins, qt_ref, k_ref, vt_ref)


def _mla_in_kernel(x_ref, pos_ref, nw_ref, wq_ref, wkv_ref, wks_ref, qn_ref, kvn_ref, wqn_ref, wqr_ref, wqs_ref,
                   wuk_ref, freq_ref, qt_ref, k_ref, vt_ref):
    tm = x_ref.shape[0]
    tb = ATT_BLOCK
    hb = _rms(x_ref[...], nw_ref[...]).astype(BF16)
    c_q = _rms(_dot(hb, wq_ref[...]), qn_ref[...]).astype(BF16)
    kv_t = _dot_nt(wkv_ref[...], hb)
    ks_t = _dot_nt(wks_ref[...], hb)
    c_kv = kv_t[:MLA_KV_LORA]
    c_kv = c_kv * lax.rsqrt(jnp.mean(c_kv * c_kv, axis=0, keepdims=True) + EPS) * kvn_ref[...]
    ang = freq_ref[...] * pos_ref[...].astype(F32)
    cos32, sin32 = jnp.cos(ang), jnp.sin(ang)
    cos = jnp.concatenate([cos32] * 4, axis=0)
    sin = jnp.concatenate([-sin32, sin32] * 2, axis=0)
    k_rope = kv_t[MLA_KV_LORA:] * cos + ks_t * sin
    k_ref[...] = jnp.concatenate([c_kv, k_rope], axis=0).T.astype(BF16)
    v_ext = jnp.concatenate([c_kv, jnp.ones((VT_ROWS - MLA_KV_LORA, tm), F32)], axis=0).astype(BF16)
    for c in range(tm // ATT_KV):
        vt_ref[c] = v_ext[:, c * ATT_KV:(c + 1) * ATT_KV]
    q_nope = _dot_nt(wqn_ref[...], c_q).astype(BF16)
    q_r = _dot_nt(wqr_ref[...], c_q)
    q_s = _dot_nt(wqs_ref[...], c_q)
    for h in range(MLA_HEADS):
        sl = slice(h * LANES, (h + 1) * LANES)
        q_lat = (_dot(wuk_ref[h], q_nope[sl]) * Q_SCALE).astype(BF16)
        q_rope = ((q_r[sl] * cos + q_s[sl] * sin) * Q_SCALE).astype(BF16)
        for c in range(tm // tb):
            cs = slice(c * tb, (c + 1) * tb)
            qt_ref[c, 2 * h * LANES:(2 * h + 1) * LANES, :] = q_lat[:, cs]
            qt_ref[c, (2 * h + 1) * LANES:(2 * h + 2) * LANES, :] = q_rope[:, cs]


def _swap_halves(w):
    half = w.shape[-1] // 2
    return jnp.concatenate([w[..., half:], w[..., :half]], axis=-1)


def _mla_in(x, pos, nw, w_in, q_norm, w_uq, kv_norm, w_uk, pending=None):
    t, d = x.shape
    tm = TOK_TILE
    tb = ATT_BLOCK
    nh = MLA_HEADS
    pad_r = LANES - MLA_ROPE
    w_kr = w_in[:, MLA_Q_LORA + MLA_KV_LORA:]
    wq = w_in[:, :MLA_Q_LORA].astype(BF16)
    wkv = jnp.pad(w_in[:, MLA_Q_LORA:], ((0, 0), (0, pad_r))).T.astype(BF16)
    wks = jnp.pad(_swap_halves(w_kr), ((0, 0), (0, pad_r))).T.astype(BF16)
    wqn = w_uq[:, :, :MLA_NOPE].reshape(MLA_Q_LORA, nh * MLA_NOPE).T.astype(BF16)
    w_r = w_uq[:, :, MLA_NOPE:]
    wqr = jnp.pad(w_r, ((0, 0), (0, 0), (0, pad_r))).reshape(MLA_Q_LORA, nh * LANES).T.astype(BF16)
    wqs = jnp.pad(_swap_halves(w_r), ((0, 0), (0, 0), (0, pad_r))).reshape(MLA_Q_LORA, nh * LANES).T.astype(BF16)
    wuk = w_uk.transpose(1, 0, 2).astype(BF16)
    inv_freq = 1.0 / (ROPE_THETA ** (jnp.arange(0, MLA_ROPE, 2, dtype=F32) / MLA_ROPE))
    full = lambda a: pl.BlockSpec(a.shape, lambda i: (0,) * a.ndim)
    args = (x, pos.reshape(1, t), nw.reshape(1, d), wq, wkv, wks, q_norm.reshape(1, -1), kv_norm.reshape(-1, 1),
            wqn, wqr, wqs, wuk, inv_freq.reshape(-1, 1))
    nsub = tm // tb
    nt = t // tm
    in_specs = [pl.BlockSpec((tm, d), lambda i: (i, 0)), pl.BlockSpec((1, tm), lambda i: (0, i))] \
        + [full(a) for a in args[2:]]
    out_specs = [pl.BlockSpec((nsub, 2 * nh * LANES, tb), lambda i: (i, 0, 0)),
                 pl.BlockSpec((tm, 2 * LANES), lambda i: (i, 0)),
                 pl.BlockSpec((tm // ATT_KV, VT_ROWS, ATT_KV), lambda i: (i, 0, 0))]
    out_shape = [jax.ShapeDtypeStruct((t // tb, 2 * nh * LANES, tb), BF16),
                 jax.ShapeDtypeStruct((t, 2 * LANES), BF16),
                 jax.ShapeDtypeStruct((t // ATT_KV, VT_ROWS, ATT_KV), BF16)]
    if pending is None:
        qt, k, vt = pl.pallas_call(
            _mla_in_kernel, grid=(nt,), in_specs=in_specs, out_specs=out_specs, out_shape=out_shape,
            compiler_params=_cparams("arbitrary"), name="mla_in",
        )(*args)
        return x, qt, k, vt
    ys, dest, gates = pending
    any_spec = pl.BlockSpec(memory_space=pl.ANY)
    return pl.pallas_call(
        _mla_in_combine_kernel,
        grid=(nt,),
        in_specs=[any_spec, any_spec, pl.BlockSpec((tm, 2), lambda i: (i, 0))] + in_specs,
        out_specs=[pl.BlockSpec((tm, d), lambda i: (i, 0))] + out_specs,
        out_shape=[jax.ShapeDtypeStruct((t, d), F32)] + out_shape,
        scratch_shapes=_combine_scratch(tm),
        compiler_params=_cparams("arbitrary"),
        name="mla_in_combine",
    )(dest.reshape(2, nt, tm), ys, gates.T, *args)


def _mla_attn_kernel(qt_ref, k_ref, vt_ref, wuv_ref, o_ref, *state):
    i = pl.program_id(1)
    tq = ATT_BLOCK
    tk = ATT_KV
    nh = MLA_HEADS
    dq = 2 * LANES
    s_refs, smax_refs, m_refs, acc_refs = state[0::4], state[1::4], state[2::4], state[3::4]
    last = (i * tq) // tk

    def scores(j, h):
        kb = k_ref[pl.ds(pl.multiple_of(j * tk, tk), tk), :]
        s = _dot(kb, qt_ref[0, h * dq:(h + 1) * dq, :])
        return s, jnp.max(s, axis=0, keepdims=True)

    def consume(j, h, s, s_max):
        m_old = m_refs[h][...]
        m_new = jnp.maximum(m_old, s_max)
        alpha = jnp.exp2(m_old - m_new)
        p = jnp.exp2(s - m_new).astype(BF16)
        acc_refs[h][...] = alpha * acc_refs[h][...] + _dot(vt_ref[j], p)
        m_refs[h][...] = m_new

    for h in range(nh):
        m_refs[h][...] = jnp.full_like(m_refs[h], -jnp.inf)
        acc_refs[h][...] = jnp.zeros_like(acc_refs[h])
        s_refs[h][...], smax_refs[h][...] = scores(0, h)

    def full_block(j, carry):
        for h in range(nh):
            s, s_max = s_refs[h][...], smax_refs[h][...]
            s_new, s_max_new = scores(j + 1, h)
            consume(j, h, s, s_max)
            s_refs[h][...], smax_refs[h][...] = s_new, s_max_new
        return carry

    lax.fori_loop(0, last, full_block, 0)
    kpos = lax.broadcasted_iota(I32, (tk, tq), 0) + last * tk
    qpos = lax.broadcasted_iota(I32, (tk, tq), 1) + i * tq
    for h in range(nh):
        s = jnp.where(kpos <= qpos, s_refs[h][...], -jnp.inf)
        consume(last, h, s, jnp.max(s, axis=0, keepdims=True))
    for h in range(nh):
        acc = acc_refs[h][...]
        o_lat = (acc[:MLA_KV_LORA] / acc[MLA_KV_LORA:MLA_KV_LORA + 1]).astype(BF16)
        o_ref[:, h * MLA_V:(h + 1) * MLA_V] = _dot_tn(o_lat, wuv_ref[h]).astype(BF16)


def _mla_attn(qt, k, vt, w_uv, bsz, seq):
    t = bsz * seq
    tq = ATT_BLOCK
    tk = ATT_KV
    nq = seq // tq
    nh = MLA_HEADS
    wuv = w_uv.transpose(1, 0, 2).astype(BF16)
    return pl.pallas_call(
        _mla_attn_kernel,
        grid=(bsz, nq),
        in_specs=[pl.BlockSpec((1, 2 * nh * LANES, tq), lambda b, i: (b * nq + i, 0, 0)),
                  pl.BlockSpec((seq, 2 * LANES), lambda b, i: (b, 0)),
                  pl.BlockSpec((seq // tk, VT_ROWS, tk), lambda b, i: (b, 0, 0)),
                  pl.BlockSpec(wuv.shape, lambda b, i: (0, 0, 0))],
        out_specs=pl.BlockSpec((tq, nh * MLA_V), lambda b, i: (b * nq + i, 0)),
        out_shape=jax.ShapeDtypeStruct((t, nh * MLA_V), BF16),
        scratch_shapes=[pltpu.VMEM((tk, tq), F32), pltpu.VMEM((1, tq), F32), pltpu.VMEM((1, tq), F32),
                        pltpu.VMEM((VT_ROWS, tq), F32)] * nh,
        compiler_params=_cparams("arbitrary", "arbitrary"),
        name="mla_attn",
    )(qt, k, vt, wuv)


def kernel(x, positions, attn_norm, ffn_norm, final_norm, gla_w_in, gla_w_gate, gla_b_gate, gla_out_norm, gla_w_o,
           mla_w_in, mla_q_norm, mla_w_uq, mla_kv_norm, mla_w_uk, mla_w_uv, mla_w_o,
           moe_w_group, moe_w_expert, moe_w1, moe_w3, moe_w2):
    bsz, seq, d = x.shape
    t = bsz * seq
    depth = attn_norm.shape[0]
    xf = x.reshape(t, d)
    pos = positions.reshape(t)
    pending = None
    for i in range(depth):
        j = i // 2
        if i % 2 == 0:
            if pending is not None:
                xf = _combine_plain(pending[0], pending[1], xf, pending[2])
            q, k, v, r, la = _gla_in(xf, attn_norm[i], gla_w_in[j], gla_w_gate[j], gla_b_gate[j])
            mixed = _gla_rec(q, k, v, r, la, gla_out_norm[j], bsz, seq)
            w_o = gla_w_o[j]
        else:
            xf, qt, kc, vt = _mla_in(xf, pos, attn_norm[i], mla_w_in[j], mla_q_norm[j], mla_w_uq[j],
                                     mla_kv_norm[j], mla_w_uk[j], pending)
            mixed = _mla_attn(qt, kc, vt, mla_w_uv[j], bsz, seq)
            w_o = mla_w_o[j]
        xf, h, ids, gates, counts = _post(mixed, w_o, xf, ffn_norm[i], moe_w_group[i], moe_w_expert[i])
        ys, dest = _moe(h, t, ids, counts, moe_w1, moe_w3, moe_w2, i)
        pending = (ys, dest, gates)
    out = _combine_norm(pending[0], pending[1], xf, pending[2], final_norm)
    return out.reshape(bsz, seq, d)
```

```python
import jax
import jax.numpy as jnp
from jax import lax
from jax.experimental import pallas as pl
from jax.experimental.pallas import tpu as pltpu

F32 = jnp.float32
BF16 = jnp.bfloat16
I32 = jnp.int32

EPS = 1e-6
GLA_HEADS = 4
GLA_DK = 128
GLA_DV = 256
GLA_GATE_RANK = 16
GLA_GATE_TAU = 16.0
GLA_CHUNK = 64
MLA_HEADS = 8
MLA_NOPE = 128
MLA_ROPE = 64
MLA_V = 128
MLA_Q_LORA = 256
MLA_KV_LORA = 128
MLA_SCALE = (MLA_NOPE + MLA_ROPE) ** -0.5
Q_SCALE = MLA_SCALE * 1.4426950408889634
VT_ROWS = MLA_KV_LORA + 16
ROPE_THETA = 10000.0
MOE_GROUPS = 8
MOE_PER_GROUP = 8
MOE_EXPERTS = MOE_GROUPS * MOE_PER_GROUP

D_MODEL = 1024
LANES = 128
ROW_SUB = D_MODEL // LANES
VMEM_LIMIT = 48 * 1024 * 1024

TOK_TILE = 512
POST_TILE = 1024
GLA_BLOCK = 256
ATT_BLOCK = 512
ATT_KV = 512
ROW_TILE = 512
EXP_BLOCK = 512


def _cparams(*sem):
    return pltpu.CompilerParams(dimension_semantics=sem, vmem_limit_bytes=VMEM_LIMIT)


def _rms(x, w):
    return x * lax.rsqrt(jnp.mean(x * x, axis=-1, keepdims=True) + EPS) * w


def _dot(a, b):
    return jnp.dot(a, b, preferred_element_type=F32)


def _dot_nt(a, b):
    return lax.dot_general(a, b, (((1,), (1,)), ((), ())), preferred_element_type=F32)


def _dot_tn(a, b):
    return lax.dot_general(a, b, (((0,), (0,)), ((), ())), preferred_element_type=F32)


def _split_bf16(x, pieces):
    out = []
    for _ in range(pieces - 1):
        hi = x.astype(BF16)
        out.append(hi)
        x = x - hi.astype(F32)
    out.append(x.astype(BF16))
    return out


def _gla_in_kernel(x_ref, nw_ref, wq_ref, wk_ref, wv_ref, wr_ref, wa_ref, wgh_ref, wgl_ref, bg_ref,
                   q_ref, k_ref, v_ref, r_ref, la_ref):
    hb = _rms(x_ref[...], nw_ref[...]).astype(BF16)
    a_hi, a_lo = _split_bf16(_dot(hb, wa_ref[...]), 2)
    z = _dot(a_hi, wgh_ref[...]) + _dot(a_hi, wgl_ref[...]) + _dot(a_lo, wgh_ref[...]) + bg_ref[...]
    log_sig = jnp.minimum(z, 0.0) - jnp.log1p(jnp.exp(-jnp.abs(z)))
    la_ref[...] = log_sig * (1.0 / GLA_GATE_TAU)
    q_ref[...] = _dot(hb, wq_ref[...]) * (GLA_DK ** -0.5)
    k_ref[...] = _dot(hb, wk_ref[...])
    v_ref[...] = _dot(hb, wv_ref[...]).astype(BF16)
    r_ref[...] = _dot(hb, wr_ref[...])


def _gla_in(x, nw, w_in, w_gate, b_gate):
    t, d = x.shape
    qk = GLA_HEADS * GLA_DK
    vd = GLA_HEADS * GLA_DV
    wq = w_in[:, :qk].astype(BF16)
    wk = w_in[:, qk:2 * qk].astype(BF16)
    wv = w_in[:, 2 * qk:2 * qk + vd].astype(BF16)
    wa = jnp.pad(w_in[:, 2 * qk + vd:2 * qk + vd + GLA_GATE_RANK], ((0, 0), (0, LANES - GLA_GATE_RANK))).astype(BF16)
    wr = w_in[:, 2 * qk + vd + GLA_GATE_RANK:].astype(BF16)
    wg = jnp.pad(w_gate, ((0, LANES - GLA_GATE_RANK), (0, 0)))
    wgh = wg.astype(BF16)
    wgl = (wg - wgh.astype(F32)).astype(BF16)
    tm = TOK_TILE
    row = lambda n: pl.BlockSpec((tm, n), lambda i: (i, 0))
    full = lambda a: pl.BlockSpec(a.shape, lambda i: (0, 0))
    nw2, bg2 = nw.reshape(1, d), b_gate.reshape(1, qk)
    return pl.pallas_call(
        _gla_in_kernel,
        grid=(t // tm,),
        in_specs=[row(d), full(nw2), full(wq), full(wk), full(wv), full(wr), full(wa), full(wgh), full(wgl),
                  full(bg2)],
        out_specs=[row(qk), row(qk), row(vd), row(vd), row(qk)],
        out_shape=[jax.ShapeDtypeStruct((t, qk), F32), jax.ShapeDtypeStruct((t, qk), F32),
                   jax.ShapeDtypeStruct((t, vd), BF16), jax.ShapeDtypeStruct((t, vd), F32),
                   jax.ShapeDtypeStruct((t, qk), F32)],
        compiler_params=_cparams("arbitrary"),
        name="gla_in",
    )(x, nw2, wq, wk, wv, wr, wa, wgh, wgl, bg2)


def _gla_rec_kernel(q_ref, k_ref, v_ref, r_ref, la_ref, onw_ref, o_ref, st_ref):
    c = GLA_CHUNK
    blk = GLA_BLOCK
    nc = blk // c

    @pl.when(pl.program_id(1) == 0)
    def _():
        st_ref[...] = jnp.zeros_like(st_ref)

    rows = lax.broadcasted_iota(I32, (blk, blk), 0)
    cols = lax.broadcasted_iota(I32, (blk, blk), 1)
    causal = (cols <= rows) & (cols // c == rows // c)
    tri = jnp.where(causal, 1.0, 0.0).astype(BF16)
    g = sum(_dot(tri, piece) for piece in _split_bf16(la_ref[...], 3))
    g_mid = jnp.concatenate(
        [jnp.broadcast_to(g[ci * c + c // 2:ci * c + c // 2 + 1], (c, g.shape[1])) for ci in range(nc)], axis=0)
    g_last = jnp.concatenate(
        [jnp.broadcast_to(g[ci * c + c - 1:ci * c + c], (c, g.shape[1])) for ci in range(nc)], axis=0)
    q = q_ref[...]
    k = k_ref[...]
    q_intra = (q * jnp.exp(g - g_mid)).astype(BF16)
    k_intra = (k * jnp.exp(g_mid - g)).astype(BF16)
    q_inter = (q * jnp.exp(g)).astype(BF16)
    k_state = (k * jnp.exp(g_last - g)).astype(BF16)
    onw = onw_ref[...]
    heads = range(GLA_HEADS)
    ksl = [slice(h * GLA_DK, (h + 1) * GLA_DK) for h in heads]
    vsl = [slice(h * GLA_DV, (h + 1) * GLA_DV) for h in heads]
    v = [v_ref[:, vsl[h]] for h in heads]
    a = [jnp.where(causal, _dot_nt(q_intra[:, ksl[h]], k_intra[:, ksl[h]]), 0.0).astype(BF16) for h in heads]
    kv = [[_dot_tn(v[h][ci * c:(ci + 1) * c], k_state[ci * c:(ci + 1) * c, ksl[h]]) for ci in range(nc)]
          for h in heads]
    o_intra = [_dot(a[h], v[h]) for h in heads]
    for h in heads:
        st = st_ref[h]
        o_inter = []
        for ci in range(nc):
            o_inter.append(_dot_nt(q_inter[ci * c:(ci + 1) * c, ksl[h]], st.astype(BF16)))
            st = st * jnp.exp(g[ci * c + c - 1:ci * c + c, ksl[h]]) + kv[h][ci]
        st_ref[h] = st
        o = _rms(o_intra[h] + jnp.concatenate(o_inter, axis=0), onw)
        r = r_ref[:, vsl[h]]
        o_ref[:, vsl[h]] = (o * (r * jax.nn.sigmoid(r))).astype(BF16)


def _gla_rec(q, k, v, r, la, onw, bsz, seq):
    t = bsz * seq
    qk = GLA_HEADS * GLA_DK
    vd = GLA_HEADS * GLA_DV
    nb = seq // GLA_BLOCK
    row = lambda n: pl.BlockSpec((GLA_BLOCK, n), lambda b, j: (b * nb + j, 0))
    onw2 = onw.reshape(1, GLA_DV)
    return pl.pallas_call(
        _gla_rec_kernel,
        grid=(bsz, nb),
        in_specs=[row(qk), row(qk), row(vd), row(vd), row(qk), pl.BlockSpec((1, GLA_DV), lambda b, j: (0, 0))],
        out_specs=row(vd),
        out_shape=jax.ShapeDtypeStruct((t, vd), BF16),
        scratch_shapes=[pltpu.VMEM((GLA_HEADS, GLA_DV, GLA_DK), F32)],
        compiler_params=_cparams("arbitrary", "arbitrary"),
        name="gla_rec",
    )(q, k, v, r, la, onw2)


def _store_rows(ref, val):
    n, d = val.shape
    ref[...] = val.reshape(n * d // LANES, LANES)


def _load_rows(ref, start, n, nc, dtype):
    return jnp.concatenate([ref[pl.ds(start * nc + c, n, stride=nc), :].astype(dtype) for c in range(nc)], axis=1)


def _post_kernel(a_ref, wo_ref, x_ref, nw_ref, wrh_ref, wrl_ref,
                 x1_ref, h_ref, ids_ref, gates_ref, cnt_ref, carry_ref):
    tm = a_ref.shape[0]
    ne = MOE_EXPERTS

    @pl.when(pl.program_id(0) == 0)
    def _():
        carry_ref[...] = jnp.zeros_like(carry_ref)

    x1 = x_ref[...] + _dot(a_ref[...], wo_ref[...])
    x1_ref[...] = x1
    h = _rms(x1, nw_ref[...])
    _store_rows(h_ref, h)
    h_hi, h_lo = _split_bf16(h, 2)
    lt = _dot_nt(wrh_ref[...], h_hi) + _dot_nt(wrh_ref[...], h_lo) + _dot_nt(wrl_ref[...], h_hi)
    gl = lt[0:MOE_GROUPS]
    gmax = jnp.max(gl, axis=0, keepdims=True)
    gi = lax.broadcasted_iota(I32, gl.shape, 0)
    g_sel = jnp.min(jnp.where(gl == gmax, gi, MOE_GROUPS), axis=0, keepdims=True)
    g_w = 1.0 / jnp.sum(jnp.exp(gl - gmax), axis=0, keepdims=True)
    el = lt[MOE_GROUPS:MOE_GROUPS + ne]
    ei = lax.broadcasted_iota(I32, el.shape, 0)
    in_group = (ei // MOE_PER_GROUP) == g_sel
    neg = jnp.float32(-jnp.inf)
    el1 = jnp.where(in_group, el, neg)
    l1 = jnp.max(el1, axis=0, keepdims=True)
    i1 = jnp.min(jnp.where(el1 == l1, ei, ne), axis=0, keepdims=True)
    el2 = jnp.where(ei == i1, neg, el1)
    l2 = jnp.max(el2, axis=0, keepdims=True)
    i2 = jnp.min(jnp.where(el2 == l2, ei, ne), axis=0, keepdims=True)
    e2 = jnp.exp(l2 - l1)
    gate1 = g_w / (1.0 + e2)
    gate2 = g_w * e2 / (1.0 + e2)
    oh1 = ei == i1
    oh2 = ei == i2
    both = jnp.where(oh1 | oh2, 1.0, 0.0)
    su = lax.broadcasted_iota(I32, (tm, tm), 0)
    tu = lax.broadcasted_iota(I32, (tm, tm), 1)
    upper = jnp.where(su < tu, 1.0, 0.0).astype(BF16)
    prefix = _dot(both.astype(BF16), upper) + carry_ref[:, 0:1]
    rank1 = jnp.sum(jnp.where(oh1, prefix, 0.0), axis=0, keepdims=True)
    rank2 = jnp.sum(jnp.where(oh2, prefix, 0.0), axis=0, keepdims=True)
    carry = carry_ref[...] + jnp.sum(both, axis=1, keepdims=True)
    carry_ref[...] = carry
    cnt_ref[...] = carry.astype(I32)
    ids_ref[...] = jnp.concatenate([i1, i2, rank1.astype(I32), rank2.astype(I32)], axis=0)
    gates_ref[...] = jnp.concatenate([gate1, gate2], axis=0)


def _post(a, wo, x, nw, w_group, w_expert):
    t, d = x.shape
    tm = POST_TILE
    wr = jnp.concatenate([w_group.T, w_expert.T], axis=0)
    wr = jnp.pad(wr, ((0, LANES - wr.shape[0]), (0, 0)))
    wrh = wr.astype(BF16)
    wrl = (wr - wrh.astype(F32)).astype(BF16)
    wo = wo.astype(BF16)
    nw2 = nw.reshape(1, d)
    row = lambda n: pl.BlockSpec((tm, n), lambda i: (i, 0))
    full = lambda arr: pl.BlockSpec(arr.shape, lambda i: (0, 0))
    return pl.pallas_call(
        _post_kernel,
        grid=(t // tm,),
        in_specs=[row(a.shape[1]), full(wo), row(d), full(nw2), full(wrh), full(wrl)],
        out_specs=[row(d), pl.BlockSpec((tm * d // LANES, LANES), lambda i: (i, 0)),
                   pl.BlockSpec((4, tm), lambda i: (0, i)), pl.BlockSpec((2, tm), lambda i: (0, i)),
                   pl.BlockSpec((MOE_EXPERTS, LANES), lambda i: (0, 0))],
        out_shape=[jax.ShapeDtypeStruct((t, d), F32), jax.ShapeDtypeStruct((t * d // LANES, LANES), F32),
                   jax.ShapeDtypeStruct((4, t), I32), jax.ShapeDtypeStruct((2, t), F32),
                   jax.ShapeDtypeStruct((MOE_EXPERTS, LANES), I32)],
        scratch_shapes=[pltpu.VMEM((MOE_EXPERTS, LANES), F32)],
        compiler_params=_cparams("arbitrary"),
        name="post_router",
    )(a, wo, x, nw2, wrh, wrl)


def _row(ref, i):
    return ref.at[pl.ds(pl.multiple_of(i * ROW_SUB, ROW_SUB), ROW_SUB)]


def _zero_fill(zinfo_ref, zero_ref, xs_hbm, sem, wait):
    ne = MOE_EXPERTS
    nblk = xs_hbm.shape[0] // (EXP_BLOCK * ROW_SUB)

    def copy(off_rows, n_rows):
        off = pl.multiple_of(off_rows * ROW_SUB, ROW_SUB)
        cp = pltpu.make_async_copy(zero_ref.at[pl.ds(0, n_rows * ROW_SUB)],
                                   xs_hbm.at[pl.ds(off, n_rows * ROW_SUB)], sem)
        cp.wait() if wait else cp.start()

    def expert(e, carry):
        pad = zinfo_ref[ne + e]
        for b in range(EXP_BLOCK.bit_length() - 1):
            @pl.when(((pad >> b) & 1) == 1)
            def _():
                copy(zinfo_ref[e] + (pad & ((1 << b) - 1)), 1 << b)
        return carry

    def tail(kk, carry):
        blk = zinfo_ref[2 * ne] + kk

        @pl.when(blk < nblk)
        def _():
            copy(blk * EXP_BLOCK, EXP_BLOCK)
        return carry

    lax.fori_loop(0, ne, expert, 0)
    lax.fori_loop(0, ne + 1, tail, 0)


RING = 3
ISSUE_UNROLL = 8


def _slot_copies(dest_hbm, i, idx_ref, slot, sem):
    return [pltpu.make_async_copy(dest_hbm.at[a, i], idx_ref.at[slot, a], sem.at[slot]) for a in range(2)]


def _dispatch_kernel(zinfo_ref, dest_hbm, h_hbm, xs_hbm, idx_ref, hbuf_ref, zero_ref, sem_idx, sem_h, sem_rows):
    i = pl.program_id(0)
    nt = pl.num_programs(0)
    tr = ROW_TILE
    rows = tr * ROW_SUB

    def tile_copies(j):
        slot = j % RING
        src = h_hbm.at[pl.ds(pl.multiple_of(j * rows, rows), rows)]
        return [pltpu.make_async_copy(src, hbuf_ref.at[slot], sem_h.at[slot])] + \
            _slot_copies(dest_hbm, j, idx_ref, slot, sem_idx)

    def scatter_rows(slot):
        def body(t, carry):
            src = _row(hbuf_ref.at[slot], t)
            pltpu.make_async_copy(src, _row(xs_hbm, idx_ref[slot, 0, t]), sem_rows.at[slot]).start()
            pltpu.make_async_copy(src, _row(xs_hbm, idx_ref[slot, 1, t]), sem_rows.at[slot]).start(priority=1)
            return carry
        lax.fori_loop(0, tr, body, 0, unroll=ISSUE_UNROLL)

    def wait_rows(slot):
        def body(t, carry):
            pltpu.make_async_copy(_row(hbuf_ref.at[0], 0), _row(xs_hbm, 0), sem_rows.at[slot]).wait()
            return carry
        lax.fori_loop(0, 2 * tr, body, 0, unroll=ISSUE_UNROLL)

    @pl.when(i == 0)
    def _():
        for cp in tile_copies(0):
            cp.start()
        zero_ref[...] = jnp.zeros_like(zero_ref)
        _zero_fill(zinfo_ref, zero_ref, xs_hbm, sem_rows.at[1], wait=False)
        _zero_fill(zinfo_ref, zero_ref, xs_hbm, sem_rows.at[1], wait=True)

    @pl.when(i + 1 < nt)
    def _():
        for cp in tile_copies(i + 1):
            cp.start()

    for cp in tile_copies(i):
        cp.wait()

    for slot in range(RING):
        @pl.when(i % RING == slot)
        def _():
            scatter_rows(slot)

            @pl.when(i > 0)
            def _():
                wait_rows((slot - 1) % RING)

            @pl.when(i == nt - 1)
            def _():
                wait_rows(slot)


def _dispatch(h, dest, zoff, n_slots):
    t = h.shape[0] // ROW_SUB
    tr = ROW_TILE
    nt = t // tr
    grid_spec = pltpu.PrefetchScalarGridSpec(
        num_scalar_prefetch=1,
        grid=(nt,),
        in_specs=[pl.BlockSpec(memory_space=pl.ANY), pl.BlockSpec(memory_space=pl.ANY)],
        out_specs=pl.BlockSpec(memory_space=pl.ANY),
        scratch_shapes=[pltpu.SMEM((RING, 2, tr), I32), pltpu.VMEM((RING, tr * ROW_SUB, LANES), F32),
                        pltpu.VMEM((EXP_BLOCK * ROW_SUB, LANES), F32),
                        pltpu.SemaphoreType.DMA((RING,)), pltpu.SemaphoreType.DMA((RING,)),
                        pltpu.SemaphoreType.DMA((RING,))],
    )
    return pl.pallas_call(
        _dispatch_kernel,
        grid_spec=grid_spec,
        out_shape=jax.ShapeDtypeStruct((n_slots * ROW_SUB, LANES), F32),
        compiler_params=_cparams("arbitrary"),
        name="moe_dispatch",
    )(zoff, dest.reshape(2, nt, tr), h)


def _expert_kernel(be_ref, nused_ref, xs_ref, w1_ref, w3_ref, w2_ref, ys_ref, w1b_ref, w3b_ref, w2b_ref):
    i = pl.program_id(0)
    used = i < nused_ref[0]

    @pl.when(used & ((i == 0) | (be_ref[i] != be_ref[jnp.maximum(i - 1, 0)])))
    def _():
        w1b_ref[...] = w1_ref[0, 0].astype(BF16)
        w3b_ref[...] = w3_ref[0, 0].astype(BF16)
        w2b_ref[...] = w2_ref[0, 0].astype(BF16)

    @pl.when(used)
    def _():
        xb = _load_rows(xs_ref, 0, EXP_BLOCK, ROW_SUB, BF16)
        a = _dot(xb, w1b_ref[...])
        b = _dot(xb, w3b_ref[...])
        hid = (a * jax.nn.sigmoid(a) * b).astype(BF16)
        _store_rows(ys_ref, _dot(hid, w2b_ref[...]))

    @pl.when(i >= nused_ref[0])
    def _():
        ys_ref[...] = jnp.zeros_like(ys_ref)


def _experts(xs, block_e, n_used, w1, w3, w2, layer):
    bm = EXP_BLOCK
    n_slots = xs.shape[0] // ROW_SUB
    nblk = n_slots // bm
    d, ff = w1.shape[-2:]

    def last_used(i, nu):
        return jnp.maximum(jnp.minimum(i, nu[0] - 1), 0)

    def xmap(i, be, nu):
        return (last_used(i, nu), 0)

    def wmap(i, be, nu):
        return (layer, be[last_used(i, nu)], 0, 0)

    grid_spec = pltpu.PrefetchScalarGridSpec(
        num_scalar_prefetch=2,
        grid=(nblk,),
        in_specs=[pl.BlockSpec((bm * ROW_SUB, LANES), xmap), pl.BlockSpec((1, 1, d, ff), wmap),
                  pl.BlockSpec((1, 1, d, ff), wmap), pl.BlockSpec((1, 1, ff, d), wmap)],
        out_specs=pl.BlockSpec((bm * ROW_SUB, LANES), lambda i, be, nu: (i, 0)),
        scratch_shapes=[pltpu.VMEM((d, ff), BF16), pltpu.VMEM((d, ff), BF16), pltpu.VMEM((ff, d), BF16)],
    )
    return pl.pallas_call(
        _expert_kernel,
        grid_spec=grid_spec,
        out_shape=jax.ShapeDtypeStruct((n_slots * ROW_SUB, LANES), F32),
        compiler_params=_cparams("arbitrary"),
        name="moe_experts",
    )(block_e, n_used, xs, w1, w3, w2)


def _combine_tiles(dest_hbm, ys_hbm, x_ref, g_ref, o_ref, idx_ref, buf_ref, sem_idx, sem_rows, finish):
    i = pl.program_id(0)
    nt = pl.num_programs(0)
    tr = x_ref.shape[0]

    def gather(slot):
        buf = buf_ref.at[slot]

        def body(t, carry):
            pltpu.make_async_copy(_row(ys_hbm, idx_ref[slot, 0, t]), _row(buf, t), sem_rows.at[slot]).start()
            pltpu.make_async_copy(_row(ys_hbm, idx_ref[slot, 1, t]), _row(buf, tr + t),
                                  sem_rows.at[slot]).start(priority=1)
            return carry

        lax.fori_loop(0, tr, body, 0, unroll=ISSUE_UNROLL)

    def wait_rows(slot):
        def body(t, carry):
            pltpu.make_async_copy(_row(ys_hbm, 0), _row(buf_ref.at[0], 0), sem_rows.at[slot]).wait()
            return carry
        lax.fori_loop(0, 2 * tr, body, 0, unroll=ISSUE_UNROLL)

    @pl.when(i == 0)
    def _():
        for cp in _slot_copies(dest_hbm, 0, idx_ref, 0, sem_idx):
            cp.start()
        for cp in _slot_copies(dest_hbm, 0, idx_ref, 0, sem_idx):
            cp.wait()
        gather(0)

        @pl.when(nt > 1)
        def _():
            for cp in _slot_copies(dest_hbm, 1, idx_ref, 1, sem_idx):
                cp.start()

    for slot in range(2):
        @pl.when(i % 2 == slot)
        def _():
            nxt = 1 - slot

            @pl.when(i + 1 < nt)
            def _():
                for cp in _slot_copies(dest_hbm, i + 1, idx_ref, nxt, sem_idx):
                    cp.wait()
                gather(nxt)

            @pl.when(i + 2 < nt)
            def _():
                for cp in _slot_copies(dest_hbm, i + 2, idx_ref, slot, sem_idx):
                    cp.start()

            wait_rows(slot)
            g = g_ref[...]
            y1 = _load_rows(buf_ref.at[slot], 0, tr, ROW_SUB, F32)
            y2 = _load_rows(buf_ref.at[slot], tr, tr, ROW_SUB, F32)
            o_ref[...] = finish(x_ref[...] + g[:, 0:1] * y1 + g[:, 1:2] * y2)


def _combine_scratch(tr):
    return [pltpu.SMEM((2, 2, tr), I32), pltpu.VMEM((2, 2 * tr * ROW_SUB, LANES), F32),
            pltpu.SemaphoreType.DMA((2,)), pltpu.SemaphoreType.DMA((2,))]


def _combine_norm_kernel(dest_hbm, ys_hbm, x_ref, g_ref, nw_ref, o_ref, *scratch):
    _combine_tiles(dest_hbm, ys_hbm, x_ref, g_ref, o_ref, *scratch, finish=lambda v: _rms(v, nw_ref[...]))


def _combine_norm(ys, dest, x, gates, nw):
    t, d = x.shape
    tr = ROW_TILE
    nt = t // tr
    return pl.pallas_call(
        _combine_norm_kernel,
        grid=(nt,),
        in_specs=[pl.BlockSpec(memory_space=pl.ANY), pl.BlockSpec(memory_space=pl.ANY),
                  pl.BlockSpec((tr, d), lambda i: (i, 0)), pl.BlockSpec((tr, 2), lambda i: (i, 0)),
                  pl.BlockSpec((1, d), lambda i: (0, 0))],
        out_specs=pl.BlockSpec((tr, d), lambda i: (i, 0)),
        out_shape=jax.ShapeDtypeStruct((t, d), F32),
        scratch_shapes=_combine_scratch(tr),
        compiler_params=_cparams("arbitrary"),
        name="moe_combine",
    )(dest.reshape(2, nt, tr), ys, x, gates.T, nw.reshape(1, d))


def _slots_kernel(ids_ref, pst_ref, dest_ref):
    ids = ids_ref[...]
    ei = lax.broadcasted_iota(I32, (MOE_EXPERTS, ids.shape[1]), 0)
    pst = pst_ref[...]
    d1 = jnp.sum(jnp.where(ei == ids[0:1], pst, 0), axis=0, keepdims=True) + ids[2:3]
    d2 = jnp.sum(jnp.where(ei == ids[1:2], pst, 0), axis=0, keepdims=True) + ids[3:4]
    dest_ref[...] = jnp.concatenate([d1, d2], axis=0)


def _slots(ids, pstarts):
    t = ids.shape[1]
    tm = TOK_TILE
    return pl.pallas_call(
        _slots_kernel,
        grid=(t // tm,),
        in_specs=[pl.BlockSpec((4, tm), lambda i: (0, i)), pl.BlockSpec((MOE_EXPERTS, 1), lambda i: (0, 0))],
        out_specs=pl.BlockSpec((2, tm), lambda i: (0, i)),
        out_shape=jax.ShapeDtypeStruct((2, t), I32),
        compiler_params=_cparams("arbitrary"),
        name="moe_slots",
    )(ids, pstarts.reshape(MOE_EXPERTS, 1))


def _combine_plain_kernel(dest_hbm, ys_hbm, x_ref, g_ref, o_ref, *scratch):
    _combine_tiles(dest_hbm, ys_hbm, x_ref, g_ref, o_ref, *scratch, finish=lambda v: v)


def _combine_plain(ys, dest, x, gates):
    t, d = x.shape
    tr = ROW_TILE
    nt = t // tr
    return pl.pallas_call(
        _combine_plain_kernel,
        grid=(nt,),
        in_specs=[pl.BlockSpec(memory_space=pl.ANY), pl.BlockSpec(memory_space=pl.ANY),
                  pl.BlockSpec((tr, d), lambda i: (i, 0)), pl.BlockSpec((tr, 2), lambda i: (i, 0))],
        out_specs=pl.BlockSpec((tr, d), lambda i: (i, 0)),
        out_shape=jax.ShapeDtypeStruct((t, d), F32),
        scratch_shapes=_combine_scratch(tr),
        compiler_params=_cparams("arbitrary"),
        name="moe_combine",
    )(dest.reshape(2, nt, tr), ys, x, gates.T)


def _moe(h, t, ids, counts, w1, w3, w2, layer):
    bm = EXP_BLOCK
    n_asg = 2 * t
    n_slots = n_asg + MOE_EXPERTS * bm
    cnt = counts[:, 0]
    padded = (cnt + bm - 1) // bm * bm
    pend = jnp.cumsum(padded)
    pstarts = (pend - padded).astype(I32)
    dest = _slots(ids, pstarts)
    nblk = n_slots // bm
    block_start = jnp.arange(nblk, dtype=I32) * bm
    block_e = jnp.minimum(jnp.sum(pend[None, :] <= block_start[:, None], axis=1), MOE_EXPERTS - 1).astype(I32)
    n_used = (pend[-1:] // bm).astype(I32)
    zinfo = jnp.concatenate([pstarts + cnt, padded - cnt, n_used]).astype(I32)
    xs = _dispatch(h, dest, zinfo, n_slots)
    ys = _experts(xs, block_e, n_used, w1, w3, w2, layer)
    return ys, dest


def _mla_in_combine_kernel(dest_hbm, ys_hbm, g_ref, x_ref, *rest):
    ins, (xc_ref, qt_ref, k_ref, vt_ref), scratch = rest[:12], rest[12:16], rest[16:]
    _combine_tiles(dest_hbm, ys_hbm, x_ref, g_ref, xc_ref, *scratch, finish=lambda v: v)
    _mla_in_kernel(xc_ref, *ins, qt_ref, k_ref, vt_ref)


def _mla_in_kernel(x_ref, pos_ref, nw_ref, wq_ref, wkv_ref, wks_ref, qn_ref, kvn_ref, wqn_ref, wqr_ref, wqs_ref,
                   wuk_ref, freq_ref, qt_ref, k_ref, vt_ref):
    tm = x_ref.shape[0]
    tb = ATT_BLOCK
    hb = _rms(x_ref[...], nw_ref[...]).astype(BF16)
    c_q = _rms(_dot(hb, wq_ref[...]), qn_ref[...]).astype(BF16)
    kv_t = _dot_nt(wkv_ref[...], hb)
    ks_t = _dot_nt(wks_ref[...], hb)
    c_kv = kv_t[:MLA_KV_LORA]
    c_kv = c_kv * lax.rsqrt(jnp.mean(c_kv * c_kv, axis=0, keepdims=True) + EPS) * kvn_ref[...]
    ang = freq_ref[...] * pos_ref[...].astype(F32)
    cos32, sin32 = jnp.cos(ang), jnp.sin(ang)
    cos = jnp.concatenate([cos32] * 4, axis=0)
    sin = jnp.concatenate([-sin32, sin32] * 2, axis=0)
    k_rope = kv_t[MLA_KV_LORA:] * cos + ks_t * sin
    k_ref[...] = jnp.concatenate([c_kv, k_rope], axis=0).T.astype(BF16)
    v_ext = jnp.concatenate([c_kv, jnp.ones((VT_ROWS - MLA_KV_LORA, tm), F32)], axis=0).astype(BF16)
    for c in range(tm // ATT_KV):
        vt_ref[c] = v_ext[:, c * ATT_KV:(c + 1) * ATT_KV]
    q_nope = _dot_nt(wqn_ref[...], c_q).astype(BF16)
    q_r = _dot_nt(wqr_ref[...], c_q)
    q_s = _dot_nt(wqs_ref[...], c_q)
    for h in range(MLA_HEADS):
        sl = slice(h * LANES, (h + 1) * LANES)
        q_lat = (_dot(wuk_ref[h], q_nope[sl]) * Q_SCALE).astype(BF16)
        q_rope = ((q_r[sl] * cos + q_s[sl] * sin) * Q_SCALE).astype(BF16)
        for c in range(tm // tb):
            cs = slice(c * tb, (c + 1) * tb)
            qt_ref[c, 2 * h * LANES:(2 * h + 1) * LANES, :] = q_lat[:, cs]
            qt_ref[c, (2 * h + 1) * LANES:(2 * h + 2) * LANES, :] = q_rope[:, cs]


def _swap_halves(w):
    half = w.shape[-1] // 2
    return jnp.concatenate([w[..., half:], w[..., :half]], axis=-1)


def _mla_in(x, pos, nw, w_in, q_norm, w_uq, kv_norm, w_uk, pending=None):
    t, d = x.shape
    tm = TOK_TILE
    tb = ATT_BLOCK
    nh = MLA_HEADS
    pad_r = LANES - MLA_ROPE
    w_kr = w_in[:, MLA_Q_LORA + MLA_KV_LORA:]
    wq = w_in[:, :MLA_Q_LORA].astype(BF16)
    wkv = jnp.pad(w_in[:, MLA_Q_LORA:], ((0, 0), (0, pad_r))).T.astype(BF16)
    wks = jnp.pad(_swap_halves(w_kr), ((0, 0), (0, pad_r))).T.astype(BF16)
    wqn = w_uq[:, :, :MLA_NOPE].reshape(MLA_Q_LORA, nh * MLA_NOPE).T.astype(BF16)
    w_r = w_uq[:, :, MLA_NOPE:]
    wqr = jnp.pad(w_r, ((0, 0), (0, 0), (0, pad_r))).reshape(MLA_Q_LORA, nh * LANES).T.astype(BF16)
    wqs = jnp.pad(_swap_halves(w_r), ((0, 0), (0, 0), (0, pad_r))).reshape(MLA_Q_LORA, nh * LANES).T.astype(BF16)
    wuk = w_uk.transpose(1, 0, 2).astype(BF16)
    inv_freq = 1.0 / (ROPE_THETA ** (jnp.arange(0, MLA_ROPE, 2, dtype=F32) / MLA_ROPE))
    full = lambda a: pl.BlockSpec(a.shape, lambda i: (0,) * a.ndim)
    args = (x, pos.reshape(1, t), nw.reshape(1, d), wq, wkv, wks, q_norm.reshape(1, -1), kv_norm.reshape(-1, 1),
            wqn, wqr, wqs, wuk, inv_freq.reshape(-1, 1))
    nsub = tm // tb
    nt = t // tm
    in_specs = [pl.BlockSpec((tm, d), lambda i: (i, 0)), pl.BlockSpec((1, tm), lambda i: (0, i))] \
        + [full(a) for a in args[2:]]
    out_specs = [pl.BlockSpec((nsub, 2 * nh * LANES, tb), lambda i: (i, 0, 0)),
                 pl.BlockSpec((tm, 2 * LANES), lambda i: (i, 0)),
                 pl.BlockSpec((tm // ATT_KV, VT_ROWS, ATT_KV), lambda i: (i, 0, 0))]
    out_shape = [jax.ShapeDtypeStruct((t // tb, 2 * nh * LANES, tb), BF16),
                 jax.ShapeDtypeStruct((t, 2 * LANES), BF16),
                 jax.ShapeDtypeStruct((t // ATT_KV, VT_ROWS, ATT_KV), BF16)]
    if pending is None:
        qt, k, vt = pl.pallas_call(
            _mla_in_kernel, grid=(nt,), in_specs=in_specs, out_specs=out_specs, out_shape=out_shape,
            compiler_params=_cparams("arbitrary"), name="mla_in",
        )(*args)
        return x, qt, k, vt
    ys, dest, gates = pending
    any_spec = pl.BlockSpec(memory_space=pl.ANY)
    return pl.pallas_call(
        _mla_in_combine_kernel,
        grid=(nt,),
        in_specs=[any_spec, any_spec, pl.BlockSpec((tm, 2), lambda i: (i, 0))] + in_specs,
        out_specs=[pl.BlockSpec((tm, d), lambda i: (i, 0))] + out_specs,
        out_shape=[jax.ShapeDtypeStruct((t, d), F32)] + out_shape,
        scratch_shapes=_combine_scratch(tm),
        compiler_params=_cparams("arbitrary"),
        name="mla_in_combine",
    )(dest.reshape(2, nt, tm), ys, gates.T, *args)


def _mla_attn_kernel(qt_ref, k_ref, vt_ref, wuv_ref, o_ref, *state):
    i = pl.program_id(1)
    tq = ATT_BLOCK
    tk = ATT_KV
    nh = MLA_HEADS
    dq = 2 * LANES
    s_refs, smax_refs, m_refs, acc_refs = state[0::4], state[1::4], state[2::4], state[3::4]
    last = (i * tq) // tk

    def scores(j, h):
        kb = k_ref[pl.ds(pl.multiple_of(j * tk, tk), tk), :]
        s = _dot(kb, qt_ref[0, h * dq:(h + 1) * dq, :])
        return s, jnp.max(s, axis=0, keepdims=True)

    def consume(j, h, s, s_max):
        m_old = m_refs[h][...]
        m_new = jnp.maximum(m_old, s_max)
        alpha = jnp.exp2(m_old - m_new)
        p = jnp.exp2(s - m_new).astype(BF16)
        acc_refs[h][...] = alpha * acc_refs[h][...] + _dot(vt_ref[j], p)
        m_refs[h][...] = m_new

    for h in range(nh):
        m_refs[h][...] = jnp.full_like(m_refs[h], -jnp.inf)
        acc_refs[h][...] = jnp.zeros_like(acc_refs[h])
        s_refs[h][...], smax_refs[h][...] = scores(0, h)

    def full_block(j, carry):
        for h in range(nh):
            s, s_max = s_refs[h][...], smax_refs[h][...]
            s_new, s_max_new = scores(j + 1, h)
            consume(j, h, s, s_max)
            s_refs[h][...], smax_refs[h][...] = s_new, s_max_new
        return carry

    lax.fori_loop(0, last, full_block, 0)
    kpos = lax.broadcasted_iota(I32, (tk, tq), 0) + last * tk
    qpos = lax.broadcasted_iota(I32, (tk, tq), 1) + i * tq
    for h in range(nh):
        s = jnp.where(kpos <= qpos, s_refs[h][...], -jnp.inf)
        consume(last, h, s, jnp.max(s, axis=0, keepdims=True))
    for h in range(nh):
        acc = acc_refs[h][...]
        o_lat = (acc[:MLA_KV_LORA] / acc[MLA_KV_LORA:MLA_KV_LORA + 1]).astype(BF16)
        o_ref[:, h * MLA_V:(h + 1) * MLA_V] = _dot_tn(o_lat, wuv_ref[h]).astype(BF16)


def _mla_attn(qt, k, vt, w_uv, bsz, seq):
    t = bsz * seq
    tq = ATT_BLOCK
    tk = ATT_KV
    nq = seq // tq
    nh = MLA_HEADS
    wuv = w_uv.transpose(1, 0, 2).astype(BF16)
    return pl.pallas_call(
        _mla_attn_kernel,
        grid=(bsz, nq),
        in_specs=[pl.BlockSpec((1, 2 * nh * LANES, tq), lambda b, i: (b * nq + i, 0, 0)),
                  pl.BlockSpec((seq, 2 * LANES), lambda b, i: (b, 0)),
                  pl.BlockSpec((seq // tk, VT_ROWS, tk), lambda b, i: (b, 0, 0)),
                  pl.BlockSpec(wuv.shape, lambda b, i: (0, 0, 0))],
        out_specs=pl.BlockSpec((tq, nh * MLA_V), lambda b, i: (b * nq + i, 0)),
        out_shape=jax.ShapeDtypeStruct((t, nh * MLA_V), BF16),
        scratch_shapes=[pltpu.VMEM((tk, tq), F32), pltpu.VMEM((1, tq), F32), pltpu.VMEM((1, tq), F32),
                        pltpu.VMEM((VT_ROWS, tq), F32)] * nh,
        compiler_params=_cparams("arbitrary", "arbitrary"),
        name="mla_attn",
    )(qt, k, vt, wuv)


def kernel(x, positions, attn_norm, ffn_norm, final_norm, gla_w_in, gla_w_gate, gla_b_gate, gla_out_norm, gla_w_o,
           mla_w_in, mla_q_norm, mla_w_uq, mla_kv_norm, mla_w_uk, mla_w_uv, mla_w_o,
           moe_w_group, moe_w_expert, moe_w1, moe_w3, moe_w2):
    bsz, seq, d = x.shape
    t = bsz * seq
    depth = attn_norm.shape[0]
    xf = x.reshape(t, d)
    pos = positions.reshape(t)
    pending = None
    for i in range(depth):
        j = i // 2
        if i % 2 == 0:
            if pending is not None:
                xf = _combine_plain(pending[0], pending[1], xf, pending[2])
            q, k, v, r, la = _gla_in(xf, attn_norm[i], gla_w_in[j], gla_w_gate[j], gla_b_gate[j])
            mixed = _gla_rec(q, k, v, r, la, gla_out_norm[j], bsz, seq)
            w_o = gla_w_o[j]
        else:
            xf, qt, kc, vt = _mla_in(xf, pos, attn_norm[i], mla_w_in[j], mla_q_norm[j], mla_w_uq[j],
                                     mla_kv_norm[j], mla_w_uk[j], pending)
            mixed = _mla_attn(qt, kc, vt, mla_w_uv[j], bsz, seq)
            w_o = mla_w_o[j]
        xf, h, ids, gates, counts = _post(mixed, w_o, xf, ffn_norm[i], moe_w_group[i], moe_w_expert[i])
        ys, dest = _moe(h, t, ids, counts, moe_w1, moe_w3, moe_w2, i)
        pending = (ys, dest, gates)
    out = _combine_norm(pending[0], pending[1], xf, pending[2], final_norm)
    return out.reshape(bsz, seq, d)
```
